```python
import jax, jax.numpy as jnp
from jax import lax
import numpy as np

D_MODEL = 2048
BATCH = 2
SEQ = 8192
DEPTH = 1
DEC_BATCH = 32
DEC_SEQ = 1
PAST_LEN = 16384
PAGE_SIZE = 128

N_MEM = 256
EPS = 1e-6
ROPE_THETA = 10000.0
NEG = -1e30

GLA_HEADS = 4
GLA_DK = D_MODEL // 16
GLA_DV = D_MODEL // 8
GLA_RANK = 16
GLA_TAU = 16.0
GLA_CHUNK = 64
GLA_KW = GLA_HEADS * GLA_DK
GLA_VW = GLA_HEADS * GLA_DV

SWA_PATTERNS = ((128, 1), (512, 4), (2048, 16))
SWA_GROUPS = 3
SWA_HEADS = 4
SWA_HD = D_MODEL // 16
SWA_W = SWA_HEADS * SWA_HD
SWA_BLK = 128

MEM_HEADS = 4
MEM_HD = D_MODEL // 16
MEM_W = MEM_HEADS * MEM_HD

N_BRANCH = 3
IN_SIZES = (GLA_KW, GLA_KW, GLA_VW, GLA_VW, GLA_RANK,
            SWA_GROUPS * SWA_W, SWA_GROUPS * SWA_W, SWA_GROUPS * SWA_W, SWA_W,
            MEM_W, MEM_W, N_BRANCH * D_MODEL)
IN_TOTAL = 2 * GLA_KW + 2 * GLA_VW + GLA_RANK + 3 * SWA_GROUPS * SWA_W + SWA_W + 2 * MEM_W + N_BRANCH * D_MODEL

kernel_name = "gated_branch_gla_dilated_swa_memory_decoder_step"


def _rmsnorm(x, g):
    xf = x.astype(jnp.float32)
    y = xf * lax.rsqrt(jnp.mean(xf * xf, axis=-1, keepdims=True) + EPS)
    return (y * g.astype(jnp.float32)).astype(x.dtype)


def _heads(t, n_heads, hd):
    return t.reshape(t.shape[0], t.shape[1], n_heads, hd)


def _rope(x, pos):
    half = x.shape[-1] // 2
    inv = ROPE_THETA ** (-jnp.arange(half, dtype=jnp.float32) / half)
    ang = pos.astype(jnp.float32)[:, None] * inv[None, :]
    cos = jnp.cos(ang)[None, :, None, :]
    sin = jnp.sin(ang)[None, :, None, :]
    xf = x.astype(jnp.float32)
    x1, x2 = xf[..., :half], xf[..., half:]
    return jnp.concatenate([x1 * cos - x2 * sin, x2 * cos + x1 * sin], axis=-1).astype(x.dtype)


def _gla(q, k, v, log_a, s0):
    B, L, H, DK = q.shape
    DV = v.shape[-1]
    C = min(GLA_CHUNK, L)
    Lp = -(-L // C) * C
    n = Lp // C
    padw = ((0, 0), (0, Lp - L), (0, 0), (0, 0))

    def chunks(t):
        t = jnp.pad(t.astype(jnp.float32), padw)
        return t.reshape(B, n, C, H, t.shape[-1]).transpose(1, 0, 3, 2, 4)

    qc = chunks(q) * (DK ** -0.5)
    kc, vc, ac = chunks(k), chunks(v), chunks(log_a)
    causal = jnp.tril(jnp.ones((C, C), dtype=bool))[:, :, None]

    def step(S, inp):
        qi, ki, vi, ai = inp
        b = jnp.cumsum(ai, axis=2)
        diff = b[:, :, :, None, :] - b[:, :, None, :, :]
        decay = jnp.exp(jnp.where(causal, diff, -jnp.inf))
        A = jnp.einsum('bhtk,bhsk,bhtsk->bhts', qi, ki, decay)
        o = jnp.einsum('bhts,bhsv->bhtv', A, vi) + jnp.einsum('bhtk,bhkv->bhtv', qi * jnp.exp(b), S)
        b_end = b[:, :, -1:, :]
        S_new = jnp.exp(b_end[:, :, 0, :])[..., None] * S + jnp.einsum('bhsk,bhsv->bhkv', ki * jnp.exp(b_end - b), vi)
        return S_new, o

    S, o = lax.scan(step, s0.astype(jnp.float32), (qc, kc, vc, ac))
    o = o.transpose(1, 0, 3, 2, 4).reshape(B, Lp, H, DV)[:, :L]
    return o, S


def _dilated_prompt(q, k, v, window, dil):
    B, S, H, E = q.shape
    n_keys = window // dil
    blk = SWA_BLK
    unit = dil * blk
    Sp = -(-S // unit) * unit
    nb = Sp // unit
    padw = ((0, 0), (0, Sp - S), (0, 0), (0, 0))

    def split(t):
        t = jnp.pad(t.astype(jnp.float32), padw)
        return t.reshape(B, nb * blk, dil, H, E).transpose(0, 2, 1, 3, 4).reshape(B, dil, nb, blk, H, E)

    def with_prev(t):
        prev = jnp.concatenate([jnp.zeros_like(t[:, :, :1]), t[:, :, :-1]], axis=2)
        return jnp.concatenate([prev, t], axis=3)

    qb = split(q) * (E ** -0.5)
    kk, vv = with_prev(split(k)), with_prev(split(v))
    s = jnp.einsum('brnqhe,brnkhe->brnhqk', qb, kk)
    qi = jnp.arange(blk)[:, None]
    ki = jnp.arange(2 * blk)[None, :]
    delta = blk + qi - ki
    band = (delta >= 0) & (delta <= n_keys)
    valid = band[None] & ((jnp.arange(nb)[:, None, None] > 0) | (ki[None] >= blk))
    s = jnp.where(valid[None, None, :, None], s, NEG)
    m = jnp.max(s, axis=-1, keepdims=True)
    p = jnp.exp(s - m)
    den = jnp.sum(p, axis=-1)
    o = jnp.einsum('brnhqk,brnkhe->brnqhe', p, vv) / den.transpose(0, 1, 2, 4, 3)[..., None]
    lse = (m[..., 0] + jnp.log(den)).transpose(0, 1, 2, 4, 3)
    o = o.reshape(B, dil, nb * blk, H, E).transpose(0, 2, 1, 3, 4).reshape(B, Sp, H, E)[:, :S]
    lse = lse.reshape(B, dil, nb * blk, H).transpose(0, 2, 1, 3).reshape(B, Sp, H)[:, :S]
    return o, lse


def _dilated_step(q, k_new, v_new, buf, window, dil):
    Lb = buf.shape[1]
    L, E = q.shape[1], q.shape[-1]
    n_keys = window // dil
    kv_all = jnp.concatenate([buf, jnp.stack([k_new, v_new], axis=2)], axis=1)
    idx = Lb + jnp.arange(L)[:, None] - dil * jnp.arange(n_keys + 1)[None, :]
    valid = idx >= 0
    g = kv_all[:, jnp.maximum(idx, 0)].astype(jnp.float32)
    s = jnp.einsum('blhe,blmhe->blhm', q.astype(jnp.float32) * (E ** -0.5), g[:, :, :, 0])
    s = jnp.where(valid[None, :, None, :], s, NEG)
    m = jnp.max(s, axis=-1, keepdims=True)
    p = jnp.exp(s - m)
    den = jnp.sum(p, axis=-1)
    o = jnp.einsum('blhm,blmhe->blhe', p, g[:, :, :, 1]) / den[..., None]
    lse = m[..., 0] + jnp.log(den)
    return o, lse, kv_all[:, L:]


def _mem_kv(mem, g_mem, w_mem_kv):
    kv = _rmsnorm(mem, g_mem) @ w_mem_kv
    return kv.reshape(mem.shape[0], mem.shape[1], 2, MEM_HEADS, MEM_HD)


def _layer(x, mem_kv, gla_s0, swa_bufs, pos0, g_norm, w_in, w_alpha2, b_alpha, g_gla_out,
           w_proj_a, w_proj_b, w_proj_c, w_out):
    nbat, L, _ = x.shape
    h = _rmsnorm(x, g_norm)
    z = h @ w_in
    cuts = [int(c) for c in np.cumsum(IN_SIZES)[:-1]]
    gq, gk, gv, gr, ga, sq, sk, sv, sr, mq, mr, gts = jnp.split(z, cuts, axis=-1)

    log_a = jax.nn.log_sigmoid((ga @ w_alpha2 + b_alpha).astype(jnp.float32)) / GLA_TAU
    o_a, gla_s = _gla(_heads(gq, GLA_HEADS, GLA_DK), _heads(gk, GLA_HEADS, GLA_DK),
                      _heads(gv, GLA_HEADS, GLA_DV), _heads(log_a, GLA_HEADS, GLA_DK), gla_s0)
    o_a = _rmsnorm(o_a, g_gla_out.reshape(GLA_HEADS, GLA_DV)).reshape(nbat, L, GLA_VW).astype(x.dtype)
    y_a = (o_a * jax.nn.silu(gr)) @ w_proj_a

    pos = pos0 + jnp.arange(L, dtype=jnp.int32)
    n_sh = SWA_GROUPS * SWA_HEADS
    q_b = _rope(_heads(sq, n_sh, SWA_HD), pos)
    k_b = _rope(_heads(sk, n_sh, SWA_HD), pos)
    v_b = _heads(sv, n_sh, SWA_HD)
    outs, lses, new_bufs = [], [], []
    for gi, (win, dil) in enumerate(SWA_PATTERNS):
        hsl = slice(gi * SWA_HEADS, (gi + 1) * SWA_HEADS)
        qg, kg, vg = q_b[:, :, hsl], k_b[:, :, hsl], v_b[:, :, hsl]
        if swa_bufs is None:
            o, lse = _dilated_prompt(qg, kg, vg, win, dil)
            buf = jnp.stack([kg, vg], axis=2)[:, L - min(win, L):]
        else:
            o, lse, buf = _dilated_step(qg, kg, vg, swa_bufs[gi], win, dil)
        outs.append(o)
        lses.append(lse)
        new_bufs.append(buf)
    w_grp = jax.nn.softmax(jnp.stack(lses, axis=0), axis=0)[..., None]
    o_b = jnp.sum(w_grp * jnp.stack(outs, axis=0), axis=0).reshape(nbat, L, SWA_W).astype(x.dtype)
    y_b = (o_b * jax.nn.silu(sr)) @ w_proj_b

    qm = _heads(mq, MEM_HEADS, MEM_HD).astype(jnp.float32) * (MEM_HD ** -0.5)
    sc = jnp.einsum('blhe,bmhe->bhlm', qm, mem_kv[:, :, 0].astype(jnp.float32))
    pm = jax.nn.softmax(sc, axis=-1)
    o_c = jnp.einsum('bhlm,bmhe->blhe', pm, mem_kv[:, :, 1].astype(jnp.float32)).reshape(nbat, L, MEM_W).astype(x.dtype)
    y_c = (o_c * jax.nn.silu(mr)) @ w_proj_c

    g_a, g_b, g_c = jnp.split(jax.nn.sigmoid(gts), N_BRANCH, axis=-1)
    x = x + (g_a * y_a + g_b * y_b + g_c * y_c) @ w_out
    return x, gla_s, new_bufs


def setup_inputs(seed: int = 0) -> dict:
    key = jax.random.key(seed)
    ks = jax.random.split(key, 24)

    def nrm(k, shape, scale):
        return jax.random.normal(k, shape, jnp.float32) * scale

    swa_len = [min(w, PAST_LEN) for w, _ in SWA_PATTERNS]
    return {
        "x_prompt": nrm(ks[0], (BATCH, SEQ, D_MODEL), 1.0),
        "x_sample": nrm(ks[1], (DEC_BATCH, DEC_SEQ, D_MODEL), 1.0),
        "mem_prompt": nrm(ks[2], (BATCH, N_MEM, D_MODEL), 1.0),
        "state_gla": nrm(ks[3], (DEPTH, DEC_BATCH, GLA_HEADS, GLA_DK, GLA_DV), 0.5),
        "cache_swa_w128": nrm(ks[4], (DEPTH, DEC_BATCH, swa_len[0], 2, SWA_HEADS, SWA_HD), 1.0),
        "cache_swa_w512": nrm(ks[5], (DEPTH, DEC_BATCH, swa_len[1], 2, SWA_HEADS, SWA_HD), 1.0),
        "cache_swa_w2048": nrm(ks[6], (DEPTH, DEC_BATCH, swa_len[2], 2, SWA_HEADS, SWA_HD), 1.0),
        "cache_mem_kv": nrm(ks[7], (DEPTH, DEC_BATCH, N_MEM, 2, MEM_HEADS, MEM_HD), 1.0),
        "g_norm": 1.0 + nrm(ks[8], (DEPTH, D_MODEL), 0.02),
        "w_in": nrm(ks[9], (DEPTH, D_MODEL, IN_TOTAL), D_MODEL ** -0.5),
        "w_alpha2": nrm(ks[10], (DEPTH, GLA_RANK, GLA_KW), GLA_RANK ** -0.5),
        "b_alpha": nrm(ks[11], (DEPTH, GLA_KW), 0.1),
        "g_gla_out": 1.0 + nrm(ks[12], (DEPTH, GLA_VW), 0.02),
        "g_mem": 1.0 + nrm(ks[13], (DEPTH, D_MODEL), 0.02),
        "w_mem_kv": nrm(ks[14], (DEPTH, D_MODEL, 2 * MEM_W), D_MODEL ** -0.5),
        "w_proj_a": nrm(ks[15], (DEPTH, GLA_VW, D_MODEL), GLA_VW ** -0.5),
        "w_proj_b": nrm(ks[16], (DEPTH, SWA_W, D_MODEL), SWA_W ** -0.5),
        "w_proj_c": nrm(ks[17], (DEPTH, MEM_W, D_MODEL), MEM_W ** -0.5),
        "w_out": nrm(ks[18], (DEPTH, D_MODEL, D_MODEL), D_MODEL ** -0.5),
        "g_final": 1.0 + nrm(ks[19], (D_MODEL,), 0.02),
    }


def reference(x_prompt, x_sample, mem_prompt, state_gla, cache_swa_w128, cache_swa_w512, cache_swa_w2048,
              cache_mem_kv, g_norm, w_in, w_alpha2, b_alpha, g_gla_out, g_mem, w_mem_kv, w_proj_a, w_proj_b,
              w_proj_c, w_out, g_final):
    hp, hs = x_prompt, x_sample
    gla_p, gla_s, mem_p = [], [], []
    swa_p = [[], [], []]
    swa_s = [[], [], []]
    for l in range(DEPTH):
        lw = (g_norm[l], w_in[l], w_alpha2[l], b_alpha[l], g_gla_out[l], w_proj_a[l], w_proj_b[l], w_proj_c[l], w_out[l])
        mkv_p = _mem_kv(mem_prompt, g_mem[l], w_mem_kv[l])
        s0 = jnp.zeros((hp.shape[0], GLA_HEADS, GLA_DK, GLA_DV), jnp.float32)
        hp, sp, bufs_p = _layer(hp, mkv_p, s0, None, 0, *lw)
        hs, ss, bufs_s = _layer(hs, cache_mem_kv[l], state_gla[l],
                                (cache_swa_w128[l], cache_swa_w512[l], cache_swa_w2048[l]), PAST_LEN, *lw)
        gla_p.append(sp)
        gla_s.append(ss)
        mem_p.append(mkv_p)
        for gi in range(SWA_GROUPS):
            swa_p[gi].append(bufs_p[gi])
            swa_s[gi].append(bufs_s[gi])
    y_prompt = _rmsnorm(hp, g_final)
    y_sample = _rmsnorm(hs, g_final)
    gla_prompt = jnp.stack(gla_p)
    swa_w128_prompt = jnp.stack(swa_p[0])
    swa_w512_prompt = jnp.stack(swa_p[1])
    swa_w2048_prompt = jnp.stack(swa_p[2])
    mem_kv_prompt = jnp.stack(mem_p)
    gla_sample = jnp.stack(gla_s)
    swa_w128_sample = jnp.stack(swa_s[0])
    swa_w512_sample = jnp.stack(swa_s[1])
    swa_w2048_sample = jnp.stack(swa_s[2])
    return (y_prompt, y_sample, gla_prompt, swa_w128_prompt, swa_w512_prompt, swa_w2048_prompt, mem_kv_prompt,
            gla_sample, swa_w128_sample, swa_w512_sample, swa_w2048_sample)
```

```python
import functools

import numpy as np
import jax
import jax.numpy as jnp
from jax import lax
from jax.experimental import pallas as pl
from jax.experimental.pallas import tpu as pltpu

F32 = jnp.float32
BF16 = jnp.bfloat16

EPS = 1e-6
ROPE_THETA = 10000.0
NEG = -1e30
PAST_LEN = 16384

HD = 128
N_HEADS = 4
GLA_DV = 256
GLA_RANK = 16
GLA_TAU = 16.0
GLA_CHUNK = 64
GLA_SUB = 16
SWA_PATTERNS = ((128, 1), (512, 4), (2048, 16))
SWA_BLK = 128
SWA_STEP = 2048
N_MEM = 256

LANES = 128
VMEM_LIMIT = 56 * 1024 * 1024

C_GT = 0
C_GQ, C_GK, C_GV, C_GR = 6144, 6656, 7168, 8192
C_SQ, C_SK, C_SV, C_SR = 9216, 10752, 12288, 13824
C_MQ, C_MR = 14336, 14848
N_MAIN = 15360


def _cparams(sem):
    return pltpu.CompilerParams(dimension_semantics=sem, vmem_limit_bytes=VMEM_LIMIT)


def _dot(a, b):
    return jnp.dot(a, b, preferred_element_type=F32)


def _dot_nt(a, b):
    return lax.dot_general(a, b, (((1,), (1,)), ((), ())), preferred_element_type=F32)


def _dot_tn(a, b):
    return lax.dot_general(a, b, (((0,), (0,)), ((), ())), preferred_element_type=F32)


def _silu(x):
    return x * jax.nn.sigmoid(x)


def _log_sigmoid(x):
    return jnp.minimum(x, 0.0) - jnp.log1p(jnp.exp(-jnp.abs(x)))


def _norm_matmul_kernel(*refs, has_extra, row_chunk):
    if has_extra:
        x_ref, g_ref, w_ref, wx_ref, o_ref, ox_ref, h_ref = refs
    else:
        x_ref, g_ref, w_ref, o_ref, h_ref = refs
    tm = x_ref.shape[0]

    @pl.when(pl.program_id(1) == 0)
    def _():
        def body(c, carry):
            r0 = pl.multiple_of(c * row_chunk, row_chunk)
            x = x_ref[pl.ds(r0, row_chunk), :]
            y = x * lax.rsqrt(jnp.mean(x * x, axis=-1, keepdims=True) + EPS)
            h_ref[pl.ds(r0, row_chunk), :] = (y * g_ref[...]).astype(BF16)
            return carry
        lax.fori_loop(0, tm // row_chunk, body, 0)
        if has_extra:
            ox_ref[...] = _dot(h_ref[...], wx_ref[...])

    o_ref[...] = _dot(h_ref[...], w_ref[...])


def _norm_matmul(x, g, w, wx=None, *, tm, tn):
    m, d = x.shape
    n = w.shape[1]
    assert m % tm == 0 and n % tn == 0
    has_extra = wx is not None
    row_chunk = min(tm, 128)
    in_specs = [pl.BlockSpec((tm, d), lambda i, j: (i, 0)),
                pl.BlockSpec((1, d), lambda i, j: (0, 0)),
                pl.BlockSpec((d, tn), lambda i, j: (0, j))]
    out_specs = [pl.BlockSpec((tm, tn), lambda i, j: (i, j))]
    out_shape = [jax.ShapeDtypeStruct((m, n), F32)]
    args = [x, g.reshape(1, d), w]
    if has_extra:
        nx = wx.shape[1]
        in_specs.append(pl.BlockSpec((d, nx), lambda i, j: (0, 0)))
        out_specs.append(pl.BlockSpec((tm, nx), lambda i, j: (i, 0)))
        out_shape.append(jax.ShapeDtypeStruct((m, nx), F32))
        args.append(wx)
    res = pl.pallas_call(
        functools.partial(_norm_matmul_kernel, has_extra=has_extra, row_chunk=row_chunk),
        grid=(m // tm, n // tn),
        in_specs=in_specs, out_specs=out_specs, out_shape=out_shape,
        scratch_shapes=[pltpu.VMEM((tm, d), BF16)],
        compiler_params=_cparams(("arbitrary", "arbitrary")),
        name="norm_matmul",
    )(*args)
    return res if has_extra else res[0]


def _diag_select_matrix():
    rows = np.arange(GLA_SUB * HD)[:, None] // HD
    cols = np.arange(LANES)[None, :] % GLA_SUB
    return jnp.asarray((rows == cols).astype(np.float32), dtype=BF16)


def _gla_kernel(q_ref, k_ref, v_ref, gr_ref, ga_ref, wa_ref, ba_ref, gg_ref, em_ref,
                u_ref, sout_ref, st_ref):
    t_blk = pl.program_id(1)
    n_chunks = q_ref.shape[0] // GLA_CHUNK
    C, SUB, n_sub = GLA_CHUNK, GLA_SUB, GLA_CHUNK // GLA_SUB

    @pl.when(t_blk == 0)
    def _():
        st_ref[...] = jnp.zeros_like(st_ref)

    row = lax.broadcasted_iota(jnp.int32, (C, C), 0)
    col = lax.broadcasted_iota(jnp.int32, (C, C), 1)
    tri = (col <= row).astype(F32)
    row_c = lax.broadcasted_iota(jnp.int32, (C, HD), 0)
    sub_row = lax.broadcasted_iota(jnp.int32, (SUB, HD), 0)
    lane_c = lax.broadcasted_iota(jnp.int32, (SUB, C), 1)

    def chunk(c, carry):
        r0 = pl.multiple_of(c * C, C)
        rows = pl.ds(r0, C)
        xa = _dot(ga_ref[rows, :].astype(BF16), wa_ref[...]) + ba_ref[...]
        la = _log_sigmoid(xa) / GLA_TAU
        b_all = jnp.dot(tri, la, preferred_element_type=F32, precision=lax.Precision.HIGHEST)

        p_rows = []
        a_off = []
        per_head = []
        for h in range(N_HEADS):
            q = q_ref[rows, h * HD:(h + 1) * HD] * (HD ** -0.5)
            k = k_ref[rows, h * HD:(h + 1) * HD]
            b = b_all[:, h * HD:(h + 1) * HD]
            per_head.append((q, k, b))
            for i in range(n_sub):
                sl = slice(i * SUB, (i + 1) * SUB)
                q_i, k_i, b_i = q[sl], k[sl], b[sl]
                if i == 0:
                    a_off.append(jnp.zeros((SUB, C), F32))
                else:
                    b_ref_row = b[i * SUB - 1:i * SUB, :]
                    qs = q_i * jnp.exp(b_i - b_ref_row)
                    ks = k * jnp.exp(jnp.where(row_c < i * SUB, b_ref_row - b, -jnp.inf))
                    a_off.append(_dot_nt(qs.astype(BF16), ks.astype(BF16)))
                slabs = []
                for s in range(SUB):
                    e = jnp.where(sub_row >= s, b_i - b_i[s:s + 1, :], -jnp.inf)
                    slabs.append((q_i * k_i[s:s + 1, :]) * jnp.exp(e))
                p_rows.append(jnp.concatenate(slabs, axis=1).astype(BF16))
        r_all = _dot(jnp.concatenate(p_rows, axis=0), em_ref[...])

        for h in range(N_HEADS):
            q, k, b = per_head[h]
            v = v_ref[rows, h * GLA_DV:(h + 1) * GLA_DV]
            a_rows = []
            for i in range(n_sub):
                idx = h * n_sub + i
                r_i = r_all[idx * SUB:(idx + 1) * SUB, :C]
                in_blk = (lane_c >= i * SUB) & (lane_c < (i + 1) * SUB)
                a_rows.append(a_off[idx] + jnp.where(in_blk, r_i, 0.0))
            a = jnp.concatenate(a_rows, axis=0)
            st = st_ref[h]
            o = _dot(a.astype(BF16), v.astype(BF16)) + _dot_nt((q * jnp.exp(b)).astype(BF16), st.astype(BF16))
            b_end = b[C - 1:C, :]
            kd = k * jnp.exp(b_end - b)
            st_ref[h] = st * jnp.exp(b_end) + _dot_tn(v.astype(BF16), kd.astype(BF16))
            y = o * lax.rsqrt(jnp.mean(o * o, axis=-1, keepdims=True) + EPS)
            y = y * gg_ref[:, h * GLA_DV:(h + 1) * GLA_DV]
            gate = gr_ref[rows, h * GLA_DV:(h + 1) * GLA_DV]
            u_ref[rows, h * GLA_DV:(h + 1) * GLA_DV] = (y * _silu(gate)).astype(u_ref.dtype)
        return carry

    lax.fori_loop(0, n_chunks, chunk, 0)

    @pl.when(t_blk == pl.num_programs(1) - 1)
    def _():
        for h in range(N_HEADS):
            sout_ref[h] = st_ref[h].T


def _gla_prompt(z, za, wa_pad, b_alpha, g_gla_out, batch, seq, *, t_blk):
    nt = seq // t_blk
    assert seq % t_blk == 0 and t_blk % GLA_CHUNK == 0
    kw = N_HEADS * HD
    vw = N_HEADS * GLA_DV

    def zspec(width, col0):
        cb = col0 // width
        return pl.BlockSpec((t_blk, width), lambda b, t: (b * nt + t, cb))

    return pl.pallas_call(
        _gla_kernel,
        grid=(batch, nt),
        in_specs=[zspec(kw, C_GQ), zspec(kw, C_GK), zspec(vw, C_GV), zspec(vw, C_GR),
                  pl.BlockSpec((t_blk, LANES), lambda b, t: (b * nt + t, 0)),
                  pl.BlockSpec((LANES, kw), lambda b, t: (0, 0)),
                  pl.BlockSpec((1, kw), lambda b, t: (0, 0)),
                  pl.BlockSpec((1, vw), lambda b, t: (0, 0)),
                  pl.BlockSpec((GLA_SUB * HD, LANES), lambda b, t: (0, 0))],
        out_specs=[pl.BlockSpec((t_blk, vw), lambda b, t: (b * nt + t, 0)),
                   pl.BlockSpec((None, N_HEADS, HD, GLA_DV), lambda b, t: (b, 0, 0, 0))],
        out_shape=[jax.ShapeDtypeStruct((batch * seq, vw), BF16),
                   jax.ShapeDtypeStruct((batch, N_HEADS, HD, GLA_DV), F32)],
        scratch_shapes=[pltpu.VMEM((N_HEADS, GLA_DV, HD), F32)],
        compiler_params=_cparams(("arbitrary", "arbitrary")),
        name="gla",
    )(z, z, z, z, za, wa_pad, b_alpha.reshape(1, kw), g_gla_out.reshape(1, vw), _diag_select_matrix())


def _rope_tables(pos):
    half = HD // 2
    inv = ROPE_THETA ** (-jnp.arange(half, dtype=F32) / half)
    ang = pos.astype(F32)[:, None] * inv[None, :]
    cos, sin = jnp.cos(ang), jnp.sin(ang)
    return jnp.concatenate([cos, cos], axis=1), jnp.concatenate([-sin, sin], axis=1)


def _rope(x, cos2, sin2):
    return x * cos2 + pltpu.roll(x, HD // 2, axis=1) * sin2


def _swa_kernel(*refs):
    (q0, q1, q2, k0, k1, k2, v0, v1, v2, sr_ref, cos_ref, sin_ref,
     u_ref, kb0, kb1, kb2, vb0, vb1, vb2,
     q_s, k_s, v_s, o_s, l_s) = refs
    q_in, k_in, v_in = (q0, q1, q2), (k0, k1, k2), (v0, v1, v2)
    kb, vb = (kb0, kb1, kb2), (vb0, vb1, vb2)
    i = pl.program_id(2)
    T = SWA_STEP
    n_grp = len(SWA_PATTERNS)

    @pl.when(i == 0)
    def _():
        k_s[:, 0:T, :] = jnp.zeros((n_grp, T, HD), F32)
        v_s[:, 0:T, :] = jnp.zeros((n_grp, T, HD), F32)

    @pl.when(i > 0)
    def _():
        k_s[:, 0:T, :] = k_s[:, T:2 * T, :]
        v_s[:, 0:T, :] = v_s[:, T:2 * T, :]

    cos2, sin2 = cos_ref[...], sin_ref[...]
    for g in range(n_grp):
        q_s[g] = _rope(q_in[g][...], cos2, sin2) * (HD ** -0.5)
        k_s[g, T:2 * T, :] = _rope(k_in[g][...], cos2, sin2)
        v_s[g, T:2 * T, :] = v_in[g][...]

    qi = lax.broadcasted_iota(jnp.int32, (SWA_BLK, 2 * SWA_BLK), 0)
    ki = lax.broadcasted_iota(jnp.int32, (SWA_BLK, 2 * SWA_BLK), 1)
    delta = SWA_BLK + qi - ki

    for g, (win, dil) in enumerate(SWA_PATTERNS):
        n_keys = win // dil
        unit = dil * SWA_BLK
        band = (delta >= 0) & (delta <= n_keys)
        shift = dil.bit_length() - 1

        def rows(start, size, dil=dil):
            return pl.ds(start, size) if dil == 1 else pl.ds(start, size, stride=dil)

        def block(n, carry, g=g, dil=dil, unit=unit, band=band, shift=shift, rows=rows):
            u = lax.shift_right_logical(n, shift)
            r = lax.bitwise_and(n, dil - 1)
            q = q_s[g, rows(u * unit + r, SWA_BLK), :]
            kk = k_s[g, rows(T + (u - 1) * unit + r, 2 * SWA_BLK), :]
            vv = v_s[g, rows(T + (u - 1) * unit + r, 2 * SWA_BLK), :]
            s = _dot_nt(q.astype(BF16), kk.astype(BF16))
            ki_min = jnp.where((i * (T // unit) + u) == 0, SWA_BLK, 0)
            s = jnp.where(band & (ki >= ki_min), s, NEG)
            m = jnp.max(s, axis=-1, keepdims=True)
            p = jnp.exp(s - m)
            den = jnp.sum(p, axis=-1, keepdims=True)
            o = _dot(p.astype(BF16), vv.astype(BF16)) / den
            lse = m + jnp.log(den)
            o_s[g, rows(u * unit + r, SWA_BLK), :] = o
            l_s[g, rows(u * unit + r, SWA_BLK), :] = jnp.broadcast_to(lse, (SWA_BLK, HD))
            return carry

        lax.fori_loop(0, T // SWA_BLK, block, 0)

    rc = 256

    def merge(c, carry):
        r0 = pl.multiple_of(c * rc, rc)
        rr = pl.ds(r0, rc)
        ls = [l_s[g, rr, :] for g in range(n_grp)]
        mx = functools.reduce(jnp.maximum, ls)
        es = [jnp.exp(l - mx) for l in ls]
        tot = functools.reduce(lambda a, b: a + b, es)
        ob = functools.reduce(lambda a, b: a + b, [(es[g] / tot) * o_s[g, rr, :] for g in range(n_grp)])
        u_ref[rr, :] = (ob * _silu(sr_ref[rr, :])).astype(u_ref.dtype)
        return carry

    lax.fori_loop(0, T // rc, merge, 0)

    @pl.when(i == pl.num_programs(2) - 1)
    def _():
        for g, (win, _) in enumerate(SWA_PATTERNS):
            kb[g][...] = k_s[g, 2 * T - win:2 * T, :]
            vb[g][...] = v_s[g, 2 * T - win:2 * T, :]


def _swa_prompt(z, cos2, sin2, batch, seq):
    T = SWA_STEP
    assert seq % T == 0 and all(w <= T for w, _ in SWA_PATTERNS)
    nt = seq // T
    w_grp = N_HEADS * HD

    def zspec(col0, g):
        cb0 = (col0 + g * w_grp) // HD
        return pl.BlockSpec((T, HD), lambda b, j, i: (b * nt + i, cb0 + j))

    in_specs = ([zspec(C_SQ, g) for g in range(3)] + [zspec(C_SK, g) for g in range(3)]
                + [zspec(C_SV, g) for g in range(3)] + [zspec(C_SR, 0)]
                + [pl.BlockSpec((T, HD), lambda b, j, i: (i, 0))] * 2)
    buf_specs = [pl.BlockSpec((None, w, HD), lambda b, j, i: (b, 0, j)) for w, _ in SWA_PATTERNS]
    buf_shapes = [jax.ShapeDtypeStruct((batch, w, w_grp), F32) for w, _ in SWA_PATTERNS]
    return pl.pallas_call(
        _swa_kernel,
        grid=(batch, N_HEADS, nt),
        in_specs=in_specs,
        out_specs=[pl.BlockSpec((T, HD), lambda b, j, i: (b * nt + i, j))] + buf_specs + buf_specs,
        out_shape=[jax.ShapeDtypeStruct((batch * seq, w_grp), BF16)] + buf_shapes + buf_shapes,
        scratch_shapes=[pltpu.VMEM((3, T, HD), F32), pltpu.VMEM((3, 2 * T, HD), F32),
                        pltpu.VMEM((3, 2 * T, HD), F32), pltpu.VMEM((3, T, HD), F32),
                        pltpu.VMEM((3, T, HD), F32)],
        compiler_params=_cparams(("arbitrary", "arbitrary", "arbitrary")),
        name="swa",
    )(*([z] * 10), cos2, sin2)


def _memattn_kernel(q_ref, mr_ref, kv_ref, u_ref):
    w = N_HEADS * HD
    for h in range(N_HEADS):
        cs = slice(h * HD, (h + 1) * HD)
        q = q_ref[:, cs] * (HD ** -0.5)
        k = kv_ref[:, cs]
        v = kv_ref[:, w + h * HD:w + (h + 1) * HD]
        s = _dot_nt(q.astype(BF16), k.astype(BF16))
        e = jnp.exp(s - jnp.max(s, axis=-1, keepdims=True))
        p = e / jnp.sum(e, axis=-1, keepdims=True)
        o = _dot(p.astype(BF16), v.astype(BF16))
        u_ref[:, cs] = (o * _silu(mr_ref[:, cs])).astype(u_ref.dtype)


def _memattn_prompt(z, mem_kv, batch, seq, *, t_blk):
    nt = seq // t_blk
    w = N_HEADS * HD
    return pl.pallas_call(
        _memattn_kernel,
        grid=(batch, nt),
        in_specs=[pl.BlockSpec((t_blk, w), lambda b, t: (b * nt + t, C_MQ // w)),
                  pl.BlockSpec((t_blk, w), lambda b, t: (b * nt + t, C_MR // w)),
                  pl.BlockSpec((N_MEM, 2 * w), lambda b, t: (b, 0))],
        out_specs=pl.BlockSpec((t_blk, w), lambda b, t: (b * nt + t, 0)),
        out_shape=jax.ShapeDtypeStruct((batch * seq, w), BF16),
        compiler_params=_cparams(("arbitrary", "arbitrary")),
        name="memattn",
    )(z, z, mem_kv)


def _final_kernel(ua_ref, ub_ref, uc_ref, gt_ref, x_ref, wa_ref, wb_ref, wc_ref, wo_ref, gf_ref, y_ref):
    d = x_ref.shape[1]
    ya = _dot(ua_ref[...].astype(BF16), wa_ref[...])
    yb = _dot(ub_ref[...].astype(BF16), wb_ref[...])
    yc = _dot(uc_ref[...].astype(BF16), wc_ref[...])
    mix = (jax.nn.sigmoid(gt_ref[:, 0:d]) * ya + jax.nn.sigmoid(gt_ref[:, d:2 * d]) * yb
           + jax.nn.sigmoid(gt_ref[:, 2 * d:3 * d]) * yc)
    xo = x_ref[...] + _dot(mix.astype(BF16), wo_ref[...])
    y = xo * lax.rsqrt(jnp.mean(xo * xo, axis=-1, keepdims=True) + EPS)
    y_ref[...] = y * gf_ref[...]


def _final(ua, ub, uc, z, x, wa, wb, wc, wo, g_final, *, tm):
    m, d = x.shape
    const = lambda i: (0, 0)
    resident = dict(pipeline_mode=pl.Buffered(1))
    return pl.pallas_call(
        _final_kernel,
        grid=(m // tm,),
        in_specs=[pl.BlockSpec((tm, ua.shape[1]), lambda i: (i, 0)),
                  pl.BlockSpec((tm, ub.shape[1]), lambda i: (i, 0)),
                  pl.BlockSpec((tm, uc.shape[1]), lambda i: (i, 0)),
                  pl.BlockSpec((tm, 3 * d), lambda i: (i, C_GT // (3 * d))),
                  pl.BlockSpec((tm, d), lambda i: (i, 0)),
                  pl.BlockSpec(wa.shape, const, **resident),
                  pl.BlockSpec(wb.shape, const, **resident),
                  pl.BlockSpec(wc.shape, const, **resident),
                  pl.BlockSpec(wo.shape, const, **resident),
                  pl.BlockSpec((1, d), const)],
        out_specs=pl.BlockSpec((tm, d), lambda i: (i, 0)),
        out_shape=jax.ShapeDtypeStruct((m, d), F32),
        compiler_params=_cparams(("arbitrary",)),
        name="final",
    )(ua, ub, uc, z, x, wa, wb, wc, wo, g_final.reshape(1, d))


def _heads_rows(row, col0):
    return jnp.concatenate([row[:, col0 + h * HD:col0 + (h + 1) * HD] for h in range(N_HEADS)], axis=0)


def _decode_attention(q4, kk, vv, k_new=None, v_new=None):
    s = jnp.sum(kk * q4[None], axis=-1, keepdims=True)
    m = jnp.max(s, axis=0)
    if k_new is not None:
        s_new = jnp.sum(k_new * q4, axis=-1, keepdims=True)
        m = jnp.maximum(m, s_new)
    p = jnp.exp(s - m[None])
    den = jnp.sum(p, axis=0)
    acc = jnp.sum(p * vv, axis=0)
    if k_new is not None:
        p_new = jnp.exp(s_new - m)
        den = den + p_new
        acc = acc + p_new * v_new
    return acc / den, m + jnp.log(den)


def _sample_kernel(*refs):
    (z_ref, za_ref, wa_ref, ba_ref, gg_ref, cos_ref, sin_ref, st_ref,
     cg0, cg1, cg2, cm_ref, ca0, ca1, ca2,
     ua_ref, ub_ref, uc_ref, sout_ref, co0, co1, co2,
     new_s, sems) = refs
    cg, ca, co = (cg0, cg1, cg2), (ca0, ca1, ca2), (co0, co1, co2)
    b = pl.program_id(0)
    zrow = z_ref[pl.ds(b, 1), :]

    def bulk_copy(g):
        win = SWA_PATTERNS[g][0]
        return pltpu.make_async_copy(ca[g].at[0, b, pl.ds(1, win - 1)], co[g].at[0, b, pl.ds(0, win - 1)],
                                     sems.at[0, g])

    def row_copy(g):
        win = SWA_PATTERNS[g][0]
        return pltpu.make_async_copy(new_s.at[g], co[g].at[0, b, win - 1], sems.at[1, g])

    for g in range(3):
        bulk_copy(g).start()

    ga8 = jnp.broadcast_to(za_ref[pl.ds(b, 1), :], (8, LANES))
    xa = _dot(ga8.astype(BF16), wa_ref[...])[0:1, :] + ba_ref[...]
    a_row = jnp.exp(_log_sigmoid(xa) / GLA_TAU)
    eye = (lax.broadcasted_iota(jnp.int32, (HD, HD), 0) == lax.broadcasted_iota(jnp.int32, (HD, HD), 1))

    def col(row_vec):
        return jnp.sum(jnp.where(eye, jnp.broadcast_to(row_vec, (HD, HD)), 0.0), axis=1, keepdims=True)

    for h in range(N_HEADS):
        q = zrow[:, C_GQ + h * HD:C_GQ + (h + 1) * HD] * (HD ** -0.5)
        k = zrow[:, C_GK + h * HD:C_GK + (h + 1) * HD]
        v = zrow[:, C_GV + h * GLA_DV:C_GV + (h + 1) * GLA_DV]
        s_new = col(a_row[:, h * HD:(h + 1) * HD]) * st_ref[h] + col(k) * v
        sout_ref[h] = s_new
        o = jnp.sum(col(q) * s_new, axis=0, keepdims=True)
        y = o * lax.rsqrt(jnp.mean(o * o, axis=-1, keepdims=True) + EPS)
        y = y * gg_ref[:, h * GLA_DV:(h + 1) * GLA_DV]
        gate = zrow[:, C_GR + h * GLA_DV:C_GR + (h + 1) * GLA_DV]
        ua_ref[:, h * GLA_DV:(h + 1) * GLA_DV] = y * _silu(gate)

    cos2, sin2 = cos_ref[...], sin_ref[...]
    outs, lses = [], []
    w_grp = N_HEADS * HD
    for g in range(3):
        q4 = _rope(_heads_rows(zrow, C_SQ + g * w_grp), cos2, sin2) * (HD ** -0.5)
        k4 = _rope(_heads_rows(zrow, C_SK + g * w_grp), cos2, sin2)
        v4 = _heads_rows(zrow, C_SV + g * w_grp)
        new_s[g, 0] = k4
        new_s[g, 1] = v4
        row_copy(g).start()
        o, lse = _decode_attention(q4, cg[g][:, 0], cg[g][:, 1], k4, v4)
        outs.append(o)
        lses.append(lse)
    mx = functools.reduce(jnp.maximum, lses)
    es = [jnp.exp(l - mx) for l in lses]
    tot = es[0] + es[1] + es[2]
    ob = (es[0] / tot) * outs[0] + (es[1] / tot) * outs[1] + (es[2] / tot) * outs[2]
    for h in range(N_HEADS):
        gate = zrow[:, C_SR + h * HD:C_SR + (h + 1) * HD]
        ub_ref[:, h * HD:(h + 1) * HD] = ob[h:h + 1, :] * _silu(gate)

    qm = _heads_rows(zrow, C_MQ) * (HD ** -0.5)
    oc, _ = _decode_attention(qm, cm_ref[:, 0], cm_ref[:, 1])
    for h in range(N_HEADS):
        gate = zrow[:, C_MR + h * HD:C_MR + (h + 1) * HD]
        uc_ref[:, h * HD:(h + 1) * HD] = oc[h:h + 1, :] * _silu(gate)

    for g in range(3):
        bulk_copy(g).wait()
        row_copy(g).wait()


def _sample_mixers(z, za, wa_pad, b_alpha, g_gla_out, cos2, sin2, state, caches, cache_mem):
    db = z.shape[0]
    kw, vw, w = N_HEADS * HD, N_HEADS * GLA_DV, N_HEADS * HD
    const2 = lambda b: (0, 0)
    gathered, gather_specs = [], []
    for c, (win, dil) in zip(caches, SWA_PATTERNS):
        assert c.shape == (1, db, win, 2, N_HEADS, HD)
        n_keys = win // dil
        gathered.append(c.reshape(db, n_keys, dil, 2, N_HEADS, HD))
        gather_specs.append(pl.BlockSpec((None, n_keys, None, 2, N_HEADS, HD), lambda b: (b, 0, 0, 0, 0, 0)))
    any_spec = pl.BlockSpec(memory_space=pl.ANY)
    row3 = lambda n: pl.BlockSpec((None, 1, n), lambda b: (b, 0, 0))
    res = pl.pallas_call(
        _sample_kernel,
        grid=(db,),
        in_specs=[pl.BlockSpec(z.shape, const2), pl.BlockSpec(za.shape, const2),
                  pl.BlockSpec(wa_pad.shape, const2), pl.BlockSpec((1, kw), const2),
                  pl.BlockSpec((1, vw), const2), pl.BlockSpec((1, HD), const2), pl.BlockSpec((1, HD), const2),
                  pl.BlockSpec((None, None, N_HEADS, HD, GLA_DV), lambda b: (0, b, 0, 0, 0))]
                 + gather_specs
                 + [pl.BlockSpec((None, None, N_MEM, 2, N_HEADS, HD), lambda b: (0, b, 0, 0, 0, 0))]
                 + [any_spec] * 3,
        out_specs=[row3(vw), row3(w), row3(w),
                   pl.BlockSpec((None, None, N_HEADS, HD, GLA_DV), lambda b: (0, b, 0, 0, 0))] + [any_spec] * 3,
        out_shape=[jax.ShapeDtypeStruct((db, 1, vw), F32), jax.ShapeDtypeStruct((db, 1, w), F32),
                   jax.ShapeDtypeStruct((db, 1, w), F32), jax.ShapeDtypeStruct(state.shape, F32)]
                  + [jax.ShapeDtypeStruct(c.shape, F32) for c in caches],
        scratch_shapes=[pltpu.VMEM((3, 2, N_HEADS, HD), F32), pltpu.SemaphoreType.DMA((2, 3))],
        compiler_params=_cparams(("arbitrary",)),
        name="sample_mixers",
    )(z, za, wa_pad, b_alpha.reshape(1, kw), g_gla_out.reshape(1, vw), cos2, sin2, state,
      *gathered, cache_mem, *caches)
    ua, ub, uc, s_out = res[:4]
    return ua.reshape(db, vw), ub.reshape(db, w), uc.reshape(db, w), s_out, res[4:]


def kernel(x_prompt, x_sample, mem_prompt, state_gla, cache_swa_w128, cache_swa_w512, cache_swa_w2048, cache_mem_kv, g_norm, w_in, w_alpha2, b_alpha, g_gla_out, g_mem, w_mem_kv, w_proj_a, w_proj_b, w_proj_c, w_out, g_final):
    batch, seq, d = x_prompt.shape
    db, dec_seq, _ = x_sample.shape
    assert g_norm.shape[0] == 1 and dec_seq == 1
    w_grp = N_HEADS * HD

    c_ga = 2 * N_HEADS * HD + 2 * N_HEADS * GLA_DV
    w_full = w_in[0]
    c_gt = w_full.shape[1] - 3 * d
    w_main = jnp.concatenate([w_full[:, c_gt:], w_full[:, :c_ga], w_full[:, c_ga + GLA_RANK:c_gt]],
                             axis=1).astype(BF16)
    w_ga = jnp.pad(w_full[:, c_ga:c_ga + GLA_RANK], ((0, 0), (0, LANES - GLA_RANK))).astype(BF16)
    wa_pad = jnp.pad(w_alpha2[0], ((0, LANES - GLA_RANK), (0, 0))).astype(BF16)
    wpa, wpb, wpc, wo = (w[0].astype(BF16) for w in (w_proj_a, w_proj_b, w_proj_c, w_out))
    assert w_main.shape[1] == N_MAIN

    xp = x_prompt.reshape(batch * seq, d)
    xs = x_sample.reshape(db, d)

    z, za = _norm_matmul(xp, g_norm[0], w_main, w_ga, tm=1024, tn=1024)
    mem_kv = _norm_matmul(mem_prompt.reshape(batch * N_MEM, d), g_mem[0], w_mem_kv[0].astype(BF16),
                          tm=batch * N_MEM, tn=512)
    ua, gla_p = _gla_prompt(z, za, wa_pad, b_alpha[0], g_gla_out[0], batch, seq, t_blk=512)
    cos_p, sin_p = _rope_tables(jnp.arange(seq, dtype=jnp.int32))
    swa_res = _swa_prompt(z, cos_p, sin_p, batch, seq)
    ub, kbufs, vbufs = swa_res[0], swa_res[1:4], swa_res[4:7]
    uc = _memattn_prompt(z, mem_kv, batch, seq, t_blk=512)
    y_prompt = _final(ua, ub, uc, z, xp, wpa, wpb, wpc, wo, g_final, tm=256).reshape(batch, seq, d)

    zs, zas = _norm_matmul(xs, g_norm[0], w_main, w_ga, tm=db, tn=1024)
    cos_s, sin_s = _rope_tables(jnp.full((1,), PAST_LEN, jnp.int32))
    uas, ubs, ucs, gla_s, swa_s = _sample_mixers(
        zs, zas, wa_pad, b_alpha[0], g_gla_out[0], cos_s, sin_s, state_gla,
        (cache_swa_w128, cache_swa_w512, cache_swa_w2048), cache_mem_kv)
    y_sample = _final(uas, ubs, ucs, zs, xs, wpa, wpb, wpc, wo, g_final, tm=db).reshape(db, 1, d)

    swa_p = [jnp.stack([k.reshape(batch, w, N_HEADS, HD), v.reshape(batch, w, N_HEADS, HD)], axis=2)[None]
             for k, v, (w, _) in zip(kbufs, vbufs, SWA_PATTERNS)]
    mem_kv_prompt = mem_kv.reshape(1, batch, N_MEM, 2, N_HEADS, HD)
    return (y_prompt, y_sample, gla_p[None], swa_p[0], swa_p[1], swa_p[2], mem_kv_prompt,
            gla_s, swa_s[0], swa_s[1], swa_s[2])
```

```python
import functools

import numpy as np
import jax
import jax.numpy as jnp
from jax import lax
from jax.experimental import pallas as pl
from jax.experimental.pallas import tpu as pltpu

F32 = jnp.float32
BF16 = jnp.bfloat16

EPS = 1e-6
ROPE_THETA = 10000.0
NEG = -1e30
PAST_LEN = 16384

HD = 128
N_HEADS = 4
GLA_DV = 256
GLA_RANK = 16
GLA_TAU = 16.0
GLA_CHUNK = 64
GLA_SUB = 16
SWA_PATTERNS = ((128, 1), (512, 4), (2048, 16))
SWA_BLK = 128
SWA_STEP = 2048
N_MEM = 256

LANES = 128
VMEM_LIMIT = 56 * 1024 * 1024

C_GT = 0
C_GQ, C_GK, C_GV, C_GR = 6144, 6656, 7168, 8192
C_SQ, C_SK, C_SV, C_SR = 9216, 10752, 12288, 13824
C_MQ, C_MR = 14336, 14848
N_MAIN = 15360


def _cparams(sem):
    return pltpu.CompilerParams(dimension_semantics=sem, vmem_limit_bytes=VMEM_LIMIT)


def _dot(a, b):
    return jnp.dot(a, b, preferred_element_type=F32)


def _dot_nt(a, b):
    return lax.dot_general(a, b, (((1,), (1,)), ((), ())), preferred_element_type=F32)


def _dot_tn(a, b):
    return lax.dot_general(a, b, (((0,), (0,)), ((), ())), preferred_element_type=F32)


def _silu(x):
    return x * jax.nn.sigmoid(x)


def _log_sigmoid(x):
    return jnp.minimum(x, 0.0) - jnp.log1p(jnp.exp(-jnp.abs(x)))


def _norm_matmul_kernel(*refs, has_extra, row_chunk):
    if has_extra:
        x_ref, g_ref, w_ref, wx_ref, o_ref, ox_ref, h_ref = refs
    else:
        x_ref, g_ref, w_ref, o_ref, h_ref = refs
    tm = x_ref.shape[0]

    @pl.when(pl.program_id(1) == 0)
    def _():
        def body(c, carry):
            r0 = pl.multiple_of(c * row_chunk, row_chunk)
            x = x_ref[pl.ds(r0, row_chunk), :]
            y = x * lax.rsqrt(jnp.mean(x * x, axis=-1, keepdims=True) + EPS)
            h_ref[pl.ds(r0, row_chunk), :] = (y * g_ref[...]).astype(BF16)
            return carry
        lax.fori_loop(0, tm // row_chunk, body, 0)
        if has_extra:
            ox_ref[...] = _dot(h_ref[...], wx_ref[...])

    o_ref[...] = _dot(h_ref[...], w_ref[...])


def _norm_matmul(x, g, w, wx=None, *, tm, tn):
    m, d = x.shape
    n = w.shape[1]
    assert m % tm == 0 and n % tn == 0
    has_extra = wx is not None
    row_chunk = min(tm, 128)
    in_specs = [pl.BlockSpec((tm, d), lambda i, j: (i, 0)),
                pl.BlockSpec((1, d), lambda i, j: (0, 0)),
                pl.BlockSpec((d, tn), lambda i, j: (0, j))]
    out_specs = [pl.BlockSpec((tm, tn), lambda i, j: (i, j))]
    out_shape = [jax.ShapeDtypeStruct((m, n), F32)]
    args = [x, g.reshape(1, d), w]
    if has_extra:
        nx = wx.shape[1]
        in_specs.append(pl.BlockSpec((d, nx), lambda i, j: (0, 0)))
        out_specs.append(pl.BlockSpec((tm, nx), lambda i, j: (i, 0)))
        out_shape.append(jax.ShapeDtypeStruct((m, nx), F32))
        args.append(wx)
    res = pl.pallas_call(
        functools.partial(_norm_matmul_kernel, has_extra=has_extra, row_chunk=row_chunk),
        grid=(m // tm, n // tn),
        in_specs=in_specs, out_specs=out_specs, out_shape=out_shape,
        scratch_shapes=[pltpu.VMEM((tm, d), BF16)],
        compiler_params=_cparams(("arbitrary", "arbitrary")),
        name="norm_matmul",
    )(*args)
    return res if has_extra else res[0]


def _diag_select_matrix():
    rows = np.arange(GLA_SUB * HD)[:, None] // HD
    cols = np.arange(LANES)[None, :] % GLA_SUB
    return jnp.asarray((rows == cols).astype(np.float32), dtype=BF16)


def _gla_kernel(q_ref, k_ref, v_ref, gr_ref, ga_ref, wa_ref, ba_ref, gg_ref, em_ref,
                u_ref, sout_ref, st_ref):
    t_blk = pl.program_id(1)
    n_chunks = q_ref.shape[0] // GLA_CHUNK
    C, SUB, n_sub = GLA_CHUNK, GLA_SUB, GLA_CHUNK // GLA_SUB

    @pl.when(t_blk == 0)
    def _():
        st_ref[...] = jnp.zeros_like(st_ref)

    row = lax.broadcasted_iota(jnp.int32, (C, C), 0)
    col = lax.broadcasted_iota(jnp.int32, (C, C), 1)
    tri = (col <= row).astype(F32)
    row_c = lax.broadcasted_iota(jnp.int32, (C, HD), 0)
    sub_row = lax.broadcasted_iota(jnp.int32, (SUB, HD), 0)
    lane_c = lax.broadcasted_iota(jnp.int32, (SUB, C), 1)

    def chunk(c, carry):
        r0 = pl.multiple_of(c * C, C)
        rows = pl.ds(r0, C)
        xa = _dot(ga_ref[rows, :].astype(BF16), wa_ref[...]) + ba_ref[...]
        la = _log_sigmoid(xa) / GLA_TAU
        b_all = jnp.dot(tri, la, preferred_element_type=F32, precision=lax.Precision.HIGHEST)

        p_rows = []
        a_off = []
        per_head = []
        for h in range(N_HEADS):
            q = q_ref[rows, h * HD:(h + 1) * HD] * (HD ** -0.5)
            k = k_ref[rows, h * HD:(h + 1) * HD]
            b = b_all[:, h * HD:(h + 1) * HD]
            per_head.append((q, k, b))
            for i in range(n_sub):
                sl = slice(i * SUB, (i + 1) * SUB)
                q_i, k_i, b_i = q[sl], k[sl], b[sl]
                if i == 0:
                    a_off.append(jnp.zeros((SUB, C), F32))
                else:
                    b_ref_row = b[i * SUB - 1:i * SUB, :]
                    qs = q_i * jnp.exp(b_i - b_ref_row)
                    ks = k * jnp.exp(jnp.where(row_c < i * SUB, b_ref_row - b, -jnp.inf))
                    a_off.append(_dot_nt(qs.astype(BF16), ks.astype(BF16)))
                slabs = []
                for s in range(SUB):
                    e = jnp.where(sub_row >= s, b_i - b_i[s:s + 1, :], -jnp.inf)
                    slabs.append((q_i * k_i[s:s + 1, :]) * jnp.exp(e))
                p_rows.append(jnp.concatenate(slabs, axis=1).astype(BF16))
        r_all = _dot(jnp.concatenate(p_rows, axis=0), em_ref[...])

        for h in range(N_HEADS):
            q, k, b = per_head[h]
            v = v_ref[rows, h * GLA_DV:(h + 1) * GLA_DV]
            a_rows = []
            for i in range(n_sub):
                idx = h * n_sub + i
                r_i = r_all[idx * SUB:(idx + 1) * SUB, :C]
                in_blk = (lane_c >= i * SUB) & (lane_c < (i + 1) * SUB)
                a_rows.append(a_off[idx] + jnp.where(in_blk, r_i, 0.0))
            a = jnp.concatenate(a_rows, axis=0)
            st = st_ref[h]
            o = _dot(a.astype(BF16), v.astype(BF16)) + _dot_nt((q * jnp.exp(b)).astype(BF16), st.astype(BF16))
            b_end = b[C - 1:C, :]
            kd = k * jnp.exp(b_end - b)
            st_ref[h] = st * jnp.exp(b_end) + _dot_tn(v.astype(BF16), kd.astype(BF16))
            y = o * lax.rsqrt(jnp.mean(o * o, axis=-1, keepdims=True) + EPS)
            y = y * gg_ref[:, h * GLA_DV:(h + 1) * GLA_DV]
            gate = gr_ref[rows, h * GLA_DV:(h + 1) * GLA_DV]
            u_ref[rows, h * GLA_DV:(h + 1) * GLA_DV] = (y * _silu(gate)).astype(u_ref.dtype)
        return carry

    lax.fori_loop(0, n_chunks, chunk, 0)

    @pl.when(t_blk == pl.num_programs(1) - 1)
    def _():
        for h in range(N_HEADS):
            sout_ref[h] = st_ref[h].T


def _gla_prompt(z, za, wa_pad, b_alpha, g_gla_out, batch, seq, *, t_blk):
    nt = seq // t_blk
    assert seq % t_blk == 0 and t_blk % GLA_CHUNK == 0
    kw = N_HEADS * HD
    vw = N_HEADS * GLA_DV

    def zspec(width, col0):
        cb = col0 // width
        return pl.BlockSpec((t_blk, width), lambda b, t: (b * nt + t, cb))

    return pl.pallas_call(
        _gla_kernel,
        grid=(batch, nt),
        in_specs=[zspec(kw, C_GQ), zspec(kw, C_GK), zspec(vw, C_GV), zspec(vw, C_GR),
                  pl.BlockSpec((t_blk, LANES), lambda b, t: (b * nt + t, 0)),
                  pl.BlockSpec((LANES, kw), lambda b, t: (0, 0)),
                  pl.BlockSpec((1, kw), lambda b, t: (0, 0)),
                  pl.BlockSpec((1, vw), lambda b, t: (0, 0)),
                  pl.BlockSpec((GLA_SUB * HD, LANES), lambda b, t: (0, 0))],
        out_specs=[pl.BlockSpec((t_blk, vw), lambda b, t: (b * nt + t, 0)),
                   pl.BlockSpec((None, N_HEADS, HD, GLA_DV), lambda b, t: (b, 0, 0, 0))],
        out_shape=[jax.ShapeDtypeStruct((batch * seq, vw), BF16),
                   jax.ShapeDtypeStruct((batch, N_HEADS, HD, GLA_DV), F32)],
        scratch_shapes=[pltpu.VMEM((N_HEADS, GLA_DV, HD), F32)],
        compiler_params=_cparams(("arbitrary", "arbitrary")),
        name="gla",
    )(z, z, z, z, za, wa_pad, b_alpha.reshape(1, kw), g_gla_out.reshape(1, vw), _diag_select_matrix())


def _rope_tables(pos):
    half = HD // 2
    inv = ROPE_THETA ** (-jnp.arange(half, dtype=F32) / half)
    ang = pos.astype(F32)[:, None] * inv[None, :]
    cos, sin = jnp.cos(ang), jnp.sin(ang)
    return jnp.concatenate([cos, cos], axis=1), jnp.concatenate([-sin, sin], axis=1)


def _rope(x, cos2, sin2):
    return x * cos2 + pltpu.roll(x, HD // 2, axis=1) * sin2


def _swa_kernel(*refs):
    (q0, q1, q2, k0, k1, k2, v0, v1, v2, sr_ref, cos_ref, sin_ref,
     u_ref, kb0, kb1, kb2, vb0, vb1, vb2,
     q_s, k_s, v_s, o_s, l_s) = refs
    q_in, k_in, v_in = (q0, q1, q2), (k0, k1, k2), (v0, v1, v2)
    kb, vb = (kb0, kb1, kb2), (vb0, vb1, vb2)
    i = pl.program_id(2)
    T = SWA_STEP
    n_grp = len(SWA_PATTERNS)

    @pl.when(i == 0)
    def _():
        k_s[:, 0:T, :] = jnp.zeros((n_grp, T, HD), F32)
        v_s[:, 0:T, :] = jnp.zeros((n_grp, T, HD), F32)

    @pl.when(i > 0)
    def _():
        k_s[:, 0:T, :] = k_s[:, T:2 * T, :]
        v_s[:, 0:T, :] = v_s[:, T:2 * T, :]

    cos2, sin2 = cos_ref[...], sin_ref[...]
    for g in range(n_grp):
        q_s[g] = _rope(q_in[g][...], cos2, sin2) * (HD ** -0.5)
        k_s[g, T:2 * T, :] = _rope(k_in[g][...], cos2, sin2)
        v_s[g, T:2 * T, :] = v_in[g][...]

    qi = lax.broadcasted_iota(jnp.int32, (SWA_BLK, 2 * SWA_BLK), 0)
    ki = lax.broadcasted_iota(jnp.int32, (SWA_BLK, 2 * SWA_BLK), 1)
    delta = SWA_BLK + qi - ki

    for g, (win, dil) in enumerate(SWA_PATTERNS):
        n_keys = win // dil
        unit = dil * SWA_BLK
        band = (delta >= 0) & (delta <= n_keys)
        shift = dil.bit_length() - 1

        def rows(start, size, dil=dil):
            return pl.ds(start, size) if dil == 1 else pl.ds(start, size, stride=dil)

        def block(n, carry, g=g, dil=dil, unit=unit, band=band, shift=shift, rows=rows):
            u = lax.shift_right_logical(n, shift)
            r = lax.bitwise_and(n, dil - 1)
            q = q_s[g, rows(u * unit + r, SWA_BLK), :]
            kk = k_s[g, rows(T + (u - 1) * unit + r, 2 * SWA_BLK), :]
            vv = v_s[g, rows(T + (u - 1) * unit + r, 2 * SWA_BLK), :]
            s = _dot_nt(q.astype(BF16), kk.astype(BF16))
            ki_min = jnp.where((i * (T // unit) + u) == 0, SWA_BLK, 0)
            s = jnp.where(band & (ki >= ki_min), s, NEG)
            m = jnp.max(s, axis=-1, keepdims=True)
            p = jnp.exp(s - m)
            den = jnp.sum(p, axis=-1, keepdims=True)
            o = _dot(p.astype(BF16), vv.astype(BF16)) / den
            lse = m + jnp.log(den)
            o_s[g, rows(u * unit + r, SWA_BLK), :] = o
            l_s[g, rows(u * unit + r, SWA_BLK), :] = jnp.broadcast_to(lse, (SWA_BLK, HD))
            return carry

        lax.fori_loop(0, T // SWA_BLK, block, 0)

    rc = 256

    def merge(c, carry):
        r0 = pl.multiple_of(c * rc, rc)
        rr = pl.ds(r0, rc)
        ls = [l_s[g, rr, :] for g in range(n_grp)]
        mx = functools.reduce(jnp.maximum, ls)
        es = [jnp.exp(l - mx) for l in ls]
        tot = functools.reduce(lambda a, b: a + b, es)
        ob = functools.reduce(lambda a, b: a + b, [(es[g] / tot) * o_s[g, rr, :] for g in range(n_grp)])
        u_ref[rr, :] = (ob * _silu(sr_ref[rr, :])).astype(u_ref.dtype)
        return carry

    lax.fori_loop(0, T // rc, merge, 0)

    @pl.when(i == pl.num_programs(2) - 1)
    def _():
        for g, (win, _) in enumerate(SWA_PATTERNS):
            kb[g][...] = k_s[g, 2 * T - win:2 * T, :]
            vb[g][...] = v_s[g, 2 * T - win:2 * T, :]


def _swa_prompt(z, cos2, sin2, batch, seq):
    T = SWA_STEP
    assert seq % T == 0 and all(w <= T for w, _ in SWA_PATTERNS)
    nt = seq // T
    w_grp = N_HEADS * HD

    def zspec(col0, g):
        cb0 = (col0 + g * w_grp) // HD
        return pl.BlockSpec((T, HD), lambda b, j, i: (b * nt + i, cb0 + j))

    in_specs = ([zspec(C_SQ, g) for g in range(3)] + [zspec(C_SK, g) for g in range(3)]
                + [zspec(C_SV, g) for g in range(3)] + [zspec(C_SR, 0)]
                + [pl.BlockSpec((T, HD), lambda b, j, i: (i, 0))] * 2)
    buf_specs = [pl.BlockSpec((None, w, HD), lambda b, j, i: (b, 0, j)) for w, _ in SWA_PATTERNS]
    buf_shapes = [jax.ShapeDtypeStruct((batch, w, w_grp), F32) for w, _ in SWA_PATTERNS]
    return pl.pallas_call(
        _swa_kernel,
        grid=(batch, N_HEADS, nt),
        in_specs=in_specs,
        out_specs=[pl.BlockSpec((T, HD), lambda b, j, i: (b * nt + i, j))] + buf_specs + buf_specs,
        out_shape=[jax.ShapeDtypeStruct((batch * seq, w_grp), BF16)] + buf_shapes + buf_shapes,
        scratch_shapes=[pltpu.VMEM((3, T, HD), F32), pltpu.VMEM((3, 2 * T, HD), F32),
                        pltpu.VMEM((3, 2 * T, HD), F32), pltpu.VMEM((3, T, HD), F32),
                        pltpu.VMEM((3, T, HD), F32)],
        compiler_params=_cparams(("arbitrary", "arbitrary", "arbitrary")),
        name="swa",
    )(*([z] * 10), cos2, sin2)


def _memattn_kernel(q_ref, mr_ref, kv_ref, u_ref):
    w = N_HEADS * HD
    for h in range(N_HEADS):
        cs = slice(h * HD, (h + 1) * HD)
        q = q_ref[:, cs] * (HD ** -0.5)
        k = kv_ref[:, cs]
        v = kv_ref[:, w + h * HD:w + (h + 1) * HD]
        s = _dot_nt(q.astype(BF16), k.astype(BF16))
        e = jnp.exp(s - jnp.max(s, axis=-1, keepdims=True))
        p = e / jnp.sum(e, axis=-1, keepdims=True)
        o = _dot(p.astype(BF16), v.astype(BF16))
        u_ref[:, cs] = (o * _silu(mr_ref[:, cs])).astype(u_ref.dtype)


def _memattn_prompt(z, mem_kv, batch, seq, *, t_blk):
    nt = seq // t_blk
    w = N_HEADS * HD
    return pl.pallas_call(
        _memattn_kernel,
        grid=(batch, nt),
        in_specs=[pl.BlockSpec((t_blk, w), lambda b, t: (b * nt + t, C_MQ // w)),
                  pl.BlockSpec((t_blk, w), lambda b, t: (b * nt + t, C_MR // w)),
                  pl.BlockSpec((N_MEM, 2 * w), lambda b, t: (b, 0))],
        out_specs=pl.BlockSpec((t_blk, w), lambda b, t: (b * nt + t, 0)),
        out_shape=jax.ShapeDtypeStruct((batch * seq, w), BF16),
        compiler_params=_cparams(("arbitrary", "arbitrary")),
        name="memattn",
    )(z, z, mem_kv)


def _final_kernel(ua_ref, ub_ref, uc_ref, gt_ref, x_ref, wa_ref, wb_ref, wc_ref, wo_ref, gf_ref, y_ref):
    d = x_ref.shape[1]
    ya = _dot(ua_ref[...].astype(BF16), wa_ref[...])
    yb = _dot(ub_ref[...].astype(BF16), wb_ref[...])
    yc = _dot(uc_ref[...].astype(BF16), wc_ref[...])
    mix = (jax.nn.sigmoid(gt_ref[:, 0:d]) * ya + jax.nn.sigmoid(gt_ref[:, d:2 * d]) * yb
           + jax.nn.sigmoid(gt_ref[:, 2 * d:3 * d]) * yc)
    xo = x_ref[...] + _dot(mix.astype(BF16), wo_ref[...])
    y = xo * lax.rsqrt(jnp.mean(xo * xo, axis=-1, keepdims=True) + EPS)
    y_ref[...] = y * gf_ref[...]


def _final(ua, ub, uc, z, x, wa, wb, wc, wo, g_final, *, tm):
    m, d = x.shape
    const = lambda i: (0, 0)
    resident = dict(pipeline_mode=pl.Buffered(1))
    return pl.pallas_call(
        _final_kernel,
        grid=(m // tm,),
        in_specs=[pl.BlockSpec((tm, ua.shape[1]), lambda i: (i, 0)),
                  pl.BlockSpec((tm, ub.shape[1]), lambda i: (i, 0)),
                  pl.BlockSpec((tm, uc.shape[1]), lambda i: (i, 0)),
                  pl.BlockSpec((tm, 3 * d), lambda i: (i, C_GT // (3 * d))),
                  pl.BlockSpec((tm, d), lambda i: (i, 0)),
                  pl.BlockSpec(wa.shape, const, **resident),
                  pl.BlockSpec(wb.shape, const, **resident),
                  pl.BlockSpec(wc.shape, const, **resident),
                  pl.BlockSpec(wo.shape, const, **resident),
                  pl.BlockSpec((1, d), const)],
        out_specs=pl.BlockSpec((tm, d), lambda i: (i, 0)),
        out_shape=jax.ShapeDtypeStruct((m, d), F32),
        compiler_params=_cparams(("arbitrary",)),
        name="final",
    )(ua, ub, uc, z, x, wa, wb, wc, wo, g_final.reshape(1, d))


def _heads_rows(row, col0):
    return jnp.concatenate([row[:, col0 + h * HD:col0 + (h + 1) * HD] for h in range(N_HEADS)], axis=0)


def _decode_attention(q4, kk, vv, k_new=None, v_new=None):
    s = jnp.sum(kk * q4[None], axis=-1, keepdims=True)
    m = jnp.max(s, axis=0)
    if k_new is not None:
        s_new = jnp.sum(k_new * q4, axis=-1, keepdims=True)
        m = jnp.maximum(m, s_new)
    p = jnp.exp(s - m[None])
    den = jnp.sum(p, axis=0)
    acc = jnp.sum(p * vv, axis=0)
    if k_new is not None:
        p_new = jnp.exp(s_new - m)
        den = den + p_new
        acc = acc + p_new * v_new
    return acc / den, m + jnp.log(den)


KV_ROWS = 2 * N_HEADS
SHIFT_ROWS = tuple((w - 1) * KV_ROWS for w, _ in SWA_PATTERNS)
SHIFT_OFFS = tuple(sum(SHIFT_ROWS[:g]) for g in range(len(SWA_PATTERNS)))


def _sample_kernel(*refs):
    (z_ref, za_ref, wa_ref, ba_ref, gg_ref, cos_ref, sin_ref, st_ref,
     cg0, cg1, cg2, cm_ref, ca0, ca1, ca2,
     ua_ref, ub_ref, uc_ref, sout_ref, co0, co1, co2,
     new_s, stage, in_sem, out_sem, row_sem) = refs
    cg, ca, co = (cg0, cg1, cg2), (ca0, ca1, ca2), (co0, co1, co2)
    b = pl.program_id(0)
    nb = pl.num_programs(0)
    slot = lax.rem(b, 2)
    zrow = z_ref[pl.ds(b, 1), :]

    def in_copy(g, bb, sl):
        return pltpu.make_async_copy(ca[g].at[bb, pl.ds(KV_ROWS, SHIFT_ROWS[g])],
                                     stage.at[sl, pl.ds(SHIFT_OFFS[g], SHIFT_ROWS[g])], in_sem.at[sl, g])

    def out_copy(g, bb, sl):
        return pltpu.make_async_copy(stage.at[sl, pl.ds(SHIFT_OFFS[g], SHIFT_ROWS[g])],
                                     co[g].at[bb, pl.ds(0, SHIFT_ROWS[g])], out_sem.at[sl, g])

    def row_copy(g):
        return pltpu.make_async_copy(new_s.at[g], co[g].at[b, pl.ds(SHIFT_ROWS[g], KV_ROWS)], row_sem.at[g])

    @pl.when(b == 0)
    def _():
        for g in range(3):
            in_copy(g, 0, 0).start()

    @pl.when(b >= 1)
    def _():
        for g in range(3):
            out_copy(g, b - 1, 1 - slot).wait()

    @pl.when(b + 1 < nb)
    def _():
        for g in range(3):
            in_copy(g, b + 1, 1 - slot).start()

    ga8 = jnp.broadcast_to(za_ref[pl.ds(b, 1), :], (8, LANES))
    xa = _dot(ga8.astype(BF16), wa_ref[...])[0:1, :] + ba_ref[...]
    a_row = jnp.exp(_log_sigmoid(xa) / GLA_TAU)
    eye = (lax.broadcasted_iota(jnp.int32, (HD, HD), 0) == lax.broadcasted_iota(jnp.int32, (HD, HD), 1))

    def col(row_vec):
        return jnp.sum(jnp.where(eye, jnp.broadcast_to(row_vec, (HD, HD)), 0.0), axis=1, keepdims=True)

    for h in range(N_HEADS):
        q = zrow[:, C_GQ + h * HD:C_GQ + (h + 1) * HD] * (HD ** -0.5)
        k = zrow[:, C_GK + h * HD:C_GK + (h + 1) * HD]
        v = zrow[:, C_GV + h * GLA_DV:C_GV + (h + 1) * GLA_DV]
        s_new = col(a_row[:, h * HD:(h + 1) * HD]) * st_ref[h] + col(k) * v
        sout_ref[h] = s_new
        o = jnp.sum(col(q) * s_new, axis=0, keepdims=True)
        y = o * lax.rsqrt(jnp.mean(o * o, axis=-1, keepdims=True) + EPS)
        y = y * gg_ref[:, h * GLA_DV:(h + 1) * GLA_DV]
        gate = zrow[:, C_GR + h * GLA_DV:C_GR + (h + 1) * GLA_DV]
        ua_ref[:, h * GLA_DV:(h + 1) * GLA_DV] = y * _silu(gate)

    cos2, sin2 = cos_ref[...], sin_ref[...]
    outs, lses = [], []
    w_grp = N_HEADS * HD
    for g in range(3):
        q4 = _rope(_heads_rows(zrow, C_SQ + g * w_grp), cos2, sin2) * (HD ** -0.5)
        k4 = _rope(_heads_rows(zrow, C_SK + g * w_grp), cos2, sin2)
        v4 = _heads_rows(zrow, C_SV + g * w_grp)
        new_s[g, 0:N_HEADS, :] = k4
        new_s[g, N_HEADS:KV_ROWS, :] = v4
        row_copy(g).start()
        o, lse = _decode_attention(q4, cg[g][:, 0:N_HEADS, :], cg[g][:, N_HEADS:KV_ROWS, :], k4, v4)
        outs.append(o)
        lses.append(lse)
    mx = functools.reduce(jnp.maximum, lses)
    es = [jnp.exp(l - mx) for l in lses]
    tot = es[0] + es[1] + es[2]
    ob = (es[0] / tot) * outs[0] + (es[1] / tot) * outs[1] + (es[2] / tot) * outs[2]
    for h in range(N_HEADS):
        gate = zrow[:, C_SR + h * HD:C_SR + (h + 1) * HD]
        ub_ref[:, h * HD:(h + 1) * HD] = ob[h:h + 1, :] * _silu(gate)

    qm = _heads_rows(zrow, C_MQ) * (HD ** -0.5)
    oc, _ = _decode_attention(qm, cm_ref[:, 0:N_HEADS, :], cm_ref[:, N_HEADS:KV_ROWS, :])
    for h in range(N_HEADS):
        gate = zrow[:, C_MR + h * HD:C_MR + (h + 1) * HD]
        uc_ref[:, h * HD:(h + 1) * HD] = oc[h:h + 1, :] * _silu(gate)

    for g in range(3):
        in_copy(g, b, slot).wait()
        out_copy(g, b, slot).start()
    for g in range(3):
        row_copy(g).wait()

    @pl.when(b == nb - 1)
    def _():
        for g in range(3):
            out_copy(g, b, slot).wait()


def _sample_mixers(z, za, wa_pad, b_alpha, g_gla_out, cos2, sin2, state, caches, cache_mem):
    db = z.shape[0]
    kw, vw, w = N_HEADS * HD, N_HEADS * GLA_DV, N_HEADS * HD
    const2 = lambda b: (0, 0)
    flat, gathered, gather_specs = [], [], []
    for c, (win, dil) in zip(caches, SWA_PATTERNS):
        assert c.shape == (1, db, win, 2, N_HEADS, HD)
        n_keys = win // dil
        flat.append(c.reshape(db, win * KV_ROWS, HD))
        gathered.append(c.reshape(db, n_keys, dil * KV_ROWS, HD))
        gather_specs.append(pl.BlockSpec((None, n_keys, KV_ROWS, HD), lambda b: (b, 0, 0, 0)))
    any_spec = pl.BlockSpec(memory_space=pl.ANY)
    row3 = lambda n: pl.BlockSpec((None, 1, n), lambda b: (b, 0, 0))
    res = pl.pallas_call(
        _sample_kernel,
        grid=(db,),
        in_specs=[pl.BlockSpec(z.shape, const2), pl.BlockSpec(za.shape, const2),
                  pl.BlockSpec(wa_pad.shape, const2), pl.BlockSpec((1, kw), const2),
                  pl.BlockSpec((1, vw), const2), pl.BlockSpec((1, HD), const2), pl.BlockSpec((1, HD), const2),
                  pl.BlockSpec((None, None, N_HEADS, HD, GLA_DV), lambda b: (0, b, 0, 0, 0))]
                 + gather_specs
                 + [pl.BlockSpec((None, N_MEM, KV_ROWS, HD), lambda b: (b, 0, 0, 0))]
                 + [any_spec] * 3,
        out_specs=[row3(vw), row3(w), row3(w),
                   pl.BlockSpec((None, None, N_HEADS, HD, GLA_DV), lambda b: (0, b, 0, 0, 0))] + [any_spec] * 3,
        out_shape=[jax.ShapeDtypeStruct((db, 1, vw), F32), jax.ShapeDtypeStruct((db, 1, w), F32),
                   jax.ShapeDtypeStruct((db, 1, w), F32), jax.ShapeDtypeStruct(state.shape, F32)]
                  + [jax.ShapeDtypeStruct(c.shape, F32) for c in flat],
        scratch_shapes=[pltpu.VMEM((3, KV_ROWS, HD), F32), pltpu.VMEM((2, sum(SHIFT_ROWS), HD), F32),
                        pltpu.SemaphoreType.DMA((2, 3)), pltpu.SemaphoreType.DMA((2, 3)),
                        pltpu.SemaphoreType.DMA((3,))],
        compiler_params=_cparams(("arbitrary",)),
        name="sample_mixers",
    )(z, za, wa_pad, b_alpha.reshape(1, kw), g_gla_out.reshape(1, vw), cos2, sin2, state,
      *gathered, cache_mem.reshape(db, N_MEM, KV_ROWS, HD), *flat)
    ua, ub, uc, s_out = res[:4]
    swa_out = [o.reshape(c.shape) for o, c in zip(res[4:], caches)]
    return ua.reshape(db, vw), ub.reshape(db, w), uc.reshape(db, w), s_out, swa_out


def kernel(x_prompt, x_sample, mem_prompt, state_gla, cache_swa_w128, cache_swa_w512, cache_swa_w2048, cache_mem_kv, g_norm, w_in, w_alpha2, b_alpha, g_gla_out, g_mem, w_mem_kv, w_proj_a, w_proj_b, w_proj_c, w_out, g_final):
    batch, seq, d = x_prompt.shape
    db, dec_seq, _ = x_sample.shape
    assert g_norm.shape[0] == 1 and dec_seq == 1
    w_grp = N_HEADS * HD

    c_ga = 2 * N_HEADS * HD + 2 * N_HEADS * GLA_DV
    w_full = w_in[0]
    c_gt = w_full.shape[1] - 3 * d
    w_main = jnp.concatenate([w_full[:, c_gt:], w_full[:, :c_ga], w_full[:, c_ga + GLA_RANK:c_gt]],
                             axis=1).astype(BF16)
    w_ga = jnp.pad(w_full[:, c_ga:c_ga + GLA_RANK], ((0, 0), (0, LANES - GLA_RANK))).astype(BF16)
    wa_pad = jnp.pad(w_alpha2[0], ((0, LANES - GLA_RANK), (0, 0))).astype(BF16)
    wpa, wpb, wpc, wo = (w[0].astype(BF16) for w in (w_proj_a, w_proj_b, w_proj_c, w_out))
    assert w_main.shape[1] == N_MAIN

    xp = x_prompt.reshape(batch * seq, d)
    xs = x_sample.reshape(db, d)

    z, za = _norm_matmul(xp, g_norm[0], w_main, w_ga, tm=1024, tn=1024)
    mem_kv = _norm_matmul(mem_prompt.reshape(batch * N_MEM, d), g_mem[0], w_mem_kv[0].astype(BF16),
                          tm=batch * N_MEM, tn=512)
    ua, gla_p = _gla_prompt(z, za, wa_pad, b_alpha[0], g_gla_out[0], batch, seq, t_blk=512)
    cos_p, sin_p = _rope_tables(jnp.arange(seq, dtype=jnp.int32))
    swa_res = _swa_prompt(z, cos_p, sin_p, batch, seq)
    ub, kbufs, vbufs = swa_res[0], swa_res[1:4], swa_res[4:7]
    uc = _memattn_prompt(z, mem_kv, batch, seq, t_blk=512)
    y_prompt = _final(ua, ub, uc, z, xp, wpa, wpb, wpc, wo, g_final, tm=256).reshape(batch, seq, d)

    zs, zas = _norm_matmul(xs, g_norm[0], w_main, w_ga, tm=db, tn=1024)
    cos_s, sin_s = _rope_tables(jnp.full((1,), PAST_LEN, jnp.int32))
    uas, ubs, ucs, gla_s, swa_s = _sample_mixers(
        zs, zas, wa_pad, b_alpha[0], g_gla_out[0], cos_s, sin_s, state_gla,
        (cache_swa_w128, cache_swa_w512, cache_swa_w2048), cache_mem_kv)
    y_sample = _final(uas, ubs, ucs, zs, xs, wpa, wpb, wpc, wo, g_final, tm=db).reshape(db, 1, d)

    swa_p = [jnp.stack([k.reshape(batch, w, N_HEADS, HD), v.reshape(batch, w, N_HEADS, HD)], axis=2)[None]
             for k, v, (w, _) in zip(kbufs, vbufs, SWA_PATTERNS)]
    mem_kv_prompt = mem_kv.reshape(1, batch, N_MEM, 2, N_HEADS, HD)
    return (y_prompt, y_sample, gla_p[None], swa_p[0], swa_p[1], swa_p[2], mem_kv_prompt,
            gla_s, swa_s[0], swa_s[1], swa_s[2])
```

```python
import functools

import numpy as np
import jax
import jax.numpy as jnp
from jax import lax
from jax.experimental import pallas as pl
from jax.experimental.pallas import tpu as pltpu

F32 = jnp.float32
BF16 = jnp.bfloat16

EPS = 1e-6
ROPE_THETA = 10000.0
NEG = -1e30
PAST_LEN = 16384

HD = 128
N_HEADS = 4
GLA_DV = 256
GLA_RANK = 16
GLA_TAU = 16.0
GLA_CHUNK = 64
GLA_SUB = 16
SWA_PATTERNS = ((128, 1), (512, 4), (2048, 16))
SWA_BLK = 128
SWA_STEP = 2048
SWA_UNROLL = 4
N_MEM = 256

LANES = 128
VMEM_LIMIT = 56 * 1024 * 1024

C_GT = 0
C_GQ, C_GK, C_GV, C_GR = 6144, 6656, 7168, 8192
C_SQ, C_SK, C_SV, C_SR = 9216, 10752, 12288, 13824
C_MQ, C_MR = 14336, 14848
N_MAIN = 15360


def _cparams(sem):
    return pltpu.CompilerParams(dimension_semantics=sem, vmem_limit_bytes=VMEM_LIMIT)


def _dot(a, b):
    return jnp.dot(a, b, preferred_element_type=F32)


def _dot_nt(a, b):
    return lax.dot_general(a, b, (((1,), (1,)), ((), ())), preferred_element_type=F32)


def _dot_tn(a, b):
    return lax.dot_general(a, b, (((0,), (0,)), ((), ())), preferred_element_type=F32)


def _silu(x):
    return x * jax.nn.sigmoid(x)


def _log_sigmoid(x):
    return jnp.minimum(x, 0.0) - jnp.log1p(jnp.exp(-jnp.abs(x)))


KV_ROWS = 2 * N_HEADS
SHIFT_ROWS = tuple((w - 1) * KV_ROWS for w, _ in SWA_PATTERNS)
SHIFT_PIECE = 4096


def _shift_pieces():
    pieces = []
    for g, total in enumerate(SHIFT_ROWS):
        for r0 in range(0, total, SHIFT_PIECE):
            pieces.append((g, r0, min(SHIFT_PIECE, total - r0)))
    return tuple(pieces)


def _cache_shift_step(p, n_batch, ca, nr, co, stage, in_sem, out_sem, row_sem):
    pieces = _shift_pieces()
    n_types = len(pieces)
    n_pieces = n_types * n_batch

    def in_copy(t, bb, sl):
        g, r0, n = pieces[t]
        return pltpu.make_async_copy(ca[g].at[bb, pl.ds(KV_ROWS + r0, n)], stage.at[sl, pl.ds(0, n)], in_sem.at[sl])

    def out_copy(t, bb, sl):
        g, r0, n = pieces[t]
        return pltpu.make_async_copy(stage.at[sl, pl.ds(0, n)], co[g].at[bb, pl.ds(r0, n)], out_sem.at[sl])

    def row_copy(t, bb, sl):
        g = pieces[t][0]
        return pltpu.make_async_copy(nr[g].at[bb], co[g].at[bb, pl.ds(SHIFT_ROWS[g], KV_ROWS)], row_sem.at[sl])

    def for_piece(q, cond, fn):
        bb, ty, sl = q // n_types, lax.rem(q, n_types), lax.rem(q, 2)
        for t in range(n_types):
            @pl.when(cond & (ty == t))
            def _(t=t):
                fn(t, bb, sl)

    def finish_write(t, bb, sl):
        out_copy(t, bb, sl).wait()
        if pieces[t][1] == 0:
            row_copy(t, bb, sl).wait()

    def start_write(t, bb, sl):
        in_copy(t, bb, sl).wait()
        out_copy(t, bb, sl).start()
        if pieces[t][1] == 0:
            row_copy(t, bb, sl).start()

    for_piece(p, p == 0, lambda t, bb, sl: in_copy(t, bb, sl).start())
    for_piece(jnp.maximum(p - 1, 0), (p >= 1) & (p <= n_pieces), finish_write)
    for_piece(p + 1, p + 1 < n_pieces, lambda t, bb, sl: in_copy(t, bb, sl).start())
    for_piece(p, p < n_pieces, start_write)


def _norm_matmul_kernel(*refs, has_extra, n_shift, row_chunk, w_rows_out):
    mm = _dot_nt if w_rows_out else _dot
    n_in = 3 + has_extra + 2 * n_shift
    n_out = 1 + has_extra + n_shift
    x_ref, g_ref, w_ref = refs[:3]
    o_ref = refs[n_in]
    h_ref = refs[n_in + n_out]
    if has_extra:
        wx_ref, ox_ref = refs[3], refs[n_in + 1]
    tm = x_ref.shape[0]

    if n_shift:
        ca = refs[3 + has_extra:3 + has_extra + n_shift]
        nr = refs[3 + has_extra + n_shift:n_in]
        co = refs[n_in + 1 + has_extra:n_in + n_out]
        stage, in_sem, out_sem, row_sem = refs[n_in + n_out + 1:]
        step = pl.program_id(0) * pl.num_programs(1) + pl.program_id(1)
        _cache_shift_step(step, ca[0].shape[0], ca, nr, co, stage, in_sem, out_sem, row_sem)

    @pl.when(pl.program_id(1) == 0)
    def _():
        def body(c, carry):
            r0 = pl.multiple_of(c * row_chunk, row_chunk)
            x = x_ref[pl.ds(r0, row_chunk), :]
            y = x * lax.rsqrt(jnp.mean(x * x, axis=-1, keepdims=True) + EPS)
            h_ref[pl.ds(r0, row_chunk), :] = (y * g_ref[...]).astype(BF16)
            return carry
        lax.fori_loop(0, tm // row_chunk, body, 0)
        if has_extra:
            ox_ref[...] = mm(h_ref[...], wx_ref[...])

    o_ref[...] = mm(h_ref[...], w_ref[...])


def _norm_matmul(x, g, w, wx=None, shift=None, *, w_rows_out=False, tm, tn):
    m, d = x.shape
    n = w.shape[0] if w_rows_out else w.shape[1]
    assert m % tm == 0 and n % tn == 0
    has_extra = wx is not None
    n_shift = 0 if shift is None else len(shift[0])
    row_chunk = min(tm, 128)
    w_spec = pl.BlockSpec((tn, d), lambda i, j: (j, 0)) if w_rows_out else pl.BlockSpec((d, tn), lambda i, j: (0, j))
    in_specs = [pl.BlockSpec((tm, d), lambda i, j: (i, 0)),
                pl.BlockSpec((1, d), lambda i, j: (0, 0)),
                w_spec]
    out_specs = [pl.BlockSpec((tm, tn), lambda i, j: (i, j))]
    out_shape = [jax.ShapeDtypeStruct((m, n), F32)]
    args = [x, g.reshape(1, d), w]
    if has_extra:
        nx = wx.shape[0] if w_rows_out else wx.shape[1]
        in_specs.append(pl.BlockSpec(wx.shape, lambda i, j: (0, 0)))
        out_specs.append(pl.BlockSpec((tm, nx), lambda i, j: (i, 0)))
        out_shape.append(jax.ShapeDtypeStruct((m, nx), F32))
        args.append(wx)
    scratch = [pltpu.VMEM((tm, d), BF16)]
    if n_shift:
        caches, new_rows = shift
        assert len(_shift_pieces()) * caches[0].shape[0] < (m // tm) * (n // tn)
        any_spec = pl.BlockSpec(memory_space=pl.ANY)
        in_specs += [any_spec] * n_shift + [pl.BlockSpec(r.shape, lambda i, j: (0, 0, 0)) for r in new_rows]
        out_specs += [any_spec] * n_shift
        out_shape += [jax.ShapeDtypeStruct(c.shape, c.dtype) for c in caches]
        args += list(caches) + list(new_rows)
        scratch += [pltpu.VMEM((2, SHIFT_PIECE, HD), F32)] + [pltpu.SemaphoreType.DMA((2,))] * 3
    res = pl.pallas_call(
        functools.partial(_norm_matmul_kernel, has_extra=has_extra, n_shift=n_shift, row_chunk=row_chunk,
                          w_rows_out=w_rows_out),
        grid=(m // tm, n // tn),
        in_specs=in_specs, out_specs=out_specs, out_shape=out_shape,
        scratch_shapes=scratch,
        compiler_params=_cparams(("arbitrary", "arbitrary")),
        name="norm_matmul",
    )(*args)
    return res if (has_extra or n_shift) else res[0]


def _diag_select_matrix():
    rows = np.arange(GLA_SUB * HD)[:, None] // HD
    cols = np.arange(LANES)[None, :] % GLA_SUB
    return jnp.asarray((rows == cols).astype(np.float32), dtype=BF16)


def _gla_kernel(q_ref, k_ref, v_ref, gr_ref, ga_ref, wa_ref, ba_ref, gg_ref, em_ref,
                u_ref, sout_ref, st_ref):
    t_blk = pl.program_id(1)
    n_chunks = q_ref.shape[0] // GLA_CHUNK
    C, SUB, n_sub = GLA_CHUNK, GLA_SUB, GLA_CHUNK // GLA_SUB

    @pl.when(t_blk == 0)
    def _():
        st_ref[...] = jnp.zeros_like(st_ref)

    row = lax.broadcasted_iota(jnp.int32, (C, C), 0)
    col = lax.broadcasted_iota(jnp.int32, (C, C), 1)
    tri = (col <= row).astype(F32)
    row_c = lax.broadcasted_iota(jnp.int32, (C, HD), 0)
    sub_row = lax.broadcasted_iota(jnp.int32, (SUB, HD), 0)
    lane_c = lax.broadcasted_iota(jnp.int32, (SUB, C), 1)

    def chunk(c, carry):
        r0 = pl.multiple_of(c * C, C)
        rows = pl.ds(r0, C)
        xa = _dot(ga_ref[rows, :].astype(BF16), wa_ref[...]) + ba_ref[...]
        la = _log_sigmoid(xa) / GLA_TAU
        b_all = jnp.dot(tri, la, preferred_element_type=F32, precision=lax.Precision.HIGHEST)

        p_rows = []
        a_off = []
        per_head = []
        for h in range(N_HEADS):
            q = q_ref[rows, h * HD:(h + 1) * HD] * (HD ** -0.5)
            k = k_ref[rows, h * HD:(h + 1) * HD]
            b = b_all[:, h * HD:(h + 1) * HD]
            per_head.append((q, k, b))
            for i in range(n_sub):
                sl = slice(i * SUB, (i + 1) * SUB)
                q_i, k_i, b_i = q[sl], k[sl], b[sl]
                if i == 0:
                    a_off.append(jnp.zeros((SUB, C), F32))
                else:
                    b_ref_row = b[i * SUB - 1:i * SUB, :]
                    qs = q_i * jnp.exp(b_i - b_ref_row)
                    ks = k * jnp.exp(jnp.where(row_c < i * SUB, b_ref_row - b, -jnp.inf))
                    a_off.append(_dot_nt(qs.astype(BF16), ks.astype(BF16)))
                slabs = []
                for s in range(SUB):
                    e = jnp.where(sub_row >= s, b_i - b_i[s:s + 1, :], -jnp.inf)
                    slabs.append((q_i * k_i[s:s + 1, :]) * jnp.exp(e))
                p_rows.append(jnp.concatenate(slabs, axis=1).astype(BF16))
        r_all = _dot(jnp.concatenate(p_rows, axis=0), em_ref[...])

        for h in range(N_HEADS):
            q, k, b = per_head[h]
            v = v_ref[rows, h * GLA_DV:(h + 1) * GLA_DV]
            a_rows = []
            for i in range(n_sub):
                idx = h * n_sub + i
                r_i = r_all[idx * SUB:(idx + 1) * SUB, :C]
                in_blk = (lane_c >= i * SUB) & (lane_c < (i + 1) * SUB)
                a_rows.append(a_off[idx] + jnp.where(in_blk, r_i, 0.0))
            a = jnp.concatenate(a_rows, axis=0)
            st = st_ref[h]
            o = _dot(a.astype(BF16), v.astype(BF16)) + _dot_nt((q * jnp.exp(b)).astype(BF16), st.astype(BF16))
            b_end = b[C - 1:C, :]
            kd = k * jnp.exp(b_end - b)
            st_ref[h] = st * jnp.exp(b_end) + _dot_tn(v.astype(BF16), kd.astype(BF16))
            y = o * lax.rsqrt(jnp.mean(o * o, axis=-1, keepdims=True) + EPS)
            y = y * gg_ref[:, h * GLA_DV:(h + 1) * GLA_DV]
            gate = gr_ref[rows, h * GLA_DV:(h + 1) * GLA_DV]
            u_ref[rows, h * GLA_DV:(h + 1) * GLA_DV] = (y * _silu(gate)).astype(u_ref.dtype)
        return carry

    lax.fori_loop(0, n_chunks, chunk, 0)

    @pl.when(t_blk == pl.num_programs(1) - 1)
    def _():
        for h in range(N_HEADS):
            sout_ref[h] = st_ref[h].T


def _gla_prompt(z, za, wa_pad, b_alpha, g_gla_out, batch, seq, *, t_blk):
    nt = seq // t_blk
    assert seq % t_blk == 0 and t_blk % GLA_CHUNK == 0
    kw = N_HEADS * HD
    vw = N_HEADS * GLA_DV

    def zspec(width, col0):
        cb = col0 // width
        return pl.BlockSpec((t_blk, width), lambda b, t: (b * nt + t, cb))

    return pl.pallas_call(
        _gla_kernel,
        grid=(batch, nt),
        in_specs=[zspec(kw, C_GQ), zspec(kw, C_GK), zspec(vw, C_GV), zspec(vw, C_GR),
                  pl.BlockSpec((t_blk, LANES), lambda b, t: (b * nt + t, 0)),
                  pl.BlockSpec((LANES, kw), lambda b, t: (0, 0)),
                  pl.BlockSpec((1, kw), lambda b, t: (0, 0)),
                  pl.BlockSpec((1, vw), lambda b, t: (0, 0)),
                  pl.BlockSpec((GLA_SUB * HD, LANES), lambda b, t: (0, 0))],
        out_specs=[pl.BlockSpec((t_blk, vw), lambda b, t: (b * nt + t, 0)),
                   pl.BlockSpec((None, N_HEADS, HD, GLA_DV), lambda b, t: (b, 0, 0, 0))],
        out_shape=[jax.ShapeDtypeStruct((batch * seq, vw), BF16),
                   jax.ShapeDtypeStruct((batch, N_HEADS, HD, GLA_DV), F32)],
        scratch_shapes=[pltpu.VMEM((N_HEADS, GLA_DV, HD), F32)],
        compiler_params=_cparams(("arbitrary", "arbitrary")),
        name="gla",
    )(z, z, z, z, za, wa_pad, b_alpha.reshape(1, kw), g_gla_out.reshape(1, vw), _diag_select_matrix())


def _rope_tables(pos):
    half = HD // 2
    inv = ROPE_THETA ** (-jnp.arange(half, dtype=F32) / half)
    ang = pos.astype(F32)[:, None] * inv[None, :]
    cos, sin = jnp.cos(ang), jnp.sin(ang)
    return jnp.concatenate([cos, cos], axis=1), jnp.concatenate([-sin, sin], axis=1)


def _rope(x, cos2, sin2):
    return x * cos2 + pltpu.roll(x, HD // 2, axis=1) * sin2


def _swa_kernel(*refs):
    (q0, q1, q2, k0, k1, k2, v0, v1, v2, sr_ref, cos_ref, sin_ref,
     u_ref, kb0, kb1, kb2, vb0, vb1, vb2,
     q_s, k_s, v_s, o_s, l_s) = refs
    q_in, k_in, v_in = (q0, q1, q2), (k0, k1, k2), (v0, v1, v2)
    kb, vb = (kb0, kb1, kb2), (vb0, vb1, vb2)
    i = pl.program_id(2)
    T = SWA_STEP
    n_grp = len(SWA_PATTERNS)

    @pl.when(i == 0)
    def _():
        for g, (_, dil) in enumerate(SWA_PATTERNS):
            unit = dil * SWA_BLK
            k_s[g, T - unit:T, :] = jnp.zeros((unit, HD), F32)
            v_s[g, T - unit:T, :] = jnp.zeros((unit, HD), F32)

    @pl.when(i > 0)
    def _():
        for g, (_, dil) in enumerate(SWA_PATTERNS):
            unit = dil * SWA_BLK
            k_s[g, T - unit:T, :] = k_s[g, 2 * T - unit:2 * T, :]
            v_s[g, T - unit:T, :] = v_s[g, 2 * T - unit:2 * T, :]

    cos2, sin2 = cos_ref[...], sin_ref[...]
    for g in range(n_grp):
        q_s[g] = _rope(q_in[g][...], cos2, sin2) * (HD ** -0.5)
        k_s[g, T:2 * T, :] = _rope(k_in[g][...], cos2, sin2)
        v_s[g, T:2 * T, :] = v_in[g][...]

    qi = lax.broadcasted_iota(jnp.int32, (SWA_BLK, 2 * SWA_BLK), 0)
    ki = lax.broadcasted_iota(jnp.int32, (SWA_BLK, 2 * SWA_BLK), 1)
    delta = SWA_BLK + qi - ki

    def block(g, n):
        win, dil = SWA_PATTERNS[g]
        unit = dil * SWA_BLK
        u = lax.shift_right_logical(n, dil.bit_length() - 1)
        r = lax.bitwise_and(n, dil - 1)

        def rows(start, size):
            return pl.ds(start, size) if dil == 1 else pl.ds(start, size, stride=dil)

        q = q_s[g, rows(u * unit + r, SWA_BLK), :]
        kk = k_s[g, rows(T + (u - 1) * unit + r, 2 * SWA_BLK), :]
        vv = v_s[g, rows(T + (u - 1) * unit + r, 2 * SWA_BLK), :]
        s = _dot_nt(q.astype(BF16), kk.astype(BF16))
        ki_min = jnp.where((i * (T // unit) + u) == 0, SWA_BLK, 0)
        s = jnp.where((delta >= 0) & (delta <= win // dil) & (ki >= ki_min), s, NEG)
        m = jnp.max(s, axis=-1, keepdims=True)
        p = jnp.exp(s - m)
        den = jnp.sum(p, axis=-1, keepdims=True)
        o = _dot(p.astype(BF16), vv.astype(BF16)) / den
        lse = m + jnp.log(den)
        o_s[g, rows(u * unit + r, SWA_BLK), :] = o
        l_s[g, rows(u * unit + r, SWA_BLK), :] = jnp.broadcast_to(lse, (SWA_BLK, HD))

    def blocks(it, carry):
        for g in range(n_grp):
            for j in range(SWA_UNROLL):
                block(g, it * SWA_UNROLL + j)
        return carry

    lax.fori_loop(0, T // SWA_BLK // SWA_UNROLL, blocks, 0)

    rc = 256

    def merge(c, carry):
        r0 = pl.multiple_of(c * rc, rc)
        rr = pl.ds(r0, rc)
        ls = [l_s[g, rr, :] for g in range(n_grp)]
        mx = functools.reduce(jnp.maximum, ls)
        es = [jnp.exp(l - mx) for l in ls]
        tot = functools.reduce(lambda a, b: a + b, es)
        ob = functools.reduce(lambda a, b: a + b, [(es[g] / tot) * o_s[g, rr, :] for g in range(n_grp)])
        u_ref[rr, :] = (ob * _silu(sr_ref[rr, :])).astype(u_ref.dtype)
        return carry

    lax.fori_loop(0, T // rc, merge, 0)

    @pl.when(i == pl.num_programs(2) - 1)
    def _():
        for g, (win, _) in enumerate(SWA_PATTERNS):
            kb[g][...] = k_s[g, 2 * T - win:2 * T, :]
            vb[g][...] = v_s[g, 2 * T - win:2 * T, :]


def _swa_prompt(z, cos2, sin2, batch, seq):
    T = SWA_STEP
    assert seq % T == 0 and all(w <= T for w, _ in SWA_PATTERNS)
    nt = seq // T
    w_grp = N_HEADS * HD

    def zspec(col0, g):
        cb0 = (col0 + g * w_grp) // HD
        return pl.BlockSpec((T, HD), lambda b, j, i: (b * nt + i, cb0 + j))

    in_specs = ([zspec(C_SQ, g) for g in range(3)] + [zspec(C_SK, g) for g in range(3)]
                + [zspec(C_SV, g) for g in range(3)] + [zspec(C_SR, 0)]
                + [pl.BlockSpec((T, HD), lambda b, j, i: (i, 0))] * 2)
    buf_specs = [pl.BlockSpec((None, w, HD), lambda b, j, i: (b, 0, j)) for w, _ in SWA_PATTERNS]
    buf_shapes = [jax.ShapeDtypeStruct((batch, w, w_grp), F32) for w, _ in SWA_PATTERNS]
    return pl.pallas_call(
        _swa_kernel,
        grid=(batch, N_HEADS, nt),
        in_specs=in_specs,
        out_specs=[pl.BlockSpec((T, HD), lambda b, j, i: (b * nt + i, j))] + buf_specs + buf_specs,
        out_shape=[jax.ShapeDtypeStruct((batch * seq, w_grp), BF16)] + buf_shapes + buf_shapes,
        scratch_shapes=[pltpu.VMEM((3, T, HD), F32), pltpu.VMEM((3, 2 * T, HD), F32),
                        pltpu.VMEM((3, 2 * T, HD), F32), pltpu.VMEM((3, T, HD), F32),
                        pltpu.VMEM((3, T, HD), F32)],
        compiler_params=_cparams(("arbitrary", "arbitrary", "arbitrary")),
        name="swa",
    )(*([z] * 10), cos2, sin2)


def _memattn_kernel(q_ref, mr_ref, kv_ref, u_ref):
    w = N_HEADS * HD
    for h in range(N_HEADS):
        cs = slice(h * HD, (h + 1) * HD)
        q = q_ref[:, cs] * (HD ** -0.5)
        k = kv_ref[:, cs]
        v = kv_ref[:, w + h * HD:w + (h + 1) * HD]
        s = _dot_nt(q.astype(BF16), k.astype(BF16))
        e = jnp.exp(s - jnp.max(s, axis=-1, keepdims=True))
        p = e / jnp.sum(e, axis=-1, keepdims=True)
        o = _dot(p.astype(BF16), v.astype(BF16))
        u_ref[:, cs] = (o * _silu(mr_ref[:, cs])).astype(u_ref.dtype)


def _memattn_prompt(z, mem_kv, batch, seq, *, t_blk):
    nt = seq // t_blk
    w = N_HEADS * HD
    return pl.pallas_call(
        _memattn_kernel,
        grid=(batch, nt),
        in_specs=[pl.BlockSpec((t_blk, w), lambda b, t: (b * nt + t, C_MQ // w)),
                  pl.BlockSpec((t_blk, w), lambda b, t: (b * nt + t, C_MR // w)),
                  pl.BlockSpec((N_MEM, 2 * w), lambda b, t: (b, 0))],
        out_specs=pl.BlockSpec((t_blk, w), lambda b, t: (b * nt + t, 0)),
        out_shape=jax.ShapeDtypeStruct((batch * seq, w), BF16),
        compiler_params=_cparams(("arbitrary", "arbitrary")),
        name="memattn",
    )(z, z, mem_kv)


def _final_kernel(ua_ref, ub_ref, uc_ref, gt_ref, x_ref, wa_ref, wb_ref, wc_ref, wo_ref, gf_ref, y_ref):
    d = x_ref.shape[1]
    ya = _dot(ua_ref[...].astype(BF16), wa_ref[...])
    yb = _dot(ub_ref[...].astype(BF16), wb_ref[...])
    yc = _dot(uc_ref[...].astype(BF16), wc_ref[...])
    mix = (jax.nn.sigmoid(gt_ref[:, 0:d]) * ya + jax.nn.sigmoid(gt_ref[:, d:2 * d]) * yb
           + jax.nn.sigmoid(gt_ref[:, 2 * d:3 * d]) * yc)
    xo = x_ref[...] + _dot(mix.astype(BF16), wo_ref[...])
    y = xo * lax.rsqrt(jnp.mean(xo * xo, axis=-1, keepdims=True) + EPS)
    y_ref[...] = y * gf_ref[...]


def _final(ua, ub, uc, z, x, wa, wb, wc, wo, g_final, *, tm):
    m, d = x.shape
    const = lambda i: (0, 0)
    resident = dict(pipeline_mode=pl.Buffered(1))
    return pl.pallas_call(
        _final_kernel,
        grid=(m // tm,),
        in_specs=[pl.BlockSpec((tm, ua.shape[1]), lambda i: (i, 0)),
                  pl.BlockSpec((tm, ub.shape[1]), lambda i: (i, 0)),
                  pl.BlockSpec((tm, uc.shape[1]), lambda i: (i, 0)),
                  pl.BlockSpec((tm, 3 * d), lambda i: (i, C_GT // (3 * d))),
                  pl.BlockSpec((tm, d), lambda i: (i, 0)),
                  pl.BlockSpec(wa.shape, const, **resident),
                  pl.BlockSpec(wb.shape, const, **resident),
                  pl.BlockSpec(wc.shape, const, **resident),
                  pl.BlockSpec(wo.shape, const, **resident),
                  pl.BlockSpec((1, d), const)],
        out_specs=pl.BlockSpec((tm, d), lambda i: (i, 0)),
        out_shape=jax.ShapeDtypeStruct((m, d), F32),
        compiler_params=_cparams(("arbitrary",)),
        name="final",
    )(ua, ub, uc, z, x, wa, wb, wc, wo, g_final.reshape(1, d))


def _heads_rows(row, col0):
    return jnp.concatenate([row[:, col0 + h * HD:col0 + (h + 1) * HD] for h in range(N_HEADS)], axis=0)


def _decode_attention(q4, kk, vv, k_new=None, v_new=None):
    s = jnp.sum(kk * q4[None], axis=-1, keepdims=True)
    m = jnp.max(s, axis=0)
    if k_new is not None:
        s_new = jnp.sum(k_new * q4, axis=-1, keepdims=True)
        m = jnp.maximum(m, s_new)
    p = jnp.exp(s - m[None])
    den = jnp.sum(p, axis=0)
    acc = jnp.sum(p * vv, axis=0)
    if k_new is not None:
        p_new = jnp.exp(s_new - m)
        den = den + p_new
        acc = acc + p_new * v_new
    return acc / den, m + jnp.log(den)


def _sample_kernel(*refs):
    (z_ref, za_ref, wa_ref, ba_ref, gg_ref, cos_ref, sin_ref, st_ref, cg0, cg1, cg2, cm_ref,
     ua_ref, ub_ref, uc_ref, sout_ref, nr0, nr1, nr2) = refs
    cg, nr = (cg0, cg1, cg2), (nr0, nr1, nr2)
    b = pl.program_id(0)
    zrow = z_ref[pl.ds(b, 1), :]

    ga8 = jnp.broadcast_to(za_ref[pl.ds(b, 1), :], (8, LANES))
    xa = _dot(ga8.astype(BF16), wa_ref[...])[0:1, :] + ba_ref[...]
    a_row = jnp.exp(_log_sigmoid(xa) / GLA_TAU)
    eye = (lax.broadcasted_iota(jnp.int32, (HD, HD), 0) == lax.broadcasted_iota(jnp.int32, (HD, HD), 1))

    def col(row_vec):
        return jnp.sum(jnp.where(eye, jnp.broadcast_to(row_vec, (HD, HD)), 0.0), axis=1, keepdims=True)

    for h in range(N_HEADS):
        q = zrow[:, C_GQ + h * HD:C_GQ + (h + 1) * HD] * (HD ** -0.5)
        k = zrow[:, C_GK + h * HD:C_GK + (h + 1) * HD]
        v = zrow[:, C_GV + h * GLA_DV:C_GV + (h + 1) * GLA_DV]
        s_new = col(a_row[:, h * HD:(h + 1) * HD]) * st_ref[h] + col(k) * v
        sout_ref[h] = s_new
        o = jnp.sum(col(q) * s_new, axis=0, keepdims=True)
        y = o * lax.rsqrt(jnp.mean(o * o, axis=-1, keepdims=True) + EPS)
        y = y * gg_ref[:, h * GLA_DV:(h + 1) * GLA_DV]
        gate = zrow[:, C_GR + h * GLA_DV:C_GR + (h + 1) * GLA_DV]
        ua_ref[:, h * GLA_DV:(h + 1) * GLA_DV] = y * _silu(gate)

    cos2, sin2 = cos_ref[...], sin_ref[...]
    outs, lses = [], []
    w_grp = N_HEADS * HD
    for g in range(3):
        q4 = _rope(_heads_rows(zrow, C_SQ + g * w_grp), cos2, sin2) * (HD ** -0.5)
        k4 = _rope(_heads_rows(zrow, C_SK + g * w_grp), cos2, sin2)
        v4 = _heads_rows(zrow, C_SV + g * w_grp)
        nr[g][0:N_HEADS, :] = k4
        nr[g][N_HEADS:KV_ROWS, :] = v4
        o, lse = _decode_attention(q4, cg[g][:, 0:N_HEADS, :], cg[g][:, N_HEADS:KV_ROWS, :], k4, v4)
        outs.append(o)
        lses.append(lse)
    mx = functools.reduce(jnp.maximum, lses)
    es = [jnp.exp(l - mx) for l in lses]
    tot = es[0] + es[1] + es[2]
    ob = (es[0] / tot) * outs[0] + (es[1] / tot) * outs[1] + (es[2] / tot) * outs[2]
    for h in range(N_HEADS):
        gate = zrow[:, C_SR + h * HD:C_SR + (h + 1) * HD]
        ub_ref[:, h * HD:(h + 1) * HD] = ob[h:h + 1, :] * _silu(gate)

    qm = _heads_rows(zrow, C_MQ) * (HD ** -0.5)
    oc, _ = _decode_attention(qm, cm_ref[:, 0:N_HEADS, :], cm_ref[:, N_HEADS:KV_ROWS, :])
    for h in range(N_HEADS):
        gate = zrow[:, C_MR + h * HD:C_MR + (h + 1) * HD]
        uc_ref[:, h * HD:(h + 1) * HD] = oc[h:h + 1, :] * _silu(gate)


def _sample_mixers(z, za, wa_pad, b_alpha, g_gla_out, cos2, sin2, state, caches, cache_mem):
    db = z.shape[0]
    kw, vw, w = N_HEADS * HD, N_HEADS * GLA_DV, N_HEADS * HD
    const2 = lambda b: (0, 0)
    gathered, gather_specs = [], []
    for c, (win, dil) in zip(caches, SWA_PATTERNS):
        assert c.shape == (1, db, win, 2, N_HEADS, HD) and PAST_LEN >= win
        n_keys = win // dil
        gathered.append(c.reshape(db, n_keys, dil * KV_ROWS, HD))
        gather_specs.append(pl.BlockSpec((None, n_keys, KV_ROWS, HD), lambda b: (b, 0, 0, 0)))
    row3 = lambda n: pl.BlockSpec((None, 1, n), lambda b: (b, 0, 0))
    new_spec = pl.BlockSpec((None, KV_ROWS, HD), lambda b: (b, 0, 0))
    res = pl.pallas_call(
        _sample_kernel,
        grid=(db,),
        in_specs=[pl.BlockSpec(z.shape, const2), pl.BlockSpec(za.shape, const2),
                  pl.BlockSpec(wa_pad.shape, const2), pl.BlockSpec((1, kw), const2),
                  pl.BlockSpec((1, vw), const2), pl.BlockSpec((1, HD), const2), pl.BlockSpec((1, HD), const2),
                  pl.BlockSpec((None, None, N_HEADS, HD, GLA_DV), lambda b: (0, b, 0, 0, 0))]
                 + gather_specs
                 + [pl.BlockSpec((None, N_MEM, KV_ROWS, HD), lambda b: (b, 0, 0, 0))],
        out_specs=[row3(vw), row3(w), row3(w),
                   pl.BlockSpec((None, None, N_HEADS, HD, GLA_DV), lambda b: (0, b, 0, 0, 0))] + [new_spec] * 3,
        out_shape=[jax.ShapeDtypeStruct((db, 1, vw), F32), jax.ShapeDtypeStruct((db, 1, w), F32),
                   jax.ShapeDtypeStruct((db, 1, w), F32), jax.ShapeDtypeStruct(state.shape, F32)]
                  + [jax.ShapeDtypeStruct((db, KV_ROWS, HD), F32)] * 3,
        compiler_params=_cparams(("arbitrary",)),
        name="sample_mixers",
    )(z, za, wa_pad, b_alpha.reshape(1, kw), g_gla_out.reshape(1, vw), cos2, sin2, state,
      *gathered, cache_mem.reshape(db, N_MEM, KV_ROWS, HD))
    ua, ub, uc, s_out = res[:4]
    return ua.reshape(db, vw), ub.reshape(db, w), uc.reshape(db, w), s_out, res[4:]


def _prep_w_in_kernel(wt_ref, wm_ref, wg_ref, buf, ga_buf, sem, ga_sem, *, c_ga, c_gt, rows):
    i = pl.program_id(0)
    n_gt = (wt_ref.shape[0] - c_gt) // rows
    n_lo = c_ga // rows

    def fetch(ii, slot):
        src = jnp.where(ii < n_gt, c_gt + ii * rows,
                        jnp.where(ii < n_gt + n_lo, (ii - n_gt) * rows, c_ga + GLA_RANK + (ii - n_gt - n_lo) * rows))
        return pltpu.make_async_copy(wt_ref.at[pl.ds(pl.multiple_of(src, GLA_RANK), rows)], buf.at[slot], sem.at[slot])

    ga_copy = pltpu.make_async_copy(wt_ref.at[pl.ds(c_ga, GLA_RANK)], ga_buf, ga_sem)

    @pl.when(i == 0)
    def _():
        fetch(0, 0).start()
        ga_copy.start()

    @pl.when(i + 1 < pl.num_programs(0))
    def _():
        fetch(i + 1, lax.rem(i + 1, 2)).start()

    @pl.when(i == 0)
    def _():
        ga_copy.wait()
        wg_ref[0:GLA_RANK, :] = ga_buf[...].astype(BF16)
        wg_ref[GLA_RANK:, :] = jnp.zeros((LANES - GLA_RANK, wg_ref.shape[1]), BF16)

    slot = lax.rem(i, 2)
    fetch(i, slot).wait()
    wm_ref[...] = buf[slot].astype(BF16)


def _prep_w_in(wt, d, *, rows):
    n_all, k = wt.shape
    c_ga = 2 * N_HEADS * HD + 2 * N_HEADS * GLA_DV
    c_gt = n_all - 3 * d
    assert n_all - GLA_RANK == N_MAIN and k == d
    assert (n_all - c_gt) % rows == 0 and c_ga % rows == 0 and (c_gt - c_ga - GLA_RANK) % rows == 0
    return pl.pallas_call(
        functools.partial(_prep_w_in_kernel, c_ga=c_ga, c_gt=c_gt, rows=rows),
        grid=(N_MAIN // rows,),
        in_specs=[pl.BlockSpec(memory_space=pl.ANY)],
        out_specs=[pl.BlockSpec((rows, d), lambda i: (i, 0)), pl.BlockSpec((LANES, d), lambda i: (0, 0))],
        out_shape=[jax.ShapeDtypeStruct((N_MAIN, d), BF16), jax.ShapeDtypeStruct((LANES, d), BF16)],
        scratch_shapes=[pltpu.VMEM((2, rows, d), F32), pltpu.VMEM((GLA_RANK, d), F32),
                        pltpu.SemaphoreType.DMA((2,)), pltpu.SemaphoreType.DMA(())],
        compiler_params=_cparams(("arbitrary",)),
        name="prep_w_in",
    )(wt)


def kernel(x_prompt, x_sample, mem_prompt, state_gla, cache_swa_w128, cache_swa_w512, cache_swa_w2048, cache_mem_kv, g_norm, w_in, w_alpha2, b_alpha, g_gla_out, g_mem, w_mem_kv, w_proj_a, w_proj_b, w_proj_c, w_out, g_final):
    batch, seq, d = x_prompt.shape
    db, dec_seq, _ = x_sample.shape
    assert g_norm.shape[0] == 1 and dec_seq == 1
    w_grp = N_HEADS * HD

    w_main, w_ga = _prep_w_in(jnp.swapaxes(w_in[0], 0, 1), d, rows=512)
    wa_pad = jnp.pad(w_alpha2[0], ((0, LANES - GLA_RANK), (0, 0))).astype(BF16)
    wpa, wpb, wpc, wo = (w[0].astype(BF16) for w in (w_proj_a, w_proj_b, w_proj_c, w_out))

    xp = x_prompt.reshape(batch * seq, d)
    xs = x_sample.reshape(db, d)

    zs, zas = _norm_matmul(xs, g_norm[0], w_main, w_ga, w_rows_out=True, tm=db, tn=1024)
    cos_s, sin_s = _rope_tables(jnp.full((1,), PAST_LEN, jnp.int32))
    caches = (cache_swa_w128, cache_swa_w512, cache_swa_w2048)
    uas, ubs, ucs, gla_s, new_rows = _sample_mixers(
        zs, zas, wa_pad, b_alpha[0], g_gla_out[0], cos_s, sin_s, state_gla, caches, cache_mem_kv)
    y_sample = _final(uas, ubs, ucs, zs, xs, wpa, wpb, wpc, wo, g_final, tm=db).reshape(db, 1, d)

    flat_caches = [c.reshape(db, w * KV_ROWS, HD) for c, (w, _) in zip(caches, SWA_PATTERNS)]
    z, za, *shifted = _norm_matmul(xp, g_norm[0], w_main, w_ga, (flat_caches, new_rows), w_rows_out=True,
                                   tm=1024, tn=1024)
    swa_s = [o.reshape(c.shape) for o, c in zip(shifted, caches)]
    mem_kv = _norm_matmul(mem_prompt.reshape(batch * N_MEM, d), g_mem[0], w_mem_kv[0].astype(BF16),
                          tm=batch * N_MEM, tn=512)
    ua, gla_p = _gla_prompt(z, za, wa_pad, b_alpha[0], g_gla_out[0], batch, seq, t_blk=512)
    cos_p, sin_p = _rope_tables(jnp.arange(seq, dtype=jnp.int32))
    swa_res = _swa_prompt(z, cos_p, sin_p, batch, seq)
    ub, kbufs, vbufs = swa_res[0], swa_res[1:4], swa_res[4:7]
    uc = _memattn_prompt(z, mem_kv, batch, seq, t_blk=512)
    y_prompt = _final(ua, ub, uc, z, xp, wpa, wpb, wpc, wo, g_final, tm=256).reshape(batch, seq, d)

    swa_p = [jnp.stack([k.reshape(batch, w, N_HEADS, HD), v.reshape(batch, w, N_HEADS, HD)], axis=2)[None]
             for k, v, (w, _) in zip(kbufs, vbufs, SWA_PATTERNS)]
    mem_kv_prompt = mem_kv.reshape(1, batch, N_MEM, 2, N_HEADS, HD)
    return (y_prompt, y_sample, gla_p[None], swa_p[0], swa_p[1], swa_p[2], mem_kv_prompt,
            gla_s, swa_s[0], swa_s[1], swa_s[2])
```

```python
import functools

import numpy as np
import jax
import jax.numpy as jnp
from jax import lax
from jax.experimental import pallas as pl
from jax.experimental.pallas import tpu as pltpu

F32 = jnp.float32
BF16 = jnp.bfloat16

EPS = 1e-6
ROPE_THETA = 10000.0
NEG = -1e30
PAST_LEN = 16384

HD = 128
N_HEADS = 4
GLA_DV = 256
GLA_RANK = 16
GLA_TAU = 16.0
GLA_CHUNK = 64
GLA_SUB = 16
SWA_PATTERNS = ((128, 1), (512, 4), (2048, 16))
SWA_BLK = 128
SWA_STEP = 2048
SWA_UNROLL = 4
N_MEM = 256

LANES = 128
VMEM_LIMIT = 56 * 1024 * 1024

C_GT = 0
C_GQ, C_GK, C_GV, C_GR = 6144, 6656, 7168, 8192
C_SQ, C_SK, C_SV, C_SR = 9216, 10752, 12288, 13824
C_MQ, C_MR = 14336, 14848
N_MAIN = 15360


def _cparams(sem):
    return pltpu.CompilerParams(dimension_semantics=sem, vmem_limit_bytes=VMEM_LIMIT)


def _dot(a, b):
    return jnp.dot(a, b, preferred_element_type=F32)


def _dot_nt(a, b):
    return lax.dot_general(a, b, (((1,), (1,)), ((), ())), preferred_element_type=F32)


def _dot_tn(a, b):
    return lax.dot_general(a, b, (((0,), (0,)), ((), ())), preferred_element_type=F32)


def _silu(x):
    return x * jax.nn.sigmoid(x)


def _log_sigmoid(x):
    return jnp.minimum(x, 0.0) - jnp.log1p(jnp.exp(-jnp.abs(x)))


KV_ROWS = 2 * N_HEADS
SHIFT_ROWS = tuple((w - 1) * KV_ROWS for w, _ in SWA_PATTERNS)
SHIFT_OFFS = tuple(sum(SHIFT_ROWS[:g]) for g in range(len(SWA_PATTERNS)))


def _cache_shift_step(step, n_steps, ca, nr, co, stage, in_sem, out_sem, row_sem):
    n_grp = len(ca)
    n_batch = ca[0].shape[0]
    assert n_steps >= n_batch
    slot = lax.rem(step, 2)

    def in_copy(g, bb, sl):
        return pltpu.make_async_copy(ca[g].at[bb, pl.ds(KV_ROWS, SHIFT_ROWS[g])],
                                     stage.at[sl, pl.ds(SHIFT_OFFS[g], SHIFT_ROWS[g])], in_sem.at[sl, g])

    def out_copy(g, bb, sl):
        return pltpu.make_async_copy(stage.at[sl, pl.ds(SHIFT_OFFS[g], SHIFT_ROWS[g])],
                                     co[g].at[bb, pl.ds(0, SHIFT_ROWS[g])], out_sem.at[sl, g])

    def row_copy(g, bb, sl):
        return pltpu.make_async_copy(nr[g].at[bb], co[g].at[bb, pl.ds(SHIFT_ROWS[g], KV_ROWS)], row_sem.at[sl, g])

    def finish_writes(bb, sl):
        for g in range(n_grp):
            out_copy(g, bb, sl).wait()
            row_copy(g, bb, sl).wait()

    @pl.when(step == 0)
    def _():
        for g in range(n_grp):
            in_copy(g, 0, 0).start()

    @pl.when((step >= 1) & (step <= n_batch))
    def _():
        finish_writes(step - 1, 1 - slot)

    @pl.when(step + 1 < n_batch)
    def _():
        for g in range(n_grp):
            in_copy(g, step + 1, 1 - slot).start()

    @pl.when(step < n_batch)
    def _():
        for g in range(n_grp):
            in_copy(g, step, slot).wait()
            out_copy(g, step, slot).start()
            row_copy(g, step, slot).start()

    if n_steps == n_batch:
        @pl.when(step == n_steps - 1)
        def _():
            finish_writes(step, slot)


def _norm_matmul_kernel(*refs, has_extra, row_chunk, w_rows_out):
    mm = _dot_nt if w_rows_out else _dot
    if has_extra:
        x_ref, g_ref, w_ref, wx_ref, o_ref, ox_ref, h_ref = refs
    else:
        x_ref, g_ref, w_ref, o_ref, h_ref = refs
    tm = x_ref.shape[0]

    @pl.when(pl.program_id(1) == 0)
    def _():
        def body(c, carry):
            r0 = pl.multiple_of(c * row_chunk, row_chunk)
            x = x_ref[pl.ds(r0, row_chunk), :]
            y = x * lax.rsqrt(jnp.mean(x * x, axis=-1, keepdims=True) + EPS)
            h_ref[pl.ds(r0, row_chunk), :] = (y * g_ref[...]).astype(BF16)
            return carry
        lax.fori_loop(0, tm // row_chunk, body, 0)
        if has_extra:
            ox_ref[...] = mm(h_ref[...], wx_ref[...])

    o_ref[...] = mm(h_ref[...], w_ref[...]).astype(o_ref.dtype)


def _norm_matmul(x, g, w, wx=None, *, w_rows_out=False, out_dtype=F32, tm, tn):
    m, d = x.shape
    n = w.shape[0] if w_rows_out else w.shape[1]
    assert m % tm == 0 and n % tn == 0
    has_extra = wx is not None
    row_chunk = min(tm, 128)
    w_spec = pl.BlockSpec((tn, d), lambda i, j: (j, 0)) if w_rows_out else pl.BlockSpec((d, tn), lambda i, j: (0, j))
    in_specs = [pl.BlockSpec((tm, d), lambda i, j: (i, 0)),
                pl.BlockSpec((1, d), lambda i, j: (0, 0)),
                w_spec]
    out_specs = [pl.BlockSpec((tm, tn), lambda i, j: (i, j))]
    out_shape = [jax.ShapeDtypeStruct((m, n), out_dtype)]
    args = [x, g.reshape(1, d), w]
    if has_extra:
        nx = wx.shape[0] if w_rows_out else wx.shape[1]
        in_specs.append(pl.BlockSpec(wx.shape, lambda i, j: (0, 0)))
        out_specs.append(pl.BlockSpec((tm, nx), lambda i, j: (i, 0)))
        out_shape.append(jax.ShapeDtypeStruct((m, nx), F32))
        args.append(wx)
    res = pl.pallas_call(
        functools.partial(_norm_matmul_kernel, has_extra=has_extra, row_chunk=row_chunk, w_rows_out=w_rows_out),
        grid=(m // tm, n // tn),
        in_specs=in_specs, out_specs=out_specs, out_shape=out_shape,
        scratch_shapes=[pltpu.VMEM((tm, d), BF16)],
        compiler_params=_cparams(("arbitrary", "arbitrary")),
        name="norm_matmul",
    )(*args)
    return res if has_extra else res[0]


def _diag_select_matrix():
    rows = np.arange(GLA_SUB * HD)[:, None] // HD
    cols = np.arange(LANES)[None, :] % GLA_SUB
    return jnp.asarray((rows == cols).astype(np.float32), dtype=BF16)


def _gla_kernel(*refs, n_shift, n_steps):
    q_ref, k_ref, v_ref, gr_ref, ga_ref, wa_ref, ba_ref, gg_ref, em_ref = refs[:9]
    n_in = 9 + 2 * n_shift
    u_ref, sout_ref = refs[n_in:n_in + 2]
    st_ref = refs[n_in + 2 + n_shift]
    t_blk = pl.program_id(1)
    n_chunks = q_ref.shape[0] // GLA_CHUNK
    C, SUB, n_sub = GLA_CHUNK, GLA_SUB, GLA_CHUNK // GLA_SUB

    if n_shift:
        stage, in_sem, out_sem, row_sem = refs[n_in + 3 + n_shift:]
        _cache_shift_step(pl.program_id(0) * pl.num_programs(1) + t_blk, n_steps, refs[9:9 + n_shift], refs[9 + n_shift:n_in], refs[n_in + 2:n_in + 2 + n_shift],
                          stage, in_sem, out_sem, row_sem)

    @pl.when(t_blk == 0)
    def _():
        st_ref[...] = jnp.zeros_like(st_ref)

    row = lax.broadcasted_iota(jnp.int32, (C, C), 0)
    col = lax.broadcasted_iota(jnp.int32, (C, C), 1)
    tri = (col <= row).astype(F32)
    row_c = lax.broadcasted_iota(jnp.int32, (C, HD), 0)
    sub_row = lax.broadcasted_iota(jnp.int32, (SUB, HD), 0)
    lane_c = lax.broadcasted_iota(jnp.int32, (SUB, C), 1)

    def chunk(c, carry):
        r0 = pl.multiple_of(c * C, C)
        rows = pl.ds(r0, C)
        xa = _dot(ga_ref[rows, :].astype(BF16), wa_ref[...]) + ba_ref[...]
        la = _log_sigmoid(xa) / GLA_TAU
        b_all = jnp.dot(tri, la, preferred_element_type=F32, precision=lax.Precision.HIGHEST)

        p_rows = []
        a_off = []
        per_head = []
        for h in range(N_HEADS):
            q = q_ref[rows, h * HD:(h + 1) * HD].astype(F32) * (HD ** -0.5)
            k = k_ref[rows, h * HD:(h + 1) * HD].astype(F32)
            b = b_all[:, h * HD:(h + 1) * HD]
            per_head.append((q, k, b))
            for i in range(n_sub):
                sl = slice(i * SUB, (i + 1) * SUB)
                q_i, k_i, b_i = q[sl], k[sl], b[sl]
                if i == 0:
                    a_off.append(jnp.zeros((SUB, C), F32))
                else:
                    b_ref_row = b[i * SUB - 1:i * SUB, :]
                    qs = q_i * jnp.exp(b_i - b_ref_row)
                    ks = k * jnp.exp(jnp.where(row_c < i * SUB, b_ref_row - b, -jnp.inf))
                    a_off.append(_dot_nt(qs.astype(BF16), ks.astype(BF16)))
                slabs = []
                for s in range(SUB):
                    e = jnp.where(sub_row >= s, b_i - b_i[s:s + 1, :], -jnp.inf)
                    slabs.append((q_i * k_i[s:s + 1, :]) * jnp.exp(e))
                p_rows.append(jnp.concatenate(slabs, axis=1).astype(BF16))
        r_all = _dot(jnp.concatenate(p_rows, axis=0), em_ref[...])

        for h in range(N_HEADS):
            q, k, b = per_head[h]
            v = v_ref[rows, h * GLA_DV:(h + 1) * GLA_DV]
            a_rows = []
            for i in range(n_sub):
                idx = h * n_sub + i
                r_i = r_all[idx * SUB:(idx + 1) * SUB, :C]
                in_blk = (lane_c >= i * SUB) & (lane_c < (i + 1) * SUB)
                a_rows.append(a_off[idx] + jnp.where(in_blk, r_i, 0.0))
            a = jnp.concatenate(a_rows, axis=0)
            st = st_ref[h]
            o = _dot(a.astype(BF16), v.astype(BF16)) + _dot_nt((q * jnp.exp(b)).astype(BF16), st.astype(BF16))
            b_end = b[C - 1:C, :]
            kd = k * jnp.exp(b_end - b)
            st_ref[h] = st * jnp.exp(b_end) + _dot_tn(v.astype(BF16), kd.astype(BF16))
            y = o * lax.rsqrt(jnp.mean(o * o, axis=-1, keepdims=True) + EPS)
            y = y * gg_ref[:, h * GLA_DV:(h + 1) * GLA_DV]
            gate = gr_ref[rows, h * GLA_DV:(h + 1) * GLA_DV].astype(F32)
            u_ref[rows, h * GLA_DV:(h + 1) * GLA_DV] = (y * _silu(gate)).astype(u_ref.dtype)
        return carry

    lax.fori_loop(0, n_chunks, chunk, 0)

    @pl.when(t_blk == pl.num_programs(1) - 1)
    def _():
        for h in range(N_HEADS):
            sout_ref[h] = st_ref[h].T


def _gla_prompt(z, za, wa_pad, b_alpha, g_gla_out, batch, seq, shift=None, *, t_blk):
    nt = seq // t_blk
    assert seq % t_blk == 0 and t_blk % GLA_CHUNK == 0
    kw = N_HEADS * HD
    vw = N_HEADS * GLA_DV

    def zspec(width, col0):
        cb = col0 // width
        return pl.BlockSpec((t_blk, width), lambda b, t: (b * nt + t, cb))

    in_specs = [zspec(kw, C_GQ), zspec(kw, C_GK), zspec(vw, C_GV), zspec(vw, C_GR),
                pl.BlockSpec((t_blk, LANES), lambda b, t: (b * nt + t, 0)),
                pl.BlockSpec((LANES, kw), lambda b, t: (0, 0)),
                pl.BlockSpec((1, kw), lambda b, t: (0, 0)),
                pl.BlockSpec((1, vw), lambda b, t: (0, 0)),
                pl.BlockSpec((GLA_SUB * HD, LANES), lambda b, t: (0, 0))]
    out_specs = [pl.BlockSpec((t_blk, vw), lambda b, t: (b * nt + t, 0)),
                 pl.BlockSpec((None, N_HEADS, HD, GLA_DV), lambda b, t: (b, 0, 0, 0))]
    out_shape = [jax.ShapeDtypeStruct((batch * seq, vw), BF16),
                 jax.ShapeDtypeStruct((batch, N_HEADS, HD, GLA_DV), F32)]
    scratch = [pltpu.VMEM((N_HEADS, GLA_DV, HD), F32)]
    args = [z, z, z, z, za, wa_pad, b_alpha.reshape(1, kw), g_gla_out.reshape(1, vw), _diag_select_matrix()]
    n_shift = 0
    if shift is not None:
        caches, new_rows = shift
        n_shift = len(caches)
        any_spec = pl.BlockSpec(memory_space=pl.ANY)
        in_specs += [any_spec] * n_shift + [pl.BlockSpec(r.shape, lambda b, t: (0, 0, 0)) for r in new_rows]
        out_specs += [any_spec] * n_shift
        out_shape += [jax.ShapeDtypeStruct(c.shape, c.dtype) for c in caches]
        args += list(caches) + list(new_rows)
        scratch += [pltpu.VMEM((2, sum(SHIFT_ROWS), HD), F32)] + [pltpu.SemaphoreType.DMA((2, n_shift))] * 3
    return pl.pallas_call(
        functools.partial(_gla_kernel, n_shift=n_shift, n_steps=batch * nt),
        grid=(batch, nt),
        in_specs=in_specs, out_specs=out_specs, out_shape=out_shape, scratch_shapes=scratch,
        compiler_params=_cparams(("arbitrary", "arbitrary")),
        name="gla",
    )(*args)


def _rope_tables(pos):
    half = HD // 2
    inv = ROPE_THETA ** (-jnp.arange(half, dtype=F32) / half)
    ang = pos.astype(F32)[:, None] * inv[None, :]
    cos, sin = jnp.cos(ang), jnp.sin(ang)
    return jnp.concatenate([cos, cos], axis=1), jnp.concatenate([-sin, sin], axis=1)


def _rope(x, cos2, sin2):
    return x * cos2 + pltpu.roll(x, HD // 2, axis=1) * sin2


def _swa_kernel(*refs):
    (q0, q1, q2, k0, k1, k2, v0, v1, v2, sr_ref, cos_ref, sin_ref,
     u_ref, kb0, kb1, kb2, vb0, vb1, vb2,
     q_s, k_s, v_s, o_s, l_s) = refs
    q_in, k_in, v_in = (q0, q1, q2), (k0, k1, k2), (v0, v1, v2)
    kb, vb = (kb0, kb1, kb2), (vb0, vb1, vb2)
    i = pl.program_id(2)
    T = SWA_STEP
    n_grp = len(SWA_PATTERNS)

    @pl.when(i == 0)
    def _():
        for g, (_, dil) in enumerate(SWA_PATTERNS):
            unit = dil * SWA_BLK
            k_s[g, T - unit:T, :] = jnp.zeros((unit, HD), F32)
            v_s[g, T - unit:T, :] = jnp.zeros((unit, HD), F32)

    @pl.when(i > 0)
    def _():
        for g, (_, dil) in enumerate(SWA_PATTERNS):
            unit = dil * SWA_BLK
            k_s[g, T - unit:T, :] = k_s[g, 2 * T - unit:2 * T, :]
            v_s[g, T - unit:T, :] = v_s[g, 2 * T - unit:2 * T, :]

    cos2, sin2 = cos_ref[...], sin_ref[...]
    for g in range(n_grp):
        q_s[g] = _rope(q_in[g][...].astype(F32), cos2, sin2) * (HD ** -0.5)
        k_s[g, T:2 * T, :] = _rope(k_in[g][...].astype(F32), cos2, sin2)
        v_s[g, T:2 * T, :] = v_in[g][...].astype(F32)

    qi = lax.broadcasted_iota(jnp.int32, (SWA_BLK, 2 * SWA_BLK), 0)
    ki = lax.broadcasted_iota(jnp.int32, (SWA_BLK, 2 * SWA_BLK), 1)
    delta = SWA_BLK + qi - ki

    def block(g, n):
        win, dil = SWA_PATTERNS[g]
        unit = dil * SWA_BLK
        u = lax.shift_right_logical(n, dil.bit_length() - 1)
        r = lax.bitwise_and(n, dil - 1)

        def rows(start, size):
            return pl.ds(start, size) if dil == 1 else pl.ds(start, size, stride=dil)

        q = q_s[g, rows(u * unit + r, SWA_BLK), :]
        kk = k_s[g, rows(T + (u - 1) * unit + r, 2 * SWA_BLK), :]
        vv = v_s[g, rows(T + (u - 1) * unit + r, 2 * SWA_BLK), :]
        s = _dot_nt(q.astype(BF16), kk.astype(BF16))
        ki_min = jnp.where((i * (T // unit) + u) == 0, SWA_BLK, 0)
        s = jnp.where((delta >= 0) & (delta <= win // dil) & (ki >= ki_min), s, NEG)
        m = jnp.max(s, axis=-1, keepdims=True)
        p = jnp.exp(s - m)
        den = jnp.sum(p, axis=-1, keepdims=True)
        o = _dot(p.astype(BF16), vv.astype(BF16)) / den
        lse = m + jnp.log(den)
        o_s[g, rows(u * unit + r, SWA_BLK), :] = o
        l_s[g, rows(u * unit + r, SWA_BLK), :] = jnp.broadcast_to(lse, (SWA_BLK, HD))

    def blocks(it, carry):
        for g in range(n_grp):
            for j in range(SWA_UNROLL):
                block(g, it * SWA_UNROLL + j)
        return carry

    lax.fori_loop(0, T // SWA_BLK // SWA_UNROLL, blocks, 0)

    rc = 256

    def merge(c, carry):
        r0 = pl.multiple_of(c * rc, rc)
        rr = pl.ds(r0, rc)
        ls = [l_s[g, rr, :] for g in range(n_grp)]
        mx = functools.reduce(jnp.maximum, ls)
        es = [jnp.exp(l - mx) for l in ls]
        tot = functools.reduce(lambda a, b: a + b, es)
        ob = functools.reduce(lambda a, b: a + b, [(es[g] / tot) * o_s[g, rr, :] for g in range(n_grp)])
        u_ref[rr, :] = (ob * _silu(sr_ref[rr, :].astype(F32))).astype(u_ref.dtype)
        return carry

    lax.fori_loop(0, T // rc, merge, 0)

    @pl.when(i == pl.num_programs(2) - 1)
    def _():
        for g, (win, _) in enumerate(SWA_PATTERNS):
            kb[g][...] = k_s[g, 2 * T - win:2 * T, :]
            vb[g][...] = v_s[g, 2 * T - win:2 * T, :]


def _swa_prompt(z, cos2, sin2, batch, seq):
    T = SWA_STEP
    assert seq % T == 0 and all(w <= T for w, _ in SWA_PATTERNS)
    nt = seq // T
    w_grp = N_HEADS * HD

    def zspec(col0, g):
        cb0 = (col0 + g * w_grp) // HD
        return pl.BlockSpec((T, HD), lambda b, j, i: (b * nt + i, cb0 + j))

    in_specs = ([zspec(C_SQ, g) for g in range(3)] + [zspec(C_SK, g) for g in range(3)]
                + [zspec(C_SV, g) for g in range(3)] + [zspec(C_SR, 0)]
                + [pl.BlockSpec((T, HD), lambda b, j, i: (i, 0))] * 2)
    buf_specs = [pl.BlockSpec((None, w, HD), lambda b, j, i: (b, 0, j)) for w, _ in SWA_PATTERNS]
    buf_shapes = [jax.ShapeDtypeStruct((batch, w, w_grp), F32) for w, _ in SWA_PATTERNS]
    return pl.pallas_call(
        _swa_kernel,
        grid=(batch, N_HEADS, nt),
        in_specs=in_specs,
        out_specs=[pl.BlockSpec((T, HD), lambda b, j, i: (b * nt + i, j))] + buf_specs + buf_specs,
        out_shape=[jax.ShapeDtypeStruct((batch * seq, w_grp), BF16)] + buf_shapes + buf_shapes,
        scratch_shapes=[pltpu.VMEM((3, T, HD), F32), pltpu.VMEM((3, 2 * T, HD), F32),
                        pltpu.VMEM((3, 2 * T, HD), F32), pltpu.VMEM((3, T, HD), F32),
                        pltpu.VMEM((3, T, HD), F32)],
        compiler_params=_cparams(("arbitrary", "arbitrary", "arbitrary")),
        name="swa",
    )(*([z] * 10), cos2, sin2)


def _memattn_kernel(q_ref, mr_ref, kv_ref, u_ref):
    w = N_HEADS * HD
    for h in range(N_HEADS):
        cs = slice(h * HD, (h + 1) * HD)
        q = q_ref[:, cs].astype(F32) * (HD ** -0.5)
        k = kv_ref[:, cs]
        v = kv_ref[:, w + h * HD:w + (h + 1) * HD]
        s = _dot_nt(q.astype(BF16), k.astype(BF16))
        e = jnp.exp(s - jnp.max(s, axis=-1, keepdims=True))
        p = e / jnp.sum(e, axis=-1, keepdims=True)
        o = _dot(p.astype(BF16), v.astype(BF16))
        u_ref[:, cs] = (o * _silu(mr_ref[:, cs].astype(F32))).astype(u_ref.dtype)


def _memattn_prompt(z, mem_kv, batch, seq, *, t_blk):
    nt = seq // t_blk
    w = N_HEADS * HD
    return pl.pallas_call(
        _memattn_kernel,
        grid=(batch, nt),
        in_specs=[pl.BlockSpec((t_blk, w), lambda b, t: (b * nt + t, C_MQ // w)),
                  pl.BlockSpec((t_blk, w), lambda b, t: (b * nt + t, C_MR // w)),
                  pl.BlockSpec((N_MEM, 2 * w), lambda b, t: (b, 0))],
        out_specs=pl.BlockSpec((t_blk, w), lambda b, t: (b * nt + t, 0)),
        out_shape=jax.ShapeDtypeStruct((batch * seq, w), BF16),
        compiler_params=_cparams(("arbitrary", "arbitrary")),
        name="memattn",
    )(z, z, mem_kv)


def _final_kernel(ua_ref, ub_ref, uc_ref, gt_ref, x_ref, wa_ref, wb_ref, wc_ref, wo_ref, gf_ref, y_ref):
    d = x_ref.shape[1]
    ya = _dot(ua_ref[...].astype(BF16), wa_ref[...])
    yb = _dot(ub_ref[...].astype(BF16), wb_ref[...])
    yc = _dot(uc_ref[...].astype(BF16), wc_ref[...])
    mix = (jax.nn.sigmoid(gt_ref[:, 0:d].astype(F32)) * ya + jax.nn.sigmoid(gt_ref[:, d:2 * d].astype(F32)) * yb
           + jax.nn.sigmoid(gt_ref[:, 2 * d:3 * d].astype(F32)) * yc)
    xo = x_ref[...] + _dot(mix.astype(BF16), wo_ref[...])
    y = xo * lax.rsqrt(jnp.mean(xo * xo, axis=-1, keepdims=True) + EPS)
    y_ref[...] = y * gf_ref[...]


def _final(ua, ub, uc, z, x, wa, wb, wc, wo, g_final, *, tm):
    m, d = x.shape
    const = lambda i: (0, 0)
    resident = dict(pipeline_mode=pl.Buffered(1))
    return pl.pallas_call(
        _final_kernel,
        grid=(m // tm,),
        in_specs=[pl.BlockSpec((tm, ua.shape[1]), lambda i: (i, 0)),
                  pl.BlockSpec((tm, ub.shape[1]), lambda i: (i, 0)),
                  pl.BlockSpec((tm, uc.shape[1]), lambda i: (i, 0)),
                  pl.BlockSpec((tm, 3 * d), lambda i: (i, C_GT // (3 * d))),
                  pl.BlockSpec((tm, d), lambda i: (i, 0)),
                  pl.BlockSpec(wa.shape, const, **resident),
                  pl.BlockSpec(wb.shape, const, **resident),
                  pl.BlockSpec(wc.shape, const, **resident),
                  pl.BlockSpec(wo.shape, const, **resident),
                  pl.BlockSpec((1, d), const)],
        out_specs=pl.BlockSpec((tm, d), lambda i: (i, 0)),
        out_shape=jax.ShapeDtypeStruct((m, d), F32),
        compiler_params=_cparams(("arbitrary",)),
        name="final",
    )(ua, ub, uc, z, x, wa, wb, wc, wo, g_final.reshape(1, d))


def _heads_rows(row, col0):
    return jnp.concatenate([row[:, col0 + h * HD:col0 + (h + 1) * HD] for h in range(N_HEADS)], axis=0)


def _decode_attention(q4, kk, vv, k_new=None, v_new=None):
    s = jnp.sum(kk * q4[None], axis=-1, keepdims=True)
    m = jnp.max(s, axis=0)
    if k_new is not None:
        s_new = jnp.sum(k_new * q4, axis=-1, keepdims=True)
        m = jnp.maximum(m, s_new)
    p = jnp.exp(s - m[None])
    den = jnp.sum(p, axis=0)
    acc = jnp.sum(p * vv, axis=0)
    if k_new is not None:
        p_new = jnp.exp(s_new - m)
        den = den + p_new
        acc = acc + p_new * v_new
    return acc / den, m + jnp.log(den)


def _sample_kernel(*refs):
    (z_ref, za_ref, wa_ref, ba_ref, gg_ref, cos_ref, sin_ref, st_ref, cg0, cg1, cg2, cm_ref,
     ua_ref, ub_ref, uc_ref, sout_ref, nr0, nr1, nr2) = refs
    cg, nr = (cg0, cg1, cg2), (nr0, nr1, nr2)
    b = pl.program_id(0)
    zrow = z_ref[pl.ds(b, 1), :]

    ga8 = jnp.broadcast_to(za_ref[pl.ds(b, 1), :], (8, LANES))
    xa = _dot(ga8.astype(BF16), wa_ref[...])[0:1, :] + ba_ref[...]
    a_row = jnp.exp(_log_sigmoid(xa) / GLA_TAU)
    eye = (lax.broadcasted_iota(jnp.int32, (HD, HD), 0) == lax.broadcasted_iota(jnp.int32, (HD, HD), 1))

    def col(row_vec):
        return jnp.sum(jnp.where(eye, jnp.broadcast_to(row_vec, (HD, HD)), 0.0), axis=1, keepdims=True)

    for h in range(N_HEADS):
        q = zrow[:, C_GQ + h * HD:C_GQ + (h + 1) * HD] * (HD ** -0.5)
        k = zrow[:, C_GK + h * HD:C_GK + (h + 1) * HD]
        v = zrow[:, C_GV + h * GLA_DV:C_GV + (h + 1) * GLA_DV]
        s_new = col(a_row[:, h * HD:(h + 1) * HD]) * st_ref[h] + col(k) * v
        sout_ref[h] = s_new
        o = jnp.sum(col(q) * s_new, axis=0, keepdims=True)
        y = o * lax.rsqrt(jnp.mean(o * o, axis=-1, keepdims=True) + EPS)
        y = y * gg_ref[:, h * GLA_DV:(h + 1) * GLA_DV]
        gate = zrow[:, C_GR + h * GLA_DV:C_GR + (h + 1) * GLA_DV]
        ua_ref[:, h * GLA_DV:(h + 1) * GLA_DV] = y * _silu(gate)

    cos2, sin2 = cos_ref[...], sin_ref[...]
    outs, lses = [], []
    w_grp = N_HEADS * HD
    for g in range(3):
        q4 = _rope(_heads_rows(zrow, C_SQ + g * w_grp), cos2, sin2) * (HD ** -0.5)
        k4 = _rope(_heads_rows(zrow, C_SK + g * w_grp), cos2, sin2)
        v4 = _heads_rows(zrow, C_SV + g * w_grp)
        nr[g][0:N_HEADS, :] = k4
        nr[g][N_HEADS:KV_ROWS, :] = v4
        o, lse = _decode_attention(q4, cg[g][:, 0:N_HEADS, :], cg[g][:, N_HEADS:KV_ROWS, :], k4, v4)
        outs.append(o)
        lses.append(lse)
    mx = functools.reduce(jnp.maximum, lses)
    es = [jnp.exp(l - mx) for l in lses]
    tot = es[0] + es[1] + es[2]
    ob = (es[0] / tot) * outs[0] + (es[1] / tot) * outs[1] + (es[2] / tot) * outs[2]
    for h in range(N_HEADS):
        gate = zrow[:, C_SR + h * HD:C_SR + (h + 1) * HD]
        ub_ref[:, h * HD:(h + 1) * HD] = ob[h:h + 1, :] * _silu(gate)

    qm = _heads_rows(zrow, C_MQ) * (HD ** -0.5)
    oc, _ = _decode_attention(qm, cm_ref[:, 0:N_HEADS, :], cm_ref[:, N_HEADS:KV_ROWS, :])
    for h in range(N_HEADS):
        gate = zrow[:, C_MR + h * HD:C_MR + (h + 1) * HD]
        uc_ref[:, h * HD:(h + 1) * HD] = oc[h:h + 1, :] * _silu(gate)


def _sample_mixers(z, za, wa_pad, b_alpha, g_gla_out, cos2, sin2, state, caches, cache_mem):
    db = z.shape[0]
    kw, vw, w = N_HEADS * HD, N_HEADS * GLA_DV, N_HEADS * HD
    const2 = lambda b: (0, 0)
    gathered, gather_specs = [], []
    for c, (win, dil) in zip(caches, SWA_PATTERNS):
        assert c.shape == (1, db, win, 2, N_HEADS, HD) and PAST_LEN >= win
        n_keys = win // dil
        gathered.append(c.reshape(db, n_keys, dil * KV_ROWS, HD))
        gather_specs.append(pl.BlockSpec((None, n_keys, KV_ROWS, HD), lambda b: (b, 0, 0, 0)))
    row3 = lambda n: pl.BlockSpec((None, 1, n), lambda b: (b, 0, 0))
    new_spec = pl.BlockSpec((None, KV_ROWS, HD), lambda b: (b, 0, 0))
    res = pl.pallas_call(
        _sample_kernel,
        grid=(db,),
        in_specs=[pl.BlockSpec(z.shape, const2), pl.BlockSpec(za.shape, const2),
                  pl.BlockSpec(wa_pad.shape, const2), pl.BlockSpec((1, kw), const2),
                  pl.BlockSpec((1, vw), const2), pl.BlockSpec((1, HD), const2), pl.BlockSpec((1, HD), const2),
                  pl.BlockSpec((None, None, N_HEADS, HD, GLA_DV), lambda b: (0, b, 0, 0, 0))]
                 + gather_specs
                 + [pl.BlockSpec((None, N_MEM, KV_ROWS, HD), lambda b: (b, 0, 0, 0))],
        out_specs=[row3(vw), row3(w), row3(w),
                   pl.BlockSpec((None, None, N_HEADS, HD, GLA_DV), lambda b: (0, b, 0, 0, 0))] + [new_spec] * 3,
        out_shape=[jax.ShapeDtypeStruct((db, 1, vw), F32), jax.ShapeDtypeStruct((db, 1, w), F32),
                   jax.ShapeDtypeStruct((db, 1, w), F32), jax.ShapeDtypeStruct(state.shape, F32)]
                  + [jax.ShapeDtypeStruct((db, KV_ROWS, HD), F32)] * 3,
        compiler_params=_cparams(("arbitrary",)),
        name="sample_mixers",
    )(z, za, wa_pad, b_alpha.reshape(1, kw), g_gla_out.reshape(1, vw), cos2, sin2, state,
      *gathered, cache_mem.reshape(db, N_MEM, KV_ROWS, HD))
    ua, ub, uc, s_out = res[:4]
    return ua.reshape(db, vw), ub.reshape(db, w), uc.reshape(db, w), s_out, res[4:]


def _prep_w_in_kernel(wt_ref, wm_ref, wg_ref, buf, ga_buf, sem, ga_sem, *, c_ga, c_gt, rows):
    i = pl.program_id(0)
    n_gt = (wt_ref.shape[0] - c_gt) // rows
    n_lo = c_ga // rows

    def fetch(ii, slot):
        src = jnp.where(ii < n_gt, c_gt + ii * rows,
                        jnp.where(ii < n_gt + n_lo, (ii - n_gt) * rows, c_ga + GLA_RANK + (ii - n_gt - n_lo) * rows))
        return pltpu.make_async_copy(wt_ref.at[pl.ds(pl.multiple_of(src, GLA_RANK), rows)], buf.at[slot], sem.at[slot])

    ga_copy = pltpu.make_async_copy(wt_ref.at[pl.ds(c_ga, GLA_RANK)], ga_buf, ga_sem)

    @pl.when(i == 0)
    def _():
        fetch(0, 0).start()
        ga_copy.start()

    @pl.when(i + 1 < pl.num_programs(0))
    def _():
        fetch(i + 1, lax.rem(i + 1, 2)).start()

    @pl.when(i == 0)
    def _():
        ga_copy.wait()
        wg_ref[0:GLA_RANK, :] = ga_buf[...].astype(BF16)
        wg_ref[GLA_RANK:, :] = jnp.zeros((LANES - GLA_RANK, wg_ref.shape[1]), BF16)

    slot = lax.rem(i, 2)
    fetch(i, slot).wait()
    wm_ref[...] = buf[slot].astype(BF16)


def _prep_w_in(wt, d, *, rows):
    n_all, k = wt.shape
    c_ga = 2 * N_HEADS * HD + 2 * N_HEADS * GLA_DV
    c_gt = n_all - 3 * d
    assert n_all - GLA_RANK == N_MAIN and k == d
    assert (n_all - c_gt) % rows == 0 and c_ga % rows == 0 and (c_gt - c_ga - GLA_RANK) % rows == 0
    return pl.pallas_call(
        functools.partial(_prep_w_in_kernel, c_ga=c_ga, c_gt=c_gt, rows=rows),
        grid=(N_MAIN // rows,),
        in_specs=[pl.BlockSpec(memory_space=pl.ANY)],
        out_specs=[pl.BlockSpec((rows, d), lambda i: (i, 0)), pl.BlockSpec((LANES, d), lambda i: (0, 0))],
        out_shape=[jax.ShapeDtypeStruct((N_MAIN, d), BF16), jax.ShapeDtypeStruct((LANES, d), BF16)],
        scratch_shapes=[pltpu.VMEM((2, rows, d), F32), pltpu.VMEM((GLA_RANK, d), F32),
                        pltpu.SemaphoreType.DMA((2,)), pltpu.SemaphoreType.DMA(())],
        compiler_params=_cparams(("arbitrary",)),
        name="prep_w_in",
    )(wt)


def kernel(x_prompt, x_sample, mem_prompt, state_gla, cache_swa_w128, cache_swa_w512, cache_swa_w2048, cache_mem_kv, g_norm, w_in, w_alpha2, b_alpha, g_gla_out, g_mem, w_mem_kv, w_proj_a, w_proj_b, w_proj_c, w_out, g_final):
    batch, seq, d = x_prompt.shape
    db, dec_seq, _ = x_sample.shape
    assert g_norm.shape[0] == 1 and dec_seq == 1
    w_grp = N_HEADS * HD

    w_main, w_ga = _prep_w_in(jnp.swapaxes(w_in[0], 0, 1), d, rows=512)
    wa_pad = jnp.pad(w_alpha2[0], ((0, LANES - GLA_RANK), (0, 0))).astype(BF16)
    wpa, wpb, wpc, wo = (w[0].astype(BF16) for w in (w_proj_a, w_proj_b, w_proj_c, w_out))

    xp = x_prompt.reshape(batch * seq, d)
    xs = x_sample.reshape(db, d)

    zs, zas = _norm_matmul(xs, g_norm[0], w_main, w_ga, w_rows_out=True, tm=db, tn=1024)
    cos_s, sin_s = _rope_tables(jnp.full((1,), PAST_LEN, jnp.int32))
    caches = (cache_swa_w128, cache_swa_w512, cache_swa_w2048)
    uas, ubs, ucs, gla_s, new_rows = _sample_mixers(
        zs, zas, wa_pad, b_alpha[0], g_gla_out[0], cos_s, sin_s, state_gla, caches, cache_mem_kv)
    y_sample = _final(uas, ubs, ucs, zs, xs, wpa, wpb, wpc, wo, g_final, tm=db).reshape(db, 1, d)

    z, za = _norm_matmul(xp, g_norm[0], w_main, w_ga, w_rows_out=True, out_dtype=BF16, tm=1024, tn=1024)
    mem_kv = _norm_matmul(mem_prompt.reshape(batch * N_MEM, d), g_mem[0], w_mem_kv[0].astype(BF16),
                          tm=batch * N_MEM, tn=512)
    flat_caches = [c.reshape(db, w * KV_ROWS, HD) for c, (w, _) in zip(caches, SWA_PATTERNS)]
    ua, gla_p, *shifted = _gla_prompt(z, za, wa_pad, b_alpha[0], g_gla_out[0], batch, seq,
                                      (flat_caches, new_rows), t_blk=512)
    swa_s = [o.reshape(c.shape) for o, c in zip(shifted, caches)]
    cos_p, sin_p = _rope_tables(jnp.arange(seq, dtype=jnp.int32))
    swa_res = _swa_prompt(z, cos_p, sin_p, batch, seq)
    ub, kbufs, vbufs = swa_res[0], swa_res[1:4], swa_res[4:7]
    uc = _memattn_prompt(z, mem_kv, batch, seq, t_blk=512)
    y_prompt = _final(ua, ub, uc, z, xp, wpa, wpb, wpc, wo, g_final, tm=512).reshape(batch, seq, d)

    swa_p = [jnp.stack([k.reshape(batch, w, N_HEADS, HD), v.reshape(batch, w, N_HEADS, HD)], axis=2)[None]
             for k, v, (w, _) in zip(kbufs, vbufs, SWA_PATTERNS)]
    mem_kv_prompt = mem_kv.reshape(1, batch, N_MEM, 2, N_HEADS, HD)
    return (y_prompt, y_sample, gla_p[None], swa_p[0], swa_p[1], swa_p[2], mem_kv_prompt,
            gla_s, swa_s[0], swa_s[1], swa_s[2])
```

```python
import functools

import numpy as np
import jax
import jax.numpy as jnp
from jax import lax
from jax.experimental import pallas as pl
from jax.experimental.pallas import tpu as pltpu

F32 = jnp.float32
BF16 = jnp.bfloat16

EPS = 1e-6
ROPE_THETA = 10000.0
NEG = -1e30
PAST_LEN = 16384

HD = 128
N_HEADS = 4
GLA_DV = 256
GLA_RANK = 16
GLA_TAU = 16.0
GLA_CHUNK = 64
GLA_SUB = 16
SWA_PATTERNS = ((128, 1), (512, 4), (2048, 16))
SWA_BLK = 128
SWA_STEP = 2048
SWA_UNROLL = 4
N_MEM = 256

LANES = 128
VMEM_LIMIT = 60000 * 1024

C_GT = 0
C_GQ, C_GK, C_GV, C_GR = 6144, 6656, 7168, 8192
C_SQ, C_SK, C_SV, C_SR = 9216, 10752, 12288, 13824
C_MQ, C_MR = 14336, 14848
N_MAIN = 15360


def _cparams(sem):
    return pltpu.CompilerParams(dimension_semantics=sem, vmem_limit_bytes=VMEM_LIMIT)


def _dot(a, b):
    return jnp.dot(a, b, preferred_element_type=F32)


def _dot_nt(a, b):
    return lax.dot_general(a, b, (((1,), (1,)), ((), ())), preferred_element_type=F32)


def _dot_tn(a, b):
    return lax.dot_general(a, b, (((0,), (0,)), ((), ())), preferred_element_type=F32)


def _silu(x):
    return x * jax.nn.sigmoid(x)


def _log_sigmoid(x):
    return jnp.minimum(x, 0.0) - jnp.log1p(jnp.exp(-jnp.abs(x)))


KV_ROWS = 2 * N_HEADS
SHIFT_ROWS = tuple((w - 1) * KV_ROWS for w, _ in SWA_PATTERNS)
SHIFT_PIECE = 4096


def _shift_pieces():
    pieces = []
    for g, total in enumerate(SHIFT_ROWS):
        for r0 in range(0, total, SHIFT_PIECE):
            pieces.append((g, r0, min(SHIFT_PIECE, total - r0)))
    return tuple(pieces)


def _cache_shift_step(p, ca, nr, co, stage, in_sem, out_sem, row_sem):
    pieces = _shift_pieces()
    n_types = len(pieces)
    n_pieces = n_types * ca[0].shape[0]

    def in_copy(t, bb, sl):
        g, r0, n = pieces[t]
        return pltpu.make_async_copy(ca[g].at[bb, pl.ds(KV_ROWS + r0, n)], stage.at[sl, pl.ds(0, n)], in_sem.at[sl])

    def out_copy(t, bb, sl):
        g, r0, n = pieces[t]
        return pltpu.make_async_copy(stage.at[sl, pl.ds(0, n)], co[g].at[bb, pl.ds(r0, n)], out_sem.at[sl])

    def row_copy(t, bb, sl):
        g = pieces[t][0]
        return pltpu.make_async_copy(nr[g].at[bb], co[g].at[bb, pl.ds(SHIFT_ROWS[g], KV_ROWS)], row_sem.at[sl])

    def for_piece(q, cond, fn):
        bb, ty, sl = q // n_types, lax.rem(q, n_types), lax.rem(q, 2)
        for t in range(n_types):
            @pl.when(cond & (ty == t))
            def _(t=t):
                fn(t, bb, sl)

    def finish_write(t, bb, sl):
        out_copy(t, bb, sl).wait()
        if pieces[t][1] == 0:
            row_copy(t, bb, sl).wait()

    def start_write(t, bb, sl):
        in_copy(t, bb, sl).wait()
        out_copy(t, bb, sl).start()
        if pieces[t][1] == 0:
            row_copy(t, bb, sl).start()

    for_piece(p, p == 0, lambda t, bb, sl: in_copy(t, bb, sl).start())
    for_piece(jnp.maximum(p - 1, 0), (p >= 1) & (p <= n_pieces), finish_write)
    for_piece(p + 1, p + 1 < n_pieces, lambda t, bb, sl: in_copy(t, bb, sl).start())
    for_piece(p, p < n_pieces, start_write)


def _norm_matmul_kernel(*refs, has_extra, row_chunk, w_rows_out):
    mm = _dot_nt if w_rows_out else _dot
    if has_extra:
        x_ref, g_ref, w_ref, wx_ref, o_ref, ox_ref, h_ref = refs
    else:
        x_ref, g_ref, w_ref, o_ref, h_ref = refs
    tm = x_ref.shape[0]

    @pl.when(pl.program_id(1) == 0)
    def _():
        def body(c, carry):
            r0 = pl.multiple_of(c * row_chunk, row_chunk)
            x = x_ref[pl.ds(r0, row_chunk), :]
            y = x * lax.rsqrt(jnp.mean(x * x, axis=-1, keepdims=True) + EPS)
            h_ref[pl.ds(r0, row_chunk), :] = (y * g_ref[...]).astype(BF16)
            return carry
        lax.fori_loop(0, tm // row_chunk, body, 0)
        if has_extra:
            ox_ref[...] = mm(h_ref[...], wx_ref[...])

    o_ref[...] = mm(h_ref[...], w_ref[...]).astype(o_ref.dtype)


def _norm_matmul(x, g, w, wx=None, *, w_rows_out=False, out_dtype=F32, tm, tn):
    m, d = x.shape
    n = w.shape[0] if w_rows_out else w.shape[1]
    assert m % tm == 0 and n % tn == 0
    has_extra = wx is not None
    row_chunk = min(tm, 128)
    w_spec = pl.BlockSpec((tn, d), lambda i, j: (j, 0)) if w_rows_out else pl.BlockSpec((d, tn), lambda i, j: (0, j))
    in_specs = [pl.BlockSpec((tm, d), lambda i, j: (i, 0)),
                pl.BlockSpec((1, d), lambda i, j: (0, 0)),
                w_spec]
    out_specs = [pl.BlockSpec((tm, tn), lambda i, j: (i, j))]
    out_shape = [jax.ShapeDtypeStruct((m, n), out_dtype)]
    args = [x, g.reshape(1, d), w]
    if has_extra:
        nx = wx.shape[0] if w_rows_out else wx.shape[1]
        in_specs.append(pl.BlockSpec(wx.shape, lambda i, j: (0, 0)))
        out_specs.append(pl.BlockSpec((tm, nx), lambda i, j: (i, 0)))
        out_shape.append(jax.ShapeDtypeStruct((m, nx), F32))
        args.append(wx)
    res = pl.pallas_call(
        functools.partial(_norm_matmul_kernel, has_extra=has_extra, row_chunk=row_chunk, w_rows_out=w_rows_out),
        grid=(m // tm, n // tn),
        in_specs=in_specs, out_specs=out_specs, out_shape=out_shape,
        scratch_shapes=[pltpu.VMEM((tm, d), BF16)],
        compiler_params=_cparams(("arbitrary", "arbitrary")),
        name="norm_matmul",
    )(*args)
    return res if has_extra else res[0]


def _diag_select_matrix():
    rows = np.arange(GLA_SUB * HD)[:, None] // HD
    cols = np.arange(LANES)[None, :] % GLA_SUB
    return jnp.asarray((rows == cols).astype(np.float32), dtype=BF16)


GLA_FILL_PLAN = (1, 2, 5, 2, 6)


def _gla_chunk(zg_ref, ga_ref, wa_ref, ba_ref, gg_ref, em_ref, st_ref, u_ref, fill=()):
    C, SUB, n_sub = GLA_CHUNK, GLA_SUB, GLA_CHUNK // GLA_SUB
    fill = list(fill)
    assert len(fill) in (0, sum(GLA_FILL_PLAN))

    def run_fill(phase):
        for _ in range(GLA_FILL_PLAN[phase] if fill else 0):
            fill.pop(0)()

    kw = N_HEADS * HD
    c_k, c_v, c_r = kw, 2 * kw, 2 * kw + N_HEADS * GLA_DV
    row = lax.broadcasted_iota(jnp.int32, (C, C), 0)
    col = lax.broadcasted_iota(jnp.int32, (C, C), 1)
    tri = (col <= row).astype(F32)
    row_c = lax.broadcasted_iota(jnp.int32, (C, HD), 0)
    sub_row = lax.broadcasted_iota(jnp.int32, (SUB, HD), 0)
    lane_c = lax.broadcasted_iota(jnp.int32, (SUB, C), 1)

    if True:
        rows = slice(None)
        xa = _dot(ga_ref[rows, :].astype(BF16), wa_ref[...]) + ba_ref[...]
        run_fill(0)
        la = _log_sigmoid(xa) / GLA_TAU
        b_all = jnp.dot(tri, la, preferred_element_type=F32, precision=lax.Precision.HIGHEST)
        run_fill(1)

        a_off = []
        per_head = []
        for h in range(N_HEADS):
            q = zg_ref[rows, h * HD:(h + 1) * HD].astype(F32) * (HD ** -0.5)
            k = zg_ref[rows, c_k + h * HD:c_k + (h + 1) * HD].astype(F32)
            v = zg_ref[rows, c_v + h * GLA_DV:c_v + (h + 1) * GLA_DV]
            b = b_all[:, h * HD:(h + 1) * HD]
            for i in range(n_sub):
                if i == 0:
                    a_off.append(jnp.zeros((SUB, C), F32))
                else:
                    sl = slice(i * SUB, (i + 1) * SUB)
                    b_ref_row = b[i * SUB - 1:i * SUB, :]
                    qs = q[sl] * jnp.exp(b[sl] - b_ref_row)
                    ks = k * jnp.exp(jnp.where(row_c < i * SUB, b_ref_row - b, -jnp.inf))
                    a_off.append(_dot_nt(qs.astype(BF16), ks.astype(BF16)))
            st = st_ref[h]
            o_inter = _dot_nt((q * jnp.exp(b)).astype(BF16), st.astype(BF16))
            b_end = b[C - 1:C, :]
            kd = k * jnp.exp(b_end - b)
            st_ref[h] = st * jnp.exp(b_end) + _dot_tn(v.astype(BF16), kd.astype(BF16))
            per_head.append((q, k, v, b, o_inter))
        run_fill(2)

        p_rows = []
        for h in range(N_HEADS):
            q, k, _, b, _ = per_head[h]
            for i in range(n_sub):
                sl = slice(i * SUB, (i + 1) * SUB)
                q_i, k_i, b_i = q[sl], k[sl], b[sl]
                slabs = []
                for s in range(SUB):
                    e = jnp.where(sub_row >= s, b_i - b_i[s:s + 1, :], -jnp.inf)
                    slabs.append((q_i * k_i[s:s + 1, :]) * jnp.exp(e))
                p_rows.append(jnp.concatenate(slabs, axis=1).astype(BF16))
        r_all = _dot(jnp.concatenate(p_rows, axis=0), em_ref[...])
        run_fill(3)

        for h in range(N_HEADS):
            _, _, v, _, o_inter = per_head[h]
            a_rows = []
            for i in range(n_sub):
                idx = h * n_sub + i
                r_i = r_all[idx * SUB:(idx + 1) * SUB, :C]
                in_blk = (lane_c >= i * SUB) & (lane_c < (i + 1) * SUB)
                a_rows.append(a_off[idx] + jnp.where(in_blk, r_i, 0.0))
            a = jnp.concatenate(a_rows, axis=0)
            o = _dot(a.astype(BF16), v.astype(BF16)) + o_inter
            y = o * lax.rsqrt(jnp.mean(o * o, axis=-1, keepdims=True) + EPS)
            y = y * gg_ref[:, h * GLA_DV:(h + 1) * GLA_DV]
            gate = zg_ref[rows, c_r + h * GLA_DV:c_r + (h + 1) * GLA_DV].astype(F32)
            u_ref[rows, h * GLA_DV:(h + 1) * GLA_DV] = (y * _silu(gate)).astype(u_ref.dtype)
        run_fill(4)


def _proj_gla_kernel(*refs, n_i, n_j, blocks_per_batch, n_shift):
    x_ref, g_ref, w_ref, wx_ref, wa_ref, ba_ref, gg_ref, em_ref = refs[:8]
    n_in = 8 + 2 * n_shift
    z_ref, u_ref, sout_ref = refs[n_in:n_in + 3]
    h_ref, zg_ref, zga_ref, st_ref, zc_ref, gac_ref, uc_ref = refs[n_in + 3 + n_shift:n_in + 10 + n_shift]
    i, j = pl.program_id(0), pl.program_id(1)
    tm, tn = z_ref.shape
    n_chunks = tm // GLA_CHUNK
    gla_w = zg_ref.shape[2]
    assert C_GQ % tn == 0 and gla_w % tn == 0 and n_chunks >= n_j
    slot = lax.rem(i, 2)
    has_mm, has_gla = i < n_i, i >= 1
    blk_in_batch = lax.rem(jnp.maximum(i - 1, 0), blocks_per_batch)

    if n_shift:
        stage, in_sem, out_sem, row_sem = refs[n_in + 10 + n_shift:]
        _cache_shift_step(i * n_j + j, refs[8:8 + n_shift], refs[8 + n_shift:n_in],
                          refs[n_in + 3:n_in + 3 + n_shift], stage, in_sem, out_sem, row_sem)

    @pl.when(has_mm & (j == 0))
    def _():
        row_chunk = 128

        def body(c, carry):
            r0 = pl.multiple_of(c * row_chunk, row_chunk)
            x = x_ref[pl.ds(r0, row_chunk), :]
            y = x * lax.rsqrt(jnp.mean(x * x, axis=-1, keepdims=True) + EPS)
            h_ref[pl.ds(r0, row_chunk), :] = (y * g_ref[...]).astype(BF16)
            return carry
        lax.fori_loop(0, tm // row_chunk, body, 0)
        zga_ref[slot] = _dot_nt(h_ref[...], wx_ref[...])

    @pl.when(has_gla & (j == 0) & (blk_in_batch == 0))
    def _():
        st_ref[...] = jnp.zeros_like(st_ref)

    def project():
        z_ref[...] = _dot_nt(h_ref[...], w_ref[...]).astype(z_ref.dtype)

    def chunk_rows(c):
        return pl.ds(pl.multiple_of(c * GLA_CHUNK, GLA_CHUNK), GLA_CHUNK)

    def stage_in(c):
        zc_ref[...] = zg_ref[1 - slot, chunk_rows(c), :]
        gac_ref[...] = zga_ref[1 - slot, chunk_rows(c), :]

    def stage_out(c):
        u_ref[chunk_rows(c), :] = uc_ref[...]

    def project_piece(p, parts=4):
        r, c = divmod(p, parts)
        rs, cs = slice(r * (tm // parts), (r + 1) * (tm // parts)), slice(c * (tn // parts), (c + 1) * (tn // parts))
        z_ref[rs, cs] = _dot_nt(h_ref[rs, :], w_ref[cs, :]).astype(z_ref.dtype)

    def gla(fill=()):
        _gla_chunk(zc_ref, gac_ref, wa_ref, ba_ref, gg_ref, em_ref, st_ref, uc_ref, fill)

    @pl.when(has_gla)
    def _():
        stage_in(j)

    @pl.when(has_mm & has_gla)
    def _():
        gla(tuple(functools.partial(project_piece, p) for p in range(sum(GLA_FILL_PLAN))))

    @pl.when(i == 0)
    def _():
        project()

    @pl.when(i == n_i)
    def _():
        gla()

    @pl.when(has_gla)
    def _():
        stage_out(j)

    @pl.when(has_gla & (j == n_j - 1))
    def _():
        for c in range(n_j, n_chunks):
            stage_in(c)
            gla()
            stage_out(c)

    for t in range(gla_w // tn):
        @pl.when(has_mm & (j == C_GQ // tn + t))
        def _(t=t):
            zg_ref[slot, :, t * tn:(t + 1) * tn] = z_ref[...]

    @pl.when(has_gla & (j == n_j - 1) & (blk_in_batch == blocks_per_batch - 1))
    def _():
        for h in range(N_HEADS):
            sout_ref[h] = st_ref[h].T


def _proj_gla(x, g, wt, wt_ga, wa_pad, b_alpha, g_gla_out, batch, seq, shift, *, tm, tn):
    m, d = x.shape
    n = wt.shape[0]
    assert m == batch * seq and seq % tm == 0 and n % tn == 0 and tm % GLA_CHUNK == 0
    n_i, n_j = m // tm, n // tn
    kw, vw = N_HEADS * HD, N_HEADS * GLA_DV
    gla_w = 2 * kw + 2 * vw
    caches, new_rows = shift
    n_shift = len(caches)
    assert len(_shift_pieces()) * caches[0].shape[0] < (n_i + 1) * n_j
    const = lambda i, j: (0, 0)
    row_i = lambda i: jnp.minimum(i, n_i - 1)
    col_j = lambda i, j: jnp.where(i < n_i, j, n_j - 1)
    prev_i = lambda i: jnp.maximum(i - 1, 0)
    any_spec = pl.BlockSpec(memory_space=pl.ANY)
    res = pl.pallas_call(
        functools.partial(_proj_gla_kernel, n_i=n_i, n_j=n_j, blocks_per_batch=seq // tm, n_shift=n_shift),
        grid=(n_i + 1, n_j),
        in_specs=[pl.BlockSpec((tm, d), lambda i, j: (row_i(i), 0)),
                  pl.BlockSpec((1, d), const),
                  pl.BlockSpec((tn, d), lambda i, j: (col_j(i, j), 0)),
                  pl.BlockSpec(wt_ga.shape, const),
                  pl.BlockSpec((LANES, kw), const), pl.BlockSpec((1, kw), const), pl.BlockSpec((1, vw), const),
                  pl.BlockSpec((GLA_SUB * HD, LANES), const)]
                 + [any_spec] * n_shift + [pl.BlockSpec(r.shape, lambda i, j: (0, 0, 0)) for r in new_rows],
        out_specs=[pl.BlockSpec((tm, tn), lambda i, j: (row_i(i), col_j(i, j))),
                   pl.BlockSpec((tm, vw), lambda i, j: (prev_i(i), 0)),
                   pl.BlockSpec((None, N_HEADS, HD, GLA_DV), lambda i, j: (prev_i(i) // (seq // tm), 0, 0, 0))]
                  + [any_spec] * n_shift,
        out_shape=[jax.ShapeDtypeStruct((m, n), BF16), jax.ShapeDtypeStruct((m, vw), BF16),
                   jax.ShapeDtypeStruct((batch, N_HEADS, HD, GLA_DV), F32)]
                  + [jax.ShapeDtypeStruct(c.shape, c.dtype) for c in caches],
        scratch_shapes=[pltpu.VMEM((tm, d), BF16), pltpu.VMEM((2, tm, gla_w), BF16), pltpu.VMEM((2, tm, LANES), F32),
                        pltpu.VMEM((N_HEADS, GLA_DV, HD), F32),
                        pltpu.VMEM((GLA_CHUNK, gla_w), BF16), pltpu.VMEM((GLA_CHUNK, LANES), F32),
                        pltpu.VMEM((GLA_CHUNK, vw), BF16), pltpu.VMEM((2, SHIFT_PIECE, HD), F32)]
                       + [pltpu.SemaphoreType.DMA((2,))] * 3,
        compiler_params=pltpu.CompilerParams(
            dimension_semantics=("arbitrary", "arbitrary"), vmem_limit_bytes=63 * 1024 * 1024),
        name="proj_gla",
    )(x, g.reshape(1, d), wt, wt_ga, wa_pad, b_alpha.reshape(1, kw), g_gla_out.reshape(1, vw),
      _diag_select_matrix(), *caches, *new_rows)
    return res[0], res[1], res[2], res[3:]


def _rope_tables(pos):
    half = HD // 2
    inv = ROPE_THETA ** (-jnp.arange(half, dtype=F32) / half)
    ang = pos.astype(F32)[:, None] * inv[None, :]
    cos, sin = jnp.cos(ang), jnp.sin(ang)
    return jnp.concatenate([cos, cos], axis=1), jnp.concatenate([-sin, sin], axis=1)


def _rope(x, cos2, sin2):
    return x * cos2 + pltpu.roll(x, HD // 2, axis=1) * sin2


def _swa_kernel(*refs):
    (q0, q1, q2, k0, k1, k2, v0, v1, v2, sr_ref, cos_ref, sin_ref,
     u_ref, kb0, kb1, kb2, vb0, vb1, vb2,
     q_s, k_s, v_s, o_s, l_s) = refs
    q_in, k_in, v_in = (q0, q1, q2), (k0, k1, k2), (v0, v1, v2)
    kb, vb = (kb0, kb1, kb2), (vb0, vb1, vb2)
    i = pl.program_id(2)
    T = SWA_STEP
    n_grp = len(SWA_PATTERNS)

    @pl.when(i == 0)
    def _():
        for g, (_, dil) in enumerate(SWA_PATTERNS):
            unit = dil * SWA_BLK
            k_s[g, T - unit:T, :] = jnp.zeros((unit, HD), F32)
            v_s[g, T - unit:T, :] = jnp.zeros((unit, HD), F32)

    @pl.when(i > 0)
    def _():
        for g, (_, dil) in enumerate(SWA_PATTERNS):
            unit = dil * SWA_BLK
            k_s[g, T - unit:T, :] = k_s[g, 2 * T - unit:2 * T, :]
            v_s[g, T - unit:T, :] = v_s[g, 2 * T - unit:2 * T, :]

    cos2, sin2 = cos_ref[...], sin_ref[...]
    for g in range(n_grp):
        q_s[g] = _rope(q_in[g][...].astype(F32), cos2, sin2) * (HD ** -0.5)
        k_s[g, T:2 * T, :] = _rope(k_in[g][...].astype(F32), cos2, sin2)
        v_s[g, T:2 * T, :] = v_in[g][...].astype(F32)

    qi = lax.broadcasted_iota(jnp.int32, (SWA_BLK, 2 * SWA_BLK), 0)
    ki = lax.broadcasted_iota(jnp.int32, (SWA_BLK, 2 * SWA_BLK), 1)
    delta = SWA_BLK + qi - ki

    def block(g, n):
        win, dil = SWA_PATTERNS[g]
        unit = dil * SWA_BLK
        u = lax.shift_right_logical(n, dil.bit_length() - 1)
        r = lax.bitwise_and(n, dil - 1)

        def rows(start, size):
            return pl.ds(start, size) if dil == 1 else pl.ds(start, size, stride=dil)

        q = q_s[g, rows(u * unit + r, SWA_BLK), :]
        kk = k_s[g, rows(T + (u - 1) * unit + r, 2 * SWA_BLK), :]
        vv = v_s[g, rows(T + (u - 1) * unit + r, 2 * SWA_BLK), :]
        s = _dot_nt(q.astype(BF16), kk.astype(BF16))
        ki_min = jnp.where((i * (T // unit) + u) == 0, SWA_BLK, 0)
        s = jnp.where((delta >= 0) & (delta <= win // dil) & (ki >= ki_min), s, NEG)
        m = jnp.max(s, axis=-1, keepdims=True)
        p = jnp.exp(s - m)
        den = jnp.sum(p, axis=-1, keepdims=True)
        o = _dot(p.astype(BF16), vv.astype(BF16)) / den
        lse = m + jnp.log(den)
        o_s[g, rows(u * unit + r, SWA_BLK), :] = o
        l_s[g, rows(u * unit + r, SWA_BLK), :] = jnp.broadcast_to(lse, (SWA_BLK, HD))

    def blocks(it, carry):
        for g in range(n_grp):
            for j in range(SWA_UNROLL):
                block(g, it * SWA_UNROLL + j)
        return carry

    lax.fori_loop(0, T // SWA_BLK // SWA_UNROLL, blocks, 0)

    rc = 256

    def merge(c, carry):
        r0 = pl.multiple_of(c * rc, rc)
        rr = pl.ds(r0, rc)
        ls = [l_s[g, rr, :] for g in range(n_grp)]
        mx = functools.reduce(jnp.maximum, ls)
        es = [jnp.exp(l - mx) for l in ls]
        tot = functools.reduce(lambda a, b: a + b, es)
        ob = functools.reduce(lambda a, b: a + b, [(es[g] / tot) * o_s[g, rr, :] for g in range(n_grp)])
        u_ref[rr, :] = (ob * _silu(sr_ref[rr, :].astype(F32))).astype(u_ref.dtype)
        return carry

    lax.fori_loop(0, T // rc, merge, 0)

    @pl.when(i == pl.num_programs(2) - 1)
    def _():
        for g, (win, _) in enumerate(SWA_PATTERNS):
            kb[g][...] = k_s[g, 2 * T - win:2 * T, :]
            vb[g][...] = v_s[g, 2 * T - win:2 * T, :]


def _swa_prompt(z, cos2, sin2, batch, seq):
    T = SWA_STEP
    assert seq % T == 0 and all(w <= T for w, _ in SWA_PATTERNS)
    nt = seq // T
    w_grp = N_HEADS * HD

    def zspec(col0, g):
        cb0 = (col0 + g * w_grp) // HD
        return pl.BlockSpec((T, HD), lambda b, j, i: (b * nt + i, cb0 + j))

    in_specs = ([zspec(C_SQ, g) for g in range(3)] + [zspec(C_SK, g) for g in range(3)]
                + [zspec(C_SV, g) for g in range(3)] + [zspec(C_SR, 0)]
                + [pl.BlockSpec((T, HD), lambda b, j, i: (i, 0))] * 2)
    buf_specs = [pl.BlockSpec((None, w, HD), lambda b, j, i: (b, 0, j)) for w, _ in SWA_PATTERNS]
    buf_shapes = [jax.ShapeDtypeStruct((batch, w, w_grp), F32) for w, _ in SWA_PATTERNS]
    return pl.pallas_call(
        _swa_kernel,
        grid=(batch, N_HEADS, nt),
        in_specs=in_specs,
        out_specs=[pl.BlockSpec((T, HD), lambda b, j, i: (b * nt + i, j))] + buf_specs + buf_specs,
        out_shape=[jax.ShapeDtypeStruct((batch * seq, w_grp), BF16)] + buf_shapes + buf_shapes,
        scratch_shapes=[pltpu.VMEM((3, T, HD), F32), pltpu.VMEM((3, 2 * T, HD), F32),
                        pltpu.VMEM((3, 2 * T, HD), F32), pltpu.VMEM((3, T, HD), F32),
                        pltpu.VMEM((3, T, HD), F32)],
        compiler_params=_cparams(("arbitrary", "arbitrary", "arbitrary")),
        name="swa",
    )(*([z] * 10), cos2, sin2)


def _memattn_kernel(q_ref, mr_ref, kv_ref, u_ref):
    w = N_HEADS * HD
    for h in range(N_HEADS):
        cs = slice(h * HD, (h + 1) * HD)
        q = q_ref[:, cs].astype(F32) * (HD ** -0.5)
        k = kv_ref[:, cs]
        v = kv_ref[:, w + h * HD:w + (h + 1) * HD]
        s = _dot_nt(q.astype(BF16), k.astype(BF16))
        e = jnp.exp(s - jnp.max(s, axis=-1, keepdims=True))
        p = e / jnp.sum(e, axis=-1, keepdims=True)
        o = _dot(p.astype(BF16), v.astype(BF16))
        u_ref[:, cs] = (o * _silu(mr_ref[:, cs].astype(F32))).astype(u_ref.dtype)


def _memattn_prompt(z, mem_kv, batch, seq, *, t_blk):
    nt = seq // t_blk
    w = N_HEADS * HD
    return pl.pallas_call(
        _memattn_kernel,
        grid=(batch, nt),
        in_specs=[pl.BlockSpec((t_blk, w), lambda b, t: (b * nt + t, C_MQ // w)),
                  pl.BlockSpec((t_blk, w), lambda b, t: (b * nt + t, C_MR // w)),
                  pl.BlockSpec((N_MEM, 2 * w), lambda b, t: (b, 0))],
        out_specs=pl.BlockSpec((t_blk, w), lambda b, t: (b * nt + t, 0)),
        out_shape=jax.ShapeDtypeStruct((batch * seq, w), BF16),
        compiler_params=_cparams(("arbitrary", "arbitrary")),
        name="memattn",
    )(z, z, mem_kv)


def _final_kernel(ua_ref, ub_ref, uc_ref, gt_ref, x_ref, wa_ref, wb_ref, wc_ref, wo_ref, gf_ref, y_ref):
    d = x_ref.shape[1]
    ya = _dot(ua_ref[...].astype(BF16), wa_ref[...])
    yb = _dot(ub_ref[...].astype(BF16), wb_ref[...])
    yc = _dot(uc_ref[...].astype(BF16), wc_ref[...])
    mix = (jax.nn.sigmoid(gt_ref[:, 0:d].astype(F32)) * ya + jax.nn.sigmoid(gt_ref[:, d:2 * d].astype(F32)) * yb
           + jax.nn.sigmoid(gt_ref[:, 2 * d:3 * d].astype(F32)) * yc)
    xo = x_ref[...] + _dot(mix.astype(BF16), wo_ref[...])
    y = xo * lax.rsqrt(jnp.mean(xo * xo, axis=-1, keepdims=True) + EPS)
    y_ref[...] = y * gf_ref[...]


def _final(ua, ub, uc, z, x, wa, wb, wc, wo, g_final, *, tm):
    m, d = x.shape
    const = lambda i: (0, 0)
    resident = dict(pipeline_mode=pl.Buffered(1))
    return pl.pallas_call(
        _final_kernel,
        grid=(m // tm,),
        in_specs=[pl.BlockSpec((tm, ua.shape[1]), lambda i: (i, 0)),
                  pl.BlockSpec((tm, ub.shape[1]), lambda i: (i, 0)),
                  pl.BlockSpec((tm, uc.shape[1]), lambda i: (i, 0)),
                  pl.BlockSpec((tm, 3 * d), lambda i: (i, C_GT // (3 * d))),
                  pl.BlockSpec((tm, d), lambda i: (i, 0)),
                  pl.BlockSpec(wa.shape, const, **resident),
                  pl.BlockSpec(wb.shape, const, **resident),
                  pl.BlockSpec(wc.shape, const, **resident),
                  pl.BlockSpec(wo.shape, const, **resident),
                  pl.BlockSpec((1, d), const)],
        out_specs=pl.BlockSpec((tm, d), lambda i: (i, 0)),
        out_shape=jax.ShapeDtypeStruct((m, d), F32),
        compiler_params=_cparams(("arbitrary",)),
        name="final",
    )(ua, ub, uc, z, x, wa, wb, wc, wo, g_final.reshape(1, d))


def _heads_rows(row, col0):
    return jnp.concatenate([row[:, col0 + h * HD:col0 + (h + 1) * HD] for h in range(N_HEADS)], axis=0)


def _decode_attention(q4, kk, vv, k_new=None, v_new=None):
    s = jnp.sum(kk * q4[None], axis=-1, keepdims=True)
    m = jnp.max(s, axis=0)
    if k_new is not None:
        s_new = jnp.sum(k_new * q4, axis=-1, keepdims=True)
        m = jnp.maximum(m, s_new)
    p = jnp.exp(s - m[None])
    den = jnp.sum(p, axis=0)
    acc = jnp.sum(p * vv, axis=0)
    if k_new is not None:
        p_new = jnp.exp(s_new - m)
        den = den + p_new
        acc = acc + p_new * v_new
    return acc / den, m + jnp.log(den)


def _sample_kernel(*refs):
    (z_ref, za_ref, wa_ref, ba_ref, gg_ref, cos_ref, sin_ref, st_ref, cg0, cg1, cg2, cm_ref,
     ua_ref, ub_ref, uc_ref, sout_ref, nr0, nr1, nr2) = refs
    cg, nr = (cg0, cg1, cg2), (nr0, nr1, nr2)
    b = pl.program_id(0)
    zrow = z_ref[pl.ds(b, 1), :]

    ga8 = jnp.broadcast_to(za_ref[pl.ds(b, 1), :], (8, LANES))
    xa = _dot(ga8.astype(BF16), wa_ref[...])[0:1, :] + ba_ref[...]
    a_row = jnp.exp(_log_sigmoid(xa) / GLA_TAU)
    eye = (lax.broadcasted_iota(jnp.int32, (HD, HD), 0) == lax.broadcasted_iota(jnp.int32, (HD, HD), 1))

    def col(row_vec):
        return jnp.sum(jnp.where(eye, jnp.broadcast_to(row_vec, (HD, HD)), 0.0), axis=1, keepdims=True)

    for h in range(N_HEADS):
        q = zrow[:, C_GQ + h * HD:C_GQ + (h + 1) * HD] * (HD ** -0.5)
        k = zrow[:, C_GK + h * HD:C_GK + (h + 1) * HD]
        v = zrow[:, C_GV + h * GLA_DV:C_GV + (h + 1) * GLA_DV]
        s_new = col(a_row[:, h * HD:(h + 1) * HD]) * st_ref[h] + col(k) * v
        sout_ref[h] = s_new
        o = jnp.sum(col(q) * s_new, axis=0, keepdims=True)
        y = o * lax.rsqrt(jnp.mean(o * o, axis=-1, keepdims=True) + EPS)
        y = y * gg_ref[:, h * GLA_DV:(h + 1) * GLA_DV]
        gate = zrow[:, C_GR + h * GLA_DV:C_GR + (h + 1) * GLA_DV]
        ua_ref[:, h * GLA_DV:(h + 1) * GLA_DV] = y * _silu(gate)

    cos2, sin2 = cos_ref[...], sin_ref[...]
    outs, lses = [], []
    w_grp = N_HEADS * HD
    for g in range(3):
        q4 = _rope(_heads_rows(zrow, C_SQ + g * w_grp), cos2, sin2) * (HD ** -0.5)
        k4 = _rope(_heads_rows(zrow, C_SK + g * w_grp), cos2, sin2)
        v4 = _heads_rows(zrow, C_SV + g * w_grp)
        nr[g][0:N_HEADS, :] = k4
        nr[g][N_HEADS:KV_ROWS, :] = v4
        o, lse = _decode_attention(q4, cg[g][:, 0:N_HEADS, :], cg[g][:, N_HEADS:KV_ROWS, :], k4, v4)
        outs.append(o)
        lses.append(lse)
    mx = functools.reduce(jnp.maximum, lses)
    es = [jnp.exp(l - mx) for l in lses]
    tot = es[0] + es[1] + es[2]
    ob = (es[0] / tot) * outs[0] + (es[1] / tot) * outs[1] + (es[2] / tot) * outs[2]
    for h in range(N_HEADS):
        gate = zrow[:, C_SR + h * HD:C_SR + (h + 1) * HD]
        ub_ref[:, h * HD:(h + 1) * HD] = ob[h:h + 1, :] * _silu(gate)

    qm = _heads_rows(zrow, C_MQ) * (HD ** -0.5)
    oc, _ = _decode_attention(qm, cm_ref[:, 0:N_HEADS, :], cm_ref[:, N_HEADS:KV_ROWS, :])
    for h in range(N_HEADS):
        gate = zrow[:, C_MR + h * HD:C_MR + (h + 1) * HD]
        uc_ref[:, h * HD:(h + 1) * HD] = oc[h:h + 1, :] * _silu(gate)


def _sample_mixers(z, za, wa_pad, b_alpha, g_gla_out, cos2, sin2, state, caches, cache_mem):
    db = z.shape[0]
    kw, vw, w = N_HEADS * HD, N_HEADS * GLA_DV, N_HEADS * HD
    const2 = lambda b: (0, 0)
    gathered, gather_specs = [], []
    for c, (win, dil) in zip(caches, SWA_PATTERNS):
        assert c.shape == (1, db, win, 2, N_HEADS, HD) and PAST_LEN >= win
        n_keys = win // dil
        gathered.append(c.reshape(db, n_keys, dil * KV_ROWS, HD))
        gather_specs.append(pl.BlockSpec((None, n_keys, KV_ROWS, HD), lambda b: (b, 0, 0, 0)))
    row3 = lambda n: pl.BlockSpec((None, 1, n), lambda b: (b, 0, 0))
    new_spec = pl.BlockSpec((None, KV_ROWS, HD), lambda b: (b, 0, 0))
    res = pl.pallas_call(
        _sample_kernel,
        grid=(db,),
        in_specs=[pl.BlockSpec(z.shape, const2), pl.BlockSpec(za.shape, const2),
                  pl.BlockSpec(wa_pad.shape, const2), pl.BlockSpec((1, kw), const2),
                  pl.BlockSpec((1, vw), const2), pl.BlockSpec((1, HD), const2), pl.BlockSpec((1, HD), const2),
                  pl.BlockSpec((None, None, N_HEADS, HD, GLA_DV), lambda b: (0, b, 0, 0, 0))]
                 + gather_specs
                 + [pl.BlockSpec((None, N_MEM, KV_ROWS, HD), lambda b: (b, 0, 0, 0))],
        out_specs=[row3(vw), row3(w), row3(w),
                   pl.BlockSpec((None, None, N_HEADS, HD, GLA_DV), lambda b: (0, b, 0, 0, 0))] + [new_spec] * 3,
        out_shape=[jax.ShapeDtypeStruct((db, 1, vw), F32), jax.ShapeDtypeStruct((db, 1, w), F32),
                   jax.ShapeDtypeStruct((db, 1, w), F32), jax.ShapeDtypeStruct(state.shape, F32)]
                  + [jax.ShapeDtypeStruct((db, KV_ROWS, HD), F32)] * 3,
        compiler_params=_cparams(("arbitrary",)),
        name="sample_mixers",
    )(z, za, wa_pad, b_alpha.reshape(1, kw), g_gla_out.reshape(1, vw), cos2, sin2, state,
      *gathered, cache_mem.reshape(db, N_MEM, KV_ROWS, HD))
    ua, ub, uc, s_out = res[:4]
    return ua.reshape(db, vw), ub.reshape(db, w), uc.reshape(db, w), s_out, res[4:]


def _prep_w_in_kernel(wt_ref, wm_ref, wg_ref, buf, ga_buf, sem, ga_sem, *, c_ga, c_gt, rows):
    i = pl.program_id(0)
    n_gt = (wt_ref.shape[0] - c_gt) // rows
    n_lo = c_ga // rows

    def fetch(ii, slot):
        src = jnp.where(ii < n_gt, c_gt + ii * rows,
                        jnp.where(ii < n_gt + n_lo, (ii - n_gt) * rows, c_ga + GLA_RANK + (ii - n_gt - n_lo) * rows))
        return pltpu.make_async_copy(wt_ref.at[pl.ds(pl.multiple_of(src, GLA_RANK), rows)], buf.at[slot], sem.at[slot])

    ga_copy = pltpu.make_async_copy(wt_ref.at[pl.ds(c_ga, GLA_RANK)], ga_buf, ga_sem)

    @pl.when(i == 0)
    def _():
        fetch(0, 0).start()
        ga_copy.start()

    @pl.when(i + 1 < pl.num_programs(0))
    def _():
        fetch(i + 1, lax.rem(i + 1, 2)).start()

    @pl.when(i == 0)
    def _():
        ga_copy.wait()
        wg_ref[0:GLA_RANK, :] = ga_buf[...].astype(BF16)
        wg_ref[GLA_RANK:, :] = jnp.zeros((LANES - GLA_RANK, wg_ref.shape[1]), BF16)

    slot = lax.rem(i, 2)
    fetch(i, slot).wait()
    wm_ref[...] = buf[slot].astype(BF16)


def _prep_w_in(wt, d, *, rows):
    n_all, k = wt.shape
    c_ga = 2 * N_HEADS * HD + 2 * N_HEADS * GLA_DV
    c_gt = n_all - 3 * d
    assert n_all - GLA_RANK == N_MAIN and k == d
    assert (n_all - c_gt) % rows == 0 and c_ga % rows == 0 and (c_gt - c_ga - GLA_RANK) % rows == 0
    return pl.pallas_call(
        functools.partial(_prep_w_in_kernel, c_ga=c_ga, c_gt=c_gt, rows=rows),
        grid=(N_MAIN // rows,),
        in_specs=[pl.BlockSpec(memory_space=pl.ANY)],
        out_specs=[pl.BlockSpec((rows, d), lambda i: (i, 0)), pl.BlockSpec((LANES, d), lambda i: (0, 0))],
        out_shape=[jax.ShapeDtypeStruct((N_MAIN, d), BF16), jax.ShapeDtypeStruct((LANES, d), BF16)],
        scratch_shapes=[pltpu.VMEM((2, rows, d), F32), pltpu.VMEM((GLA_RANK, d), F32),
                        pltpu.SemaphoreType.DMA((2,)), pltpu.SemaphoreType.DMA(())],
        compiler_params=_cparams(("arbitrary",)),
        name="prep_w_in",
    )(wt)


def kernel(x_prompt, x_sample, mem_prompt, state_gla, cache_swa_w128, cache_swa_w512, cache_swa_w2048, cache_mem_kv, g_norm, w_in, w_alpha2, b_alpha, g_gla_out, g_mem, w_mem_kv, w_proj_a, w_proj_b, w_proj_c, w_out, g_final):
    batch, seq, d = x_prompt.shape
    db, dec_seq, _ = x_sample.shape
    assert g_norm.shape[0] == 1 and dec_seq == 1
    w_grp = N_HEADS * HD

    w_main, w_ga = _prep_w_in(jnp.swapaxes(w_in[0], 0, 1), d, rows=512)
    wa_pad = jnp.pad(w_alpha2[0], ((0, LANES - GLA_RANK), (0, 0))).astype(BF16)
    wpa, wpb, wpc, wo = (w[0].astype(BF16) for w in (w_proj_a, w_proj_b, w_proj_c, w_out))

    xp = x_prompt.reshape(batch * seq, d)
    xs = x_sample.reshape(db, d)

    zs, zas = _norm_matmul(xs, g_norm[0], w_main, w_ga, w_rows_out=True, tm=db, tn=1024)
    cos_s, sin_s = _rope_tables(jnp.full((1,), PAST_LEN, jnp.int32))
    caches = (cache_swa_w128, cache_swa_w512, cache_swa_w2048)
    uas, ubs, ucs, gla_s, new_rows = _sample_mixers(
        zs, zas, wa_pad, b_alpha[0], g_gla_out[0], cos_s, sin_s, state_gla, caches, cache_mem_kv)
    y_sample = _final(uas, ubs, ucs, zs, xs, wpa, wpb, wpc, wo, g_final, tm=db).reshape(db, 1, d)

    flat_caches = [c.reshape(db, w * KV_ROWS, HD) for c, (w, _) in zip(caches, SWA_PATTERNS)]
    z, ua, gla_p, shifted = _proj_gla(xp, g_norm[0], w_main, w_ga, wa_pad, b_alpha[0], g_gla_out[0], batch, seq,
                                      (flat_caches, new_rows), tm=1024, tn=1024)
    swa_s = [o.reshape(c.shape) for o, c in zip(shifted, caches)]
    mem_kv = _norm_matmul(mem_prompt.reshape(batch * N_MEM, d), g_mem[0], w_mem_kv[0].astype(BF16),
                          tm=batch * N_MEM, tn=512)
    cos_p, sin_p = _rope_tables(jnp.arange(seq, dtype=jnp.int32))
    swa_res = _swa_prompt(z, cos_p, sin_p, batch, seq)
    ub, kbufs, vbufs = swa_res[0], swa_res[1:4], swa_res[4:7]
    uc = _memattn_prompt(z, mem_kv, batch, seq, t_blk=512)
    y_prompt = _final(ua, ub, uc, z, xp, wpa, wpb, wpc, wo, g_final, tm=512).reshape(batch, seq, d)

    swa_p = [jnp.stack([k.reshape(batch, w, N_HEADS, HD), v.reshape(batch, w, N_HEADS, HD)], axis=2)[None]
             for k, v, (w, _) in zip(kbufs, vbufs, SWA_PATTERNS)]
    mem_kv_prompt = mem_kv.reshape(1, batch, N_MEM, 2, N_HEADS, HD)
    return (y_prompt, y_sample, gla_p[None], swa_p[0], swa_p[1], swa_p[2], mem_kv_prompt,
            gla_s, swa_s[0], swa_s[1], swa_s[2])
```

```python
import functools

import numpy as np
import jax
import jax.numpy as jnp
from jax import lax
from jax.experimental import pallas as pl
from jax.experimental.pallas import tpu as pltpu

F32 = jnp.float32
BF16 = jnp.bfloat16

EPS = 1e-6
ROPE_THETA = 10000.0
NEG = -1e30
PAST_LEN = 16384

HD = 128
N_HEADS = 4
GLA_DV = 256
GLA_RANK = 16
GLA_TAU = 16.0
GLA_CHUNK = 64
GLA_SUB = 16
GLA_UNROLL = 4
SWA_PATTERNS = ((128, 1), (512, 4), (2048, 16))
SWA_BLK = 128
SWA_STEP = 2048
SWA_UNROLL = 4
N_MEM = 256

LANES = 128
VMEM_LIMIT = 60000 * 1024

C_GT = 0
C_GQ, C_GK, C_GV, C_GR = 6144, 6656, 7168, 8192
C_SQ, C_SK, C_SV, C_SR = 9216, 10752, 12288, 13824
C_MQ, C_MR = 14336, 14848
N_MAIN = 15360


def _cparams(sem):
    return pltpu.CompilerParams(dimension_semantics=sem, vmem_limit_bytes=VMEM_LIMIT)


def _dot(a, b):
    return jnp.dot(a, b, preferred_element_type=F32)


def _dot_nt(a, b):
    return lax.dot_general(a, b, (((1,), (1,)), ((), ())), preferred_element_type=F32)


def _dot_tn(a, b):
    return lax.dot_general(a, b, (((0,), (0,)), ((), ())), preferred_element_type=F32)


def _silu(x):
    return x * jax.nn.sigmoid(x)


def _log_sigmoid(x):
    return jnp.minimum(x, 0.0) - jnp.log1p(jnp.exp(-jnp.abs(x)))


KV_ROWS = 2 * N_HEADS
SHIFT_ROWS = tuple((w - 1) * KV_ROWS for w, _ in SWA_PATTERNS)
SHIFT_OFFS = tuple(sum(SHIFT_ROWS[:g]) for g in range(len(SWA_PATTERNS)))


def _cache_shift_step(step, n_steps, ca, nr, co, stage, in_sem, out_sem, row_sem):
    n_grp = len(ca)
    n_batch = ca[0].shape[0]
    assert n_steps >= n_batch
    slot = lax.rem(step, 2)

    def in_copy(g, bb, sl):
        return pltpu.make_async_copy(ca[g].at[bb, pl.ds(KV_ROWS, SHIFT_ROWS[g])],
                                     stage.at[sl, pl.ds(SHIFT_OFFS[g], SHIFT_ROWS[g])], in_sem.at[sl, g])

    def out_copy(g, bb, sl):
        return pltpu.make_async_copy(stage.at[sl, pl.ds(SHIFT_OFFS[g], SHIFT_ROWS[g])],
                                     co[g].at[bb, pl.ds(0, SHIFT_ROWS[g])], out_sem.at[sl, g])

    def row_copy(g, bb, sl):
        return pltpu.make_async_copy(nr[g].at[bb], co[g].at[bb, pl.ds(SHIFT_ROWS[g], KV_ROWS)], row_sem.at[sl, g])

    def finish_writes(bb, sl):
        for g in range(n_grp):
            out_copy(g, bb, sl).wait()
            row_copy(g, bb, sl).wait()

    @pl.when(step == 0)
    def _():
        for g in range(n_grp):
            in_copy(g, 0, 0).start()

    @pl.when((step >= 1) & (step <= n_batch))
    def _():
        finish_writes(step - 1, 1 - slot)

    @pl.when(step + 1 < n_batch)
    def _():
        for g in range(n_grp):
            in_copy(g, step + 1, 1 - slot).start()

    @pl.when(step < n_batch)
    def _():
        for g in range(n_grp):
            in_copy(g, step, slot).wait()
            out_copy(g, step, slot).start()
            row_copy(g, step, slot).start()

    if n_steps == n_batch:
        @pl.when(step == n_steps - 1)
        def _():
            finish_writes(step, slot)


def _norm_matmul_kernel(*refs, has_extra, row_chunk, w_rows_out):
    mm = _dot_nt if w_rows_out else _dot
    if has_extra:
        x_ref, g_ref, w_ref, wx_ref, o_ref, ox_ref, h_ref = refs
    else:
        x_ref, g_ref, w_ref, o_ref, h_ref = refs
    tm = x_ref.shape[0]

    @pl.when(pl.program_id(1) == 0)
    def _():
        def body(c, carry):
            r0 = pl.multiple_of(c * row_chunk, row_chunk)
            x = x_ref[pl.ds(r0, row_chunk), :]
            y = x * lax.rsqrt(jnp.mean(x * x, axis=-1, keepdims=True) + EPS)
            h_ref[pl.ds(r0, row_chunk), :] = (y * g_ref[...]).astype(BF16)
            return carry
        lax.fori_loop(0, tm // row_chunk, body, 0)
        if has_extra:
            ox_ref[...] = mm(h_ref[...], wx_ref[...])

    o_ref[...] = mm(h_ref[...], w_ref[...]).astype(o_ref.dtype)


def _norm_matmul(x, g, w, wx=None, *, w_rows_out=False, out_dtype=F32, tm, tn):
    m, d = x.shape
    n = w.shape[0] if w_rows_out else w.shape[1]
    assert m % tm == 0 and n % tn == 0
    has_extra = wx is not None
    row_chunk = min(tm, 128)
    w_spec = pl.BlockSpec((tn, d), lambda i, j: (j, 0)) if w_rows_out else pl.BlockSpec((d, tn), lambda i, j: (0, j))
    in_specs = [pl.BlockSpec((tm, d), lambda i, j: (i, 0)),
                pl.BlockSpec((1, d), lambda i, j: (0, 0)),
                w_spec]
    out_specs = [pl.BlockSpec((tm, tn), lambda i, j: (i, j))]
    out_shape = [jax.ShapeDtypeStruct((m, n), out_dtype)]
    args = [x, g.reshape(1, d), w]
    if has_extra:
        nx = wx.shape[0] if w_rows_out else wx.shape[1]
        in_specs.append(pl.BlockSpec(wx.shape, lambda i, j: (0, 0)))
        out_specs.append(pl.BlockSpec((tm, nx), lambda i, j: (i, 0)))
        out_shape.append(jax.ShapeDtypeStruct((m, nx), F32))
        args.append(wx)
    res = pl.pallas_call(
        functools.partial(_norm_matmul_kernel, has_extra=has_extra, row_chunk=row_chunk, w_rows_out=w_rows_out),
        grid=(m // tm, n // tn),
        in_specs=in_specs, out_specs=out_specs, out_shape=out_shape,
        scratch_shapes=[pltpu.VMEM((tm, d), BF16)],
        compiler_params=_cparams(("arbitrary", "arbitrary")),
        name="norm_matmul",
    )(*args)
    return res if has_extra else res[0]


def _diag_select_matrix():
    rows = np.arange(GLA_SUB * HD)[:, None] // HD
    cols = np.arange(LANES)[None, :] % GLA_SUB
    return jnp.asarray((rows == cols).astype(np.float32), dtype=BF16)


def _gla_chunks(rows_list, zg_ref, ga_ref, wa_ref, ba_ref, gg_ref, em_ref, st_ref, u_ref):
    C, SUB, n_sub = GLA_CHUNK, GLA_SUB, GLA_CHUNK // GLA_SUB
    kw = N_HEADS * HD
    c_k, c_v, c_r = kw, 2 * kw, 2 * kw + N_HEADS * GLA_DV
    row = lax.broadcasted_iota(jnp.int32, (C, C), 0)
    col = lax.broadcasted_iota(jnp.int32, (C, C), 1)
    tri = (col <= row).astype(F32)
    row_c = lax.broadcasted_iota(jnp.int32, (C, HD), 0)
    sub_row = lax.broadcasted_iota(jnp.int32, (SUB, HD), 0)
    lane_c = lax.broadcasted_iota(jnp.int32, (SUB, C), 1)

    xas = [_dot(ga_ref[rows, :].astype(BF16), wa_ref[...]) + ba_ref[...] for rows in rows_list]
    b_alls = [jnp.dot(tri, _log_sigmoid(xa) / GLA_TAU, preferred_element_type=F32, precision=lax.Precision.HIGHEST)
              for xa in xas]

    chunks = []
    for rows, b_all in zip(rows_list, b_alls):
        a_off = []
        per_head = []
        for h in range(N_HEADS):
            q = zg_ref[rows, h * HD:(h + 1) * HD].astype(F32) * (HD ** -0.5)
            k = zg_ref[rows, c_k + h * HD:c_k + (h + 1) * HD].astype(F32)
            v = zg_ref[rows, c_v + h * GLA_DV:c_v + (h + 1) * GLA_DV]
            b = b_all[:, h * HD:(h + 1) * HD]
            for i in range(n_sub):
                if i == 0:
                    a_off.append(jnp.zeros((SUB, C), F32))
                else:
                    sl = slice(i * SUB, (i + 1) * SUB)
                    b_ref_row = b[i * SUB - 1:i * SUB, :]
                    qs = q[sl] * jnp.exp(b[sl] - b_ref_row)
                    ks = k * jnp.exp(jnp.where(row_c < i * SUB, b_ref_row - b, -jnp.inf))
                    a_off.append(_dot_nt(qs.astype(BF16), ks.astype(BF16)))
            st = st_ref[h]
            o_inter = _dot_nt((q * jnp.exp(b)).astype(BF16), st.astype(BF16))
            b_end = b[C - 1:C, :]
            kd = k * jnp.exp(b_end - b)
            st_ref[h] = st * jnp.exp(b_end) + _dot_tn(v.astype(BF16), kd.astype(BF16))
            per_head.append((q, k, v, b, o_inter))
        chunks.append((rows, a_off, per_head))

    p_rows = []
    for _, _, per_head in chunks:
        for h in range(N_HEADS):
            q, k, _, b, _ = per_head[h]
            for i in range(n_sub):
                sl = slice(i * SUB, (i + 1) * SUB)
                q_i, k_i, b_i = q[sl], k[sl], b[sl]
                slabs = []
                for s in range(SUB):
                    e = jnp.where(sub_row >= s, b_i - b_i[s:s + 1, :], -jnp.inf)
                    slabs.append((q_i * k_i[s:s + 1, :]) * jnp.exp(e))
                p_rows.append(jnp.concatenate(slabs, axis=1).astype(BF16))
    r_all = _dot(jnp.concatenate(p_rows, axis=0), em_ref[...])

    for ci, (rows, a_off, per_head) in enumerate(chunks):
        for h in range(N_HEADS):
            _, _, v, _, o_inter = per_head[h]
            a_rows = []
            for i in range(n_sub):
                idx = h * n_sub + i
                r0 = (ci * N_HEADS * n_sub + idx) * SUB
                in_blk = (lane_c >= i * SUB) & (lane_c < (i + 1) * SUB)
                a_rows.append(a_off[idx] + jnp.where(in_blk, r_all[r0:r0 + SUB, :C], 0.0))
            a = jnp.concatenate(a_rows, axis=0)
            o = _dot(a.astype(BF16), v.astype(BF16)) + o_inter
            y = o * lax.rsqrt(jnp.mean(o * o, axis=-1, keepdims=True) + EPS)
            y = y * gg_ref[:, h * GLA_DV:(h + 1) * GLA_DV]
            gate = zg_ref[rows, c_r + h * GLA_DV:c_r + (h + 1) * GLA_DV].astype(F32)
            u_ref[rows, h * GLA_DV:(h + 1) * GLA_DV] = (y * _silu(gate)).astype(u_ref.dtype)


def _gla_kernel(*refs, n_shift, n_steps):
    zg_ref, ga_ref, wa_ref, ba_ref, gg_ref, em_ref = refs[:6]
    n_in = 6 + 2 * n_shift
    u_ref, sout_ref = refs[n_in:n_in + 2]
    st_ref = refs[n_in + 2 + n_shift]
    t_blk = pl.program_id(1)
    n_chunks = zg_ref.shape[0] // GLA_CHUNK
    assert n_chunks % GLA_UNROLL == 0

    if n_shift:
        stage, in_sem, out_sem, row_sem = refs[n_in + 3 + n_shift:]
        _cache_shift_step(pl.program_id(0) * pl.num_programs(1) + t_blk, n_steps, refs[6:6 + n_shift],
                          refs[6 + n_shift:n_in], refs[n_in + 2:n_in + 2 + n_shift], stage, in_sem, out_sem, row_sem)

    @pl.when(t_blk == 0)
    def _():
        st_ref[...] = jnp.zeros_like(st_ref)

    def chunks(c, carry):
        rows_list = [pl.ds(pl.multiple_of((c * GLA_UNROLL + u) * GLA_CHUNK, GLA_CHUNK), GLA_CHUNK)
                     for u in range(GLA_UNROLL)]
        _gla_chunks(rows_list, zg_ref, ga_ref, wa_ref, ba_ref, gg_ref, em_ref, st_ref, u_ref)
        return carry

    lax.fori_loop(0, n_chunks // GLA_UNROLL, chunks, 0)

    @pl.when(t_blk == pl.num_programs(1) - 1)
    def _():
        for h in range(N_HEADS):
            sout_ref[h] = st_ref[h].T


def _gla_prompt(z, za, wa_pad, b_alpha, g_gla_out, batch, seq, shift=None, *, t_blk):
    nt = seq // t_blk
    assert seq % t_blk == 0 and t_blk % GLA_CHUNK == 0
    kw = N_HEADS * HD
    vw = N_HEADS * GLA_DV
    gla_w = 2 * kw + 2 * vw
    assert C_GQ % gla_w == 0 and (C_GK, C_GV, C_GR) == (C_GQ + kw, C_GQ + 2 * kw, C_GQ + 2 * kw + vw)
    const = lambda b, t: (0, 0)
    in_specs = [pl.BlockSpec((t_blk, gla_w), lambda b, t: (b * nt + t, C_GQ // gla_w)),
                pl.BlockSpec((t_blk, LANES), lambda b, t: (b * nt + t, 0)),
                pl.BlockSpec((LANES, kw), const), pl.BlockSpec((1, kw), const), pl.BlockSpec((1, vw), const),
                pl.BlockSpec((GLA_SUB * HD, LANES), const)]
    out_specs = [pl.BlockSpec((t_blk, vw), lambda b, t: (b * nt + t, 0)),
                 pl.BlockSpec((None, N_HEADS, HD, GLA_DV), lambda b, t: (b, 0, 0, 0))]
    out_shape = [jax.ShapeDtypeStruct((batch * seq, vw), BF16),
                 jax.ShapeDtypeStruct((batch, N_HEADS, HD, GLA_DV), F32)]
    scratch = [pltpu.VMEM((N_HEADS, GLA_DV, HD), F32)]
    args = [z, za, wa_pad, b_alpha.reshape(1, kw), g_gla_out.reshape(1, vw), _diag_select_matrix()]
    n_shift = 0
    if shift is not None:
        caches, new_rows = shift
        n_shift = len(caches)
        any_spec = pl.BlockSpec(memory_space=pl.ANY)
        in_specs += [any_spec] * n_shift + [pl.BlockSpec(r.shape, lambda b, t: (0, 0, 0)) for r in new_rows]
        out_specs += [any_spec] * n_shift
        out_shape += [jax.ShapeDtypeStruct(c.shape, c.dtype) for c in caches]
        args += list(caches) + list(new_rows)
        scratch += [pltpu.VMEM((2, sum(SHIFT_ROWS), HD), F32)] + [pltpu.SemaphoreType.DMA((2, n_shift))] * 3
    return pl.pallas_call(
        functools.partial(_gla_kernel, n_shift=n_shift, n_steps=batch * nt),
        grid=(batch, nt),
        in_specs=in_specs, out_specs=out_specs, out_shape=out_shape, scratch_shapes=scratch,
        compiler_params=_cparams(("arbitrary", "arbitrary")),
        name="gla",
    )(*args)


def _rope_tables(pos):
    half = HD // 2
    inv = ROPE_THETA ** (-jnp.arange(half, dtype=F32) / half)
    ang = pos.astype(F32)[:, None] * inv[None, :]
    cos, sin = jnp.cos(ang), jnp.sin(ang)
    return jnp.concatenate([cos, cos], axis=1), jnp.concatenate([-sin, sin], axis=1)


def _rope(x, cos2, sin2):
    return x * cos2 + pltpu.roll(x, HD // 2, axis=1) * sin2


def _swa_kernel(*refs):
    (q0, q1, q2, k0, k1, k2, v0, v1, v2, sr_ref, cos_ref, sin_ref,
     u_ref, kb0, kb1, kb2, vb0, vb1, vb2,
     q_s, k_s, v_s, o_s, l_s) = refs
    q_in, k_in, v_in = (q0, q1, q2), (k0, k1, k2), (v0, v1, v2)
    kb, vb = (kb0, kb1, kb2), (vb0, vb1, vb2)
    i = pl.program_id(2)
    T = SWA_STEP
    n_grp = len(SWA_PATTERNS)

    @pl.when(i == 0)
    def _():
        for g, (_, dil) in enumerate(SWA_PATTERNS):
            unit = dil * SWA_BLK
            k_s[g, T - unit:T, :] = jnp.zeros((unit, HD), F32)
            v_s[g, T - unit:T, :] = jnp.zeros((unit, HD), F32)

    @pl.when(i > 0)
    def _():
        for g, (_, dil) in enumerate(SWA_PATTERNS):
            unit = dil * SWA_BLK
            k_s[g, T - unit:T, :] = k_s[g, 2 * T - unit:2 * T, :]
            v_s[g, T - unit:T, :] = v_s[g, 2 * T - unit:2 * T, :]

    cos2, sin2 = cos_ref[...], sin_ref[...]
    for g in range(n_grp):
        q_s[g] = _rope(q_in[g][...].astype(F32), cos2, sin2) * (HD ** -0.5)
        k_s[g, T:2 * T, :] = _rope(k_in[g][...].astype(F32), cos2, sin2)
        v_s[g, T:2 * T, :] = v_in[g][...].astype(F32)

    qi = lax.broadcasted_iota(jnp.int32, (SWA_BLK, 2 * SWA_BLK), 0)
    ki = lax.broadcasted_iota(jnp.int32, (SWA_BLK, 2 * SWA_BLK), 1)
    delta = SWA_BLK + qi - ki

    def block(g, n):
        win, dil = SWA_PATTERNS[g]
        unit = dil * SWA_BLK
        u = lax.shift_right_logical(n, dil.bit_length() - 1)
        r = lax.bitwise_and(n, dil - 1)

        def rows(start, size):
            return pl.ds(start, size) if dil == 1 else pl.ds(start, size, stride=dil)

        q = q_s[g, rows(u * unit + r, SWA_BLK), :]
        kk = k_s[g, rows(T + (u - 1) * unit + r, 2 * SWA_BLK), :]
        vv = v_s[g, rows(T + (u - 1) * unit + r, 2 * SWA_BLK), :]
        s = _dot_nt(q.astype(BF16), kk.astype(BF16))
        ki_min = jnp.where((i * (T // unit) + u) == 0, SWA_BLK, 0)
        s = jnp.where((delta >= 0) & (delta <= win // dil) & (ki >= ki_min), s, NEG)
        m = jnp.max(s, axis=-1, keepdims=True)
        p = jnp.exp(s - m)
        den = jnp.sum(p, axis=-1, keepdims=True)
        o = _dot(p.astype(BF16), vv.astype(BF16)) / den
        lse = m + jnp.log(den)
        o_s[g, rows(u * unit + r, SWA_BLK), :] = o
        l_s[g, rows(u * unit + r, SWA_BLK), :] = jnp.broadcast_to(lse, (SWA_BLK, HD))

    def blocks(it, carry):
        for g in range(n_grp):
            for j in range(SWA_UNROLL):
                block(g, it * SWA_UNROLL + j)
        return carry

    lax.fori_loop(0, T // SWA_BLK // SWA_UNROLL, blocks, 0)

    rc = 256

    def merge(c, carry):
        r0 = pl.multiple_of(c * rc, rc)
        rr = pl.ds(r0, rc)
        ls = [l_s[g, rr, :] for g in range(n_grp)]
        mx = functools.reduce(jnp.maximum, ls)
        es = [jnp.exp(l - mx) for l in ls]
        tot = functools.reduce(lambda a, b: a + b, es)
        ob = functools.reduce(lambda a, b: a + b, [(es[g] / tot) * o_s[g, rr, :] for g in range(n_grp)])
        u_ref[rr, :] = (ob * _silu(sr_ref[rr, :].astype(F32))).astype(u_ref.dtype)
        return carry

    lax.fori_loop(0, T // rc, merge, 0)

    @pl.when(i == pl.num_programs(2) - 1)
    def _():
        for g, (win, _) in enumerate(SWA_PATTERNS):
            kb[g][...] = k_s[g, 2 * T - win:2 * T, :]
            vb[g][...] = v_s[g, 2 * T - win:2 * T, :]


def _swa_prompt(z, cos2, sin2, batch, seq):
    T = SWA_STEP
    assert seq % T == 0 and all(w <= T for w, _ in SWA_PATTERNS)
    nt = seq // T
    w_grp = N_HEADS * HD

    def zspec(col0, g):
        cb0 = (col0 + g * w_grp) // HD
        return pl.BlockSpec((T, HD), lambda b, j, i: (b * nt + i, cb0 + j))

    in_specs = ([zspec(C_SQ, g) for g in range(3)] + [zspec(C_SK, g) for g in range(3)]
                + [zspec(C_SV, g) for g in range(3)] + [zspec(C_SR, 0)]
                + [pl.BlockSpec((T, HD), lambda b, j, i: (i, 0))] * 2)
    buf_specs = [pl.BlockSpec((None, w, HD), lambda b, j, i: (b, 0, j)) for w, _ in SWA_PATTERNS]
    buf_shapes = [jax.ShapeDtypeStruct((batch, w, w_grp), F32) for w, _ in SWA_PATTERNS]
    return pl.pallas_call(
        _swa_kernel,
        grid=(batch, N_HEADS, nt),
        in_specs=in_specs,
        out_specs=[pl.BlockSpec((T, HD), lambda b, j, i: (b * nt + i, j))] + buf_specs + buf_specs,
        out_shape=[jax.ShapeDtypeStruct((batch * seq, w_grp), BF16)] + buf_shapes + buf_shapes,
        scratch_shapes=[pltpu.VMEM((3, T, HD), F32), pltpu.VMEM((3, 2 * T, HD), F32),
                        pltpu.VMEM((3, 2 * T, HD), F32), pltpu.VMEM((3, T, HD), F32),
                        pltpu.VMEM((3, T, HD), F32)],
        compiler_params=_cparams(("arbitrary", "arbitrary", "arbitrary")),
        name="swa",
    )(*([z] * 10), cos2, sin2)


def _memattn_kernel(q_ref, mr_ref, kv_ref, u_ref):
    w = N_HEADS * HD
    for h in range(N_HEADS):
        cs = slice(h * HD, (h + 1) * HD)
        q = q_ref[:, cs].astype(F32) * (HD ** -0.5)
        k = kv_ref[:, cs]
        v = kv_ref[:, w + h * HD:w + (h + 1) * HD]
        s = _dot_nt(q.astype(BF16), k.astype(BF16))
        e = jnp.exp(s - jnp.max(s, axis=-1, keepdims=True))
        p = e / jnp.sum(e, axis=-1, keepdims=True)
        o = _dot(p.astype(BF16), v.astype(BF16))
        u_ref[:, cs] = (o * _silu(mr_ref[:, cs].astype(F32))).astype(u_ref.dtype)


def _memattn_prompt(z, mem_kv, batch, seq, *, t_blk):
    nt = seq // t_blk
    w = N_HEADS * HD
    return pl.pallas_call(
        _memattn_kernel,
        grid=(batch, nt),
        in_specs=[pl.BlockSpec((t_blk, w), lambda b, t: (b * nt + t, C_MQ // w)),
                  pl.BlockSpec((t_blk, w), lambda b, t: (b * nt + t, C_MR // w)),
                  pl.BlockSpec((N_MEM, 2 * w), lambda b, t: (b, 0))],
        out_specs=pl.BlockSpec((t_blk, w), lambda b, t: (b * nt + t, 0)),
        out_shape=jax.ShapeDtypeStruct((batch * seq, w), BF16),
        compiler_params=_cparams(("arbitrary", "arbitrary")),
        name="memattn",
    )(z, z, mem_kv)


def _final_kernel(ua_ref, ub_ref, uc_ref, gt_ref, x_ref, wa_ref, wb_ref, wc_ref, wo_ref, gf_ref, y_ref):
    d = x_ref.shape[1]
    ya = _dot(ua_ref[...].astype(BF16), wa_ref[...])
    yb = _dot(ub_ref[...].astype(BF16), wb_ref[...])
    yc = _dot(uc_ref[...].astype(BF16), wc_ref[...])
    mix = (jax.nn.sigmoid(gt_ref[:, 0:d].astype(F32)) * ya + jax.nn.sigmoid(gt_ref[:, d:2 * d].astype(F32)) * yb
           + jax.nn.sigmoid(gt_ref[:, 2 * d:3 * d].astype(F32)) * yc)
    xo = x_ref[...] + _dot(mix.astype(BF16), wo_ref[...])
    y = xo * lax.rsqrt(jnp.mean(xo * xo, axis=-1, keepdims=True) + EPS)
    y_ref[...] = y * gf_ref[...]


def _final(ua, ub, uc, z, x, wa, wb, wc, wo, g_final, *, tm):
    m, d = x.shape
    const = lambda i: (0, 0)
    resident = dict(pipeline_mode=pl.Buffered(1))
    return pl.pallas_call(
        _final_kernel,
        grid=(m // tm,),
        in_specs=[pl.BlockSpec((tm, ua.shape[1]), lambda i: (i, 0)),
                  pl.BlockSpec((tm, ub.shape[1]), lambda i: (i, 0)),
                  pl.BlockSpec((tm, uc.shape[1]), lambda i: (i, 0)),
                  pl.BlockSpec((tm, 3 * d), lambda i: (i, C_GT // (3 * d))),
                  pl.BlockSpec((tm, d), lambda i: (i, 0)),
                  pl.BlockSpec(wa.shape, const, **resident),
                  pl.BlockSpec(wb.shape, const, **resident),
                  pl.BlockSpec(wc.shape, const, **resident),
                  pl.BlockSpec(wo.shape, const, **resident),
                  pl.BlockSpec((1, d), const)],
        out_specs=pl.BlockSpec((tm, d), lambda i: (i, 0)),
        out_shape=jax.ShapeDtypeStruct((m, d), F32),
        compiler_params=_cparams(("arbitrary",)),
        name="final",
    )(ua, ub, uc, z, x, wa, wb, wc, wo, g_final.reshape(1, d))


def _heads_rows(row, col0):
    return jnp.concatenate([row[:, col0 + h * HD:col0 + (h + 1) * HD] for h in range(N_HEADS)], axis=0)


def _decode_attention(q4, kk, vv, k_new=None, v_new=None):
    s = jnp.sum(kk * q4[None], axis=-1, keepdims=True)
    m = jnp.max(s, axis=0)
    if k_new is not None:
        s_new = jnp.sum(k_new * q4, axis=-1, keepdims=True)
        m = jnp.maximum(m, s_new)
    p = jnp.exp(s - m[None])
    den = jnp.sum(p, axis=0)
    acc = jnp.sum(p * vv, axis=0)
    if k_new is not None:
        p_new = jnp.exp(s_new - m)
        den = den + p_new
        acc = acc + p_new * v_new
    return acc / den, m + jnp.log(den)


def _sample_kernel(*refs):
    (z_ref, za_ref, wa_ref, ba_ref, gg_ref, cos_ref, sin_ref, st_ref, cg0, cg1, cg2, cm_ref,
     ua_ref, ub_ref, uc_ref, sout_ref, nr0, nr1, nr2) = refs
    cg, nr = (cg0, cg1, cg2), (nr0, nr1, nr2)
    b = pl.program_id(0)
    zrow = z_ref[pl.ds(b, 1), :]

    ga8 = jnp.broadcast_to(za_ref[pl.ds(b, 1), :], (8, LANES))
    xa = _dot(ga8.astype(BF16), wa_ref[...])[0:1, :] + ba_ref[...]
    a_row = jnp.exp(_log_sigmoid(xa) / GLA_TAU)
    eye = (lax.broadcasted_iota(jnp.int32, (HD, HD), 0) == lax.broadcasted_iota(jnp.int32, (HD, HD), 1))

    def col(row_vec):
        return jnp.sum(jnp.where(eye, jnp.broadcast_to(row_vec, (HD, HD)), 0.0), axis=1, keepdims=True)

    for h in range(N_HEADS):
        q = zrow[:, C_GQ + h * HD:C_GQ + (h + 1) * HD] * (HD ** -0.5)
        k = zrow[:, C_GK + h * HD:C_GK + (h + 1) * HD]
        v = zrow[:, C_GV + h * GLA_DV:C_GV + (h + 1) * GLA_DV]
        s_new = col(a_row[:, h * HD:(h + 1) * HD]) * st_ref[h] + col(k) * v
        sout_ref[h] = s_new
        o = jnp.sum(col(q) * s_new, axis=0, keepdims=True)
        y = o * lax.rsqrt(jnp.mean(o * o, axis=-1, keepdims=True) + EPS)
        y = y * gg_ref[:, h * GLA_DV:(h + 1) * GLA_DV]
        gate = zrow[:, C_GR + h * GLA_DV:C_GR + (h + 1) * GLA_DV]
        ua_ref[:, h * GLA_DV:(h + 1) * GLA_DV] = y * _silu(gate)

    cos2, sin2 = cos_ref[...], sin_ref[...]
    outs, lses = [], []
    w_grp = N_HEADS * HD
    for g in range(3):
        q4 = _rope(_heads_rows(zrow, C_SQ + g * w_grp), cos2, sin2) * (HD ** -0.5)
        k4 = _rope(_heads_rows(zrow, C_SK + g * w_grp), cos2, sin2)
        v4 = _heads_rows(zrow, C_SV + g * w_grp)
        nr[g][0:N_HEADS, :] = k4
        nr[g][N_HEADS:KV_ROWS, :] = v4
        o, lse = _decode_attention(q4, cg[g][:, 0:N_HEADS, :], cg[g][:, N_HEADS:KV_ROWS, :], k4, v4)
        outs.append(o)
        lses.append(lse)
    mx = functools.reduce(jnp.maximum, lses)
    es = [jnp.exp(l - mx) for l in lses]
    tot = es[0] + es[1] + es[2]
    ob = (es[0] / tot) * outs[0] + (es[1] / tot) * outs[1] + (es[2] / tot) * outs[2]
    for h in range(N_HEADS):
        gate = zrow[:, C_SR + h * HD:C_SR + (h + 1) * HD]
        ub_ref[:, h * HD:(h + 1) * HD] = ob[h:h + 1, :] * _silu(gate)

    qm = _heads_rows(zrow, C_MQ) * (HD ** -0.5)
    oc, _ = _decode_attention(qm, cm_ref[:, 0:N_HEADS, :], cm_ref[:, N_HEADS:KV_ROWS, :])
    for h in range(N_HEADS):
        gate = zrow[:, C_MR + h * HD:C_MR + (h + 1) * HD]
        uc_ref[:, h * HD:(h + 1) * HD] = oc[h:h + 1, :] * _silu(gate)


def _sample_mixers(z, za, wa_pad, b_alpha, g_gla_out, cos2, sin2, state, caches, cache_mem):
    db = z.shape[0]
    kw, vw, w = N_HEADS * HD, N_HEADS * GLA_DV, N_HEADS * HD
    const2 = lambda b: (0, 0)
    gathered, gather_specs = [], []
    for c, (win, dil) in zip(caches, SWA_PATTERNS):
        assert c.shape == (1, db, win, 2, N_HEADS, HD) and PAST_LEN >= win
        n_keys = win // dil
        gathered.append(c.reshape(db, n_keys, dil * KV_ROWS, HD))
        gather_specs.append(pl.BlockSpec((None, n_keys, KV_ROWS, HD), lambda b: (b, 0, 0, 0)))
    row3 = lambda n: pl.BlockSpec((None, 1, n), lambda b: (b, 0, 0))
    new_spec = pl.BlockSpec((None, KV_ROWS, HD), lambda b: (b, 0, 0))
    res = pl.pallas_call(
        _sample_kernel,
        grid=(db,),
        in_specs=[pl.BlockSpec(z.shape, const2), pl.BlockSpec(za.shape, const2),
                  pl.BlockSpec(wa_pad.shape, const2), pl.BlockSpec((1, kw), const2),
                  pl.BlockSpec((1, vw), const2), pl.BlockSpec((1, HD), const2), pl.BlockSpec((1, HD), const2),
                  pl.BlockSpec((None, None, N_HEADS, HD, GLA_DV), lambda b: (0, b, 0, 0, 0))]
                 + gather_specs
                 + [pl.BlockSpec((None, N_MEM, KV_ROWS, HD), lambda b: (b, 0, 0, 0))],
        out_specs=[row3(vw), row3(w), row3(w),
                   pl.BlockSpec((None, None, N_HEADS, HD, GLA_DV), lambda b: (0, b, 0, 0, 0))] + [new_spec] * 3,
        out_shape=[jax.ShapeDtypeStruct((db, 1, vw), F32), jax.ShapeDtypeStruct((db, 1, w), F32),
                   jax.ShapeDtypeStruct((db, 1, w), F32), jax.ShapeDtypeStruct(state.shape, F32)]
                  + [jax.ShapeDtypeStruct((db, KV_ROWS, HD), F32)] * 3,
        compiler_params=_cparams(("arbitrary",)),
        name="sample_mixers",
    )(z, za, wa_pad, b_alpha.reshape(1, kw), g_gla_out.reshape(1, vw), cos2, sin2, state,
      *gathered, cache_mem.reshape(db, N_MEM, KV_ROWS, HD))
    ua, ub, uc, s_out = res[:4]
    return ua.reshape(db, vw), ub.reshape(db, w), uc.reshape(db, w), s_out, res[4:]


def _prep_w_in_kernel(wt_ref, wm_ref, wg_ref, buf, ga_buf, sem, ga_sem, *, c_ga, c_gt, rows):
    i = pl.program_id(0)
    n_gt = (wt_ref.shape[0] - c_gt) // rows
    n_lo = c_ga // rows

    def fetch(ii, slot):
        src = jnp.where(ii < n_gt, c_gt + ii * rows,
                        jnp.where(ii < n_gt + n_lo, (ii - n_gt) * rows, c_ga + GLA_RANK + (ii - n_gt - n_lo) * rows))
        return pltpu.make_async_copy(wt_ref.at[pl.ds(pl.multiple_of(src, GLA_RANK), rows)], buf.at[slot], sem.at[slot])

    ga_copy = pltpu.make_async_copy(wt_ref.at[pl.ds(c_ga, GLA_RANK)], ga_buf, ga_sem)

    @pl.when(i == 0)
    def _():
        fetch(0, 0).start()
        ga_copy.start()

    @pl.when(i + 1 < pl.num_programs(0))
    def _():
        fetch(i + 1, lax.rem(i + 1, 2)).start()

    @pl.when(i == 0)
    def _():
        ga_copy.wait()
        wg_ref[0:GLA_RANK, :] = ga_buf[...].astype(BF16)
        wg_ref[GLA_RANK:, :] = jnp.zeros((LANES - GLA_RANK, wg_ref.shape[1]), BF16)

    slot = lax.rem(i, 2)
    fetch(i, slot).wait()
    wm_ref[...] = buf[slot].astype(BF16)


def _prep_w_in(wt, d, *, rows):
    n_all, k = wt.shape
    c_ga = 2 * N_HEADS * HD + 2 * N_HEADS * GLA_DV
    c_gt = n_all - 3 * d
    assert n_all - GLA_RANK == N_MAIN and k == d
    assert (n_all - c_gt) % rows == 0 and c_ga % rows == 0 and (c_gt - c_ga - GLA_RANK) % rows == 0
    return pl.pallas_call(
        functools.partial(_prep_w_in_kernel, c_ga=c_ga, c_gt=c_gt, rows=rows),
        grid=(N_MAIN // rows,),
        in_specs=[pl.BlockSpec(memory_space=pl.ANY)],
        out_specs=[pl.BlockSpec((rows, d), lambda i: (i, 0)), pl.BlockSpec((LANES, d), lambda i: (0, 0))],
        out_shape=[jax.ShapeDtypeStruct((N_MAIN, d), BF16), jax.ShapeDtypeStruct((LANES, d), BF16)],
        scratch_shapes=[pltpu.VMEM((2, rows, d), F32), pltpu.VMEM((GLA_RANK, d), F32),
                        pltpu.SemaphoreType.DMA((2,)), pltpu.SemaphoreType.DMA(())],
        compiler_params=_cparams(("arbitrary",)),
        name="prep_w_in",
    )(wt)


def kernel(x_prompt, x_sample, mem_prompt, state_gla, cache_swa_w128, cache_swa_w512, cache_swa_w2048, cache_mem_kv, g_norm, w_in, w_alpha2, b_alpha, g_gla_out, g_mem, w_mem_kv, w_proj_a, w_proj_b, w_proj_c, w_out, g_final):
    batch, seq, d = x_prompt.shape
    db, dec_seq, _ = x_sample.shape
    assert g_norm.shape[0] == 1 and dec_seq == 1

    w_main, w_ga = _prep_w_in(jnp.swapaxes(w_in[0], 0, 1), d, rows=512)
    wa_pad = jnp.pad(w_alpha2[0], ((0, LANES - GLA_RANK), (0, 0))).astype(BF16)
    wpa, wpb, wpc, wo = (w[0].astype(BF16) for w in (w_proj_a, w_proj_b, w_proj_c, w_out))

    xp = x_prompt.reshape(batch * seq, d)
    xs = x_sample.reshape(db, d)

    zs, zas = _norm_matmul(xs, g_norm[0], w_main, w_ga, w_rows_out=True, tm=db, tn=1024)
    cos_s, sin_s = _rope_tables(jnp.full((1,), PAST_LEN, jnp.int32))
    caches = (cache_swa_w128, cache_swa_w512, cache_swa_w2048)
    uas, ubs, ucs, gla_s, new_rows = _sample_mixers(
        zs, zas, wa_pad, b_alpha[0], g_gla_out[0], cos_s, sin_s, state_gla, caches, cache_mem_kv)
    y_sample = _final(uas, ubs, ucs, zs, xs, wpa, wpb, wpc, wo, g_final, tm=db).reshape(db, 1, d)

    z, za = _norm_matmul(xp, g_norm[0], w_main, w_ga, w_rows_out=True, out_dtype=BF16, tm=1024, tn=1536)
    flat_caches = [c.reshape(db, w * KV_ROWS, HD) for c, (w, _) in zip(caches, SWA_PATTERNS)]
    ua, gla_p, *shifted = _gla_prompt(z, za, wa_pad, b_alpha[0], g_gla_out[0], batch, seq,
                                      (flat_caches, new_rows), t_blk=512)
    swa_s = [o.reshape(c.shape) for o, c in zip(shifted, caches)]
    mem_kv = _norm_matmul(mem_prompt.reshape(batch * N_MEM, d), g_mem[0], w_mem_kv[0].astype(BF16),
                          tm=batch * N_MEM, tn=512)
    cos_p, sin_p = _rope_tables(jnp.arange(seq, dtype=jnp.int32))
    swa_res = _swa_prompt(z, cos_p, sin_p, batch, seq)
    ub, kbufs, vbufs = swa_res[0], swa_res[1:4], swa_res[4:7]
    uc = _memattn_prompt(z, mem_kv, batch, seq, t_blk=512)
    y_prompt = _final(ua, ub, uc, z, xp, wpa, wpb, wpc, wo, g_final, tm=512).reshape(batch, seq, d)

    swa_p = [jnp.stack([k.reshape(batch, w, N_HEADS, HD), v.reshape(batch, w, N_HEADS, HD)], axis=2)[None]
             for k, v, (w, _) in zip(kbufs, vbufs, SWA_PATTERNS)]
    mem_kv_prompt = mem_kv.reshape(1, batch, N_MEM, 2, N_HEADS, HD)
    return (y_prompt, y_sample, gla_p[None], swa_p[0], swa_p[1], swa_p[2], mem_kv_prompt,
            gla_s, swa_s[0], swa_s[1], swa_s[2])
```

```python
import functools

import numpy as np
import jax
import jax.numpy as jnp
from jax import lax
from jax.experimental import pallas as pl
from jax.experimental.pallas import tpu as pltpu

F32 = jnp.float32
BF16 = jnp.bfloat16

EPS = 1e-6
ROPE_THETA = 10000.0
NEG = -1e30
PAST_LEN = 16384

HD = 128
N_HEADS = 4
GLA_DV = 256
GLA_RANK = 16
GLA_TAU = 16.0
LOG2_E = 1.4426950408889634
GLA_CHUNK = 64
GLA_SUB = 16
GLA_UNROLL = 4
SWA_PATTERNS = ((128, 1), (512, 4), (2048, 16))
SWA_BLK = 128
SWA_STEP = 2048
SWA_UNROLL = 2
N_MEM = 256

LANES = 128
VMEM_LIMIT = 60000 * 1024

C_GT = 0
C_GQ, C_GK, C_GV, C_GR = 6144, 6656, 7168, 8192
C_SQ, C_SK, C_SV, C_SR = 9216, 10752, 12288, 13824
C_MQ, C_MR = 14336, 14848
N_MAIN = 15360


def _cparams(sem):
    return pltpu.CompilerParams(dimension_semantics=sem, vmem_limit_bytes=VMEM_LIMIT)


def _dot(a, b):
    return jnp.dot(a, b, preferred_element_type=F32)


def _dot_nt(a, b):
    return lax.dot_general(a, b, (((1,), (1,)), ((), ())), preferred_element_type=F32)


def _dot_tn(a, b):
    return lax.dot_general(a, b, (((0,), (0,)), ((), ())), preferred_element_type=F32)


def _silu(x):
    return x * jax.nn.sigmoid(x)


def _log_sigmoid(x):
    return jnp.minimum(x, 0.0) - jnp.log1p(jnp.exp(-jnp.abs(x)))


KV_ROWS = 2 * N_HEADS
SHIFT_ROWS = tuple((w - 1) * KV_ROWS for w, _ in SWA_PATTERNS)
SHIFT_OFFS = tuple(sum(SHIFT_ROWS[:g]) for g in range(len(SWA_PATTERNS)))


def _cache_shift_step(step, n_steps, ca, nr, co, stage, in_sem, out_sem, row_sem):
    n_grp = len(ca)
    n_batch = ca[0].shape[0]
    assert n_steps >= n_batch
    slot = lax.rem(step, 2)

    def in_copy(g, bb, sl):
        return pltpu.make_async_copy(ca[g].at[bb, pl.ds(KV_ROWS, SHIFT_ROWS[g])],
                                     stage.at[sl, pl.ds(SHIFT_OFFS[g], SHIFT_ROWS[g])], in_sem.at[sl, g])

    def out_copy(g, bb, sl):
        return pltpu.make_async_copy(stage.at[sl, pl.ds(SHIFT_OFFS[g], SHIFT_ROWS[g])],
                                     co[g].at[bb, pl.ds(0, SHIFT_ROWS[g])], out_sem.at[sl, g])

    def row_copy(g, bb, sl):
        return pltpu.make_async_copy(nr[g].at[bb], co[g].at[bb, pl.ds(SHIFT_ROWS[g], KV_ROWS)], row_sem.at[sl, g])

    def finish_writes(bb, sl):
        for g in range(n_grp):
            out_copy(g, bb, sl).wait()
            row_copy(g, bb, sl).wait()

    @pl.when(step == 0)
    def _():
        for g in range(n_grp):
            in_copy(g, 0, 0).start()

    @pl.when((step >= 1) & (step <= n_batch))
    def _():
        finish_writes(step - 1, 1 - slot)

    @pl.when(step + 1 < n_batch)
    def _():
        for g in range(n_grp):
            in_copy(g, step + 1, 1 - slot).start()

    @pl.when(step < n_batch)
    def _():
        for g in range(n_grp):
            in_copy(g, step, slot).wait()
            out_copy(g, step, slot).start()
            row_copy(g, step, slot).start()

    if n_steps == n_batch:
        @pl.when(step == n_steps - 1)
        def _():
            finish_writes(step, slot)


def _norm_matmul_kernel(*refs, has_extra, row_chunk, w_rows_out):
    mm = _dot_nt if w_rows_out else _dot
    if has_extra:
        x_ref, g_ref, w_ref, wx_ref, o_ref, ox_ref, h_ref = refs
    else:
        x_ref, g_ref, w_ref, o_ref, h_ref = refs
    tm = x_ref.shape[0]

    @pl.when(pl.program_id(1) == 0)
    def _():
        def body(c, carry):
            r0 = pl.multiple_of(c * row_chunk, row_chunk)
            x = x_ref[pl.ds(r0, row_chunk), :]
            y = x * lax.rsqrt(jnp.mean(x * x, axis=-1, keepdims=True) + EPS)
            h_ref[pl.ds(r0, row_chunk), :] = (y * g_ref[...]).astype(BF16)
            return carry
        lax.fori_loop(0, tm // row_chunk, body, 0)
        if has_extra:
            ox_ref[...] = mm(h_ref[...], wx_ref[...])

    o_ref[...] = mm(h_ref[...], w_ref[...]).astype(o_ref.dtype)


def _norm_matmul(x, g, w, wx=None, *, w_rows_out=False, out_dtype=F32, tm, tn):
    m, d = x.shape
    n = w.shape[0] if w_rows_out else w.shape[1]
    assert m % tm == 0 and n % tn == 0
    has_extra = wx is not None
    row_chunk = min(tm, 128)
    w_spec = pl.BlockSpec((tn, d), lambda i, j: (j, 0)) if w_rows_out else pl.BlockSpec((d, tn), lambda i, j: (0, j))
    in_specs = [pl.BlockSpec((tm, d), lambda i, j: (i, 0)),
                pl.BlockSpec((1, d), lambda i, j: (0, 0)),
                w_spec]
    out_specs = [pl.BlockSpec((tm, tn), lambda i, j: (i, j))]
    out_shape = [jax.ShapeDtypeStruct((m, n), out_dtype)]
    args = [x, g.reshape(1, d), w]
    if has_extra:
        nx = wx.shape[0] if w_rows_out else wx.shape[1]
        in_specs.append(pl.BlockSpec(wx.shape, lambda i, j: (0, 0)))
        out_specs.append(pl.BlockSpec((tm, nx), lambda i, j: (i, 0)))
        out_shape.append(jax.ShapeDtypeStruct((m, nx), F32))
        args.append(wx)
    res = pl.pallas_call(
        functools.partial(_norm_matmul_kernel, has_extra=has_extra, row_chunk=row_chunk, w_rows_out=w_rows_out),
        grid=(m // tm, n // tn),
        in_specs=in_specs, out_specs=out_specs, out_shape=out_shape,
        scratch_shapes=[pltpu.VMEM((tm, d), BF16)],
        compiler_params=_cparams(("arbitrary", "arbitrary")),
        name="norm_matmul",
    )(*args)
    return res if has_extra else res[0]


def _diag_select_matrix():
    rows = np.arange(GLA_SUB * HD)[:, None] // HD
    cols = np.arange(LANES)[None, :] % GLA_SUB
    return jnp.asarray((rows == cols).astype(np.float32), dtype=BF16)


def _gla_chunks(rows_list, zg_ref, ga_ref, wa_ref, ba_ref, gg_ref, em_ref, st_ref, u_ref):
    C, SUB, n_sub = GLA_CHUNK, GLA_SUB, GLA_CHUNK // GLA_SUB
    kw = N_HEADS * HD
    c_k, c_v, c_r = kw, 2 * kw, 2 * kw + N_HEADS * GLA_DV
    row = lax.broadcasted_iota(jnp.int32, (C, C), 0)
    col = lax.broadcasted_iota(jnp.int32, (C, C), 1)
    tri = (col <= row).astype(F32)
    row_c = lax.broadcasted_iota(jnp.int32, (C, HD), 0)
    sub_row = lax.broadcasted_iota(jnp.int32, (SUB, HD), 0)
    lane_c = lax.broadcasted_iota(jnp.int32, (SUB, C), 1)

    xas = [_dot(ga_ref[rows, :].astype(BF16), wa_ref[...]) + ba_ref[...] for rows in rows_list]
    b_alls = [jnp.dot(tri, _log_sigmoid(xa) * (LOG2_E / GLA_TAU), preferred_element_type=F32,
                      precision=lax.Precision.HIGHEST) for xa in xas]

    chunks = []
    for rows, b_all in zip(rows_list, b_alls):
        a_off = []
        per_head = []
        for h in range(N_HEADS):
            q = zg_ref[rows, h * HD:(h + 1) * HD].astype(F32) * (HD ** -0.5)
            k = zg_ref[rows, c_k + h * HD:c_k + (h + 1) * HD].astype(F32)
            v = zg_ref[rows, c_v + h * GLA_DV:c_v + (h + 1) * GLA_DV]
            b = b_all[:, h * HD:(h + 1) * HD]
            for i in range(n_sub):
                if i == 0:
                    a_off.append(jnp.zeros((SUB, C), F32))
                else:
                    sl = slice(i * SUB, (i + 1) * SUB)
                    b_ref_row = b[i * SUB - 1:i * SUB, :]
                    qs = q[sl] * jnp.exp2(b[sl] - b_ref_row)
                    ks = k * jnp.exp2(jnp.where(row_c < i * SUB, b_ref_row - b, -jnp.inf))
                    a_off.append(_dot_nt(qs.astype(BF16), ks.astype(BF16)))
            st = st_ref[h]
            o_inter = _dot_nt((q * jnp.exp2(b)).astype(BF16), st.astype(BF16))
            b_end = b[C - 1:C, :]
            kd = k * jnp.exp2(b_end - b)
            st_ref[h] = st * jnp.exp2(b_end) + _dot_tn(v.astype(BF16), kd.astype(BF16))
            per_head.append((q, k, v, b, o_inter))
        chunks.append((rows, a_off, per_head))

    p_rows = []
    for _, _, per_head in chunks:
        for h in range(N_HEADS):
            q, k, _, b, _ = per_head[h]
            for i in range(n_sub):
                sl = slice(i * SUB, (i + 1) * SUB)
                q_i, k_i, b_i = q[sl], k[sl], b[sl]
                slabs = []
                for s in range(SUB):
                    e = jnp.where(sub_row >= s, b_i - b_i[s:s + 1, :], -jnp.inf)
                    slabs.append((q_i * k_i[s:s + 1, :]) * jnp.exp2(e))
                p_rows.append(jnp.concatenate(slabs, axis=1).astype(BF16))
    r_all = _dot(jnp.concatenate(p_rows, axis=0), em_ref[...])

    for ci, (rows, a_off, per_head) in enumerate(chunks):
        for h in range(N_HEADS):
            _, _, v, _, o_inter = per_head[h]
            a_rows = []
            for i in range(n_sub):
                idx = h * n_sub + i
                r0 = (ci * N_HEADS * n_sub + idx) * SUB
                in_blk = (lane_c >= i * SUB) & (lane_c < (i + 1) * SUB)
                a_rows.append(a_off[idx] + jnp.where(in_blk, r_all[r0:r0 + SUB, :C], 0.0))
            a = jnp.concatenate(a_rows, axis=0)
            o = _dot(a.astype(BF16), v.astype(BF16)) + o_inter
            y = o * lax.rsqrt(jnp.mean(o * o, axis=-1, keepdims=True) + EPS)
            y = y * gg_ref[:, h * GLA_DV:(h + 1) * GLA_DV]
            gate = zg_ref[rows, c_r + h * GLA_DV:c_r + (h + 1) * GLA_DV].astype(F32)
            u_ref[rows, h * GLA_DV:(h + 1) * GLA_DV] = (y * _silu(gate)).astype(u_ref.dtype)


def _gla_kernel(*refs, n_shift, n_steps):
    zg_ref, ga_ref, wa_ref, ba_ref, gg_ref, em_ref = refs[:6]
    n_in = 6 + 2 * n_shift
    u_ref, sout_ref = refs[n_in:n_in + 2]
    st_ref = refs[n_in + 2 + n_shift]
    t_blk = pl.program_id(1)
    n_chunks = zg_ref.shape[0] // GLA_CHUNK
    assert n_chunks % GLA_UNROLL == 0

    if n_shift:
        stage, in_sem, out_sem, row_sem = refs[n_in + 3 + n_shift:]
        _cache_shift_step(pl.program_id(0) * pl.num_programs(1) + t_blk, n_steps, refs[6:6 + n_shift],
                          refs[6 + n_shift:n_in], refs[n_in + 2:n_in + 2 + n_shift], stage, in_sem, out_sem, row_sem)

    @pl.when(t_blk == 0)
    def _():
        st_ref[...] = jnp.zeros_like(st_ref)

    def chunks(c, carry):
        rows_list = [pl.ds(pl.multiple_of((c * GLA_UNROLL + u) * GLA_CHUNK, GLA_CHUNK), GLA_CHUNK)
                     for u in range(GLA_UNROLL)]
        _gla_chunks(rows_list, zg_ref, ga_ref, wa_ref, ba_ref, gg_ref, em_ref, st_ref, u_ref)
        return carry

    lax.fori_loop(0, n_chunks // GLA_UNROLL, chunks, 0)

    @pl.when(t_blk == pl.num_programs(1) - 1)
    def _():
        for h in range(N_HEADS):
            sout_ref[h] = st_ref[h].T


def _gla_prompt(z, za, wa_pad, b_alpha, g_gla_out, batch, seq, shift=None, *, t_blk):
    nt = seq // t_blk
    assert seq % t_blk == 0 and t_blk % GLA_CHUNK == 0
    kw = N_HEADS * HD
    vw = N_HEADS * GLA_DV
    gla_w = 2 * kw + 2 * vw
    assert C_GQ % gla_w == 0 and (C_GK, C_GV, C_GR) == (C_GQ + kw, C_GQ + 2 * kw, C_GQ + 2 * kw + vw)
    const = lambda b, t: (0, 0)
    in_specs = [pl.BlockSpec((t_blk, gla_w), lambda b, t: (b * nt + t, C_GQ // gla_w)),
                pl.BlockSpec((t_blk, LANES), lambda b, t: (b * nt + t, 0)),
                pl.BlockSpec((LANES, kw), const), pl.BlockSpec((1, kw), const), pl.BlockSpec((1, vw), const),
                pl.BlockSpec((GLA_SUB * HD, LANES), const)]
    out_specs = [pl.BlockSpec((t_blk, vw), lambda b, t: (b * nt + t, 0)),
                 pl.BlockSpec((None, N_HEADS, HD, GLA_DV), lambda b, t: (b, 0, 0, 0))]
    out_shape = [jax.ShapeDtypeStruct((batch * seq, vw), BF16),
                 jax.ShapeDtypeStruct((batch, N_HEADS, HD, GLA_DV), F32)]
    scratch = [pltpu.VMEM((N_HEADS, GLA_DV, HD), F32)]
    args = [z, za, wa_pad, b_alpha.reshape(1, kw), g_gla_out.reshape(1, vw), _diag_select_matrix()]
    n_shift = 0
    if shift is not None:
        caches, new_rows = shift
        n_shift = len(caches)
        any_spec = pl.BlockSpec(memory_space=pl.ANY)
        in_specs += [any_spec] * n_shift + [pl.BlockSpec(r.shape, lambda b, t: (0, 0, 0)) for r in new_rows]
        out_specs += [any_spec] * n_shift
        out_shape += [jax.ShapeDtypeStruct(c.shape, c.dtype) for c in caches]
        args += list(caches) + list(new_rows)
        scratch += [pltpu.VMEM((2, sum(SHIFT_ROWS), HD), F32)] + [pltpu.SemaphoreType.DMA((2, n_shift))] * 3
    return pl.pallas_call(
        functools.partial(_gla_kernel, n_shift=n_shift, n_steps=batch * nt),
        grid=(batch, nt),
        in_specs=in_specs, out_specs=out_specs, out_shape=out_shape, scratch_shapes=scratch,
        compiler_params=_cparams(("arbitrary", "arbitrary")),
        name="gla",
    )(*args)


def _rope_tables(pos):
    half = HD // 2
    inv = ROPE_THETA ** (-jnp.arange(half, dtype=F32) / half)
    ang = pos.astype(F32)[:, None] * inv[None, :]
    cos, sin = jnp.cos(ang), jnp.sin(ang)
    return jnp.concatenate([cos, cos], axis=1), jnp.concatenate([-sin, sin], axis=1)


def _rope(x, cos2, sin2):
    return x * cos2 + pltpu.roll(x, HD // 2, axis=1) * sin2


def _swa_kernel(*refs):
    (q0, q1, q2, k0, k1, k2, v0, v1, v2, sr_ref, cos_ref, sin_ref,
     u_ref, kb0, kb1, kb2, vb0, vb1, vb2,
     q_s, k_s, v_s, o_s, l_s) = refs
    q_in, k_in, v_in = (q0, q1, q2), (k0, k1, k2), (v0, v1, v2)
    kb, vb = (kb0, kb1, kb2), (vb0, vb1, vb2)
    i = pl.program_id(2)
    T = SWA_STEP
    n_grp = len(SWA_PATTERNS)

    @pl.when(i == 0)
    def _():
        for g, (_, dil) in enumerate(SWA_PATTERNS):
            unit = dil * SWA_BLK
            k_s[g, T - unit:T, :] = jnp.zeros((unit, HD), F32)
            v_s[g, T - unit:T, :] = jnp.zeros((unit, HD), F32)

    @pl.when(i > 0)
    def _():
        for g, (_, dil) in enumerate(SWA_PATTERNS):
            unit = dil * SWA_BLK
            k_s[g, T - unit:T, :] = k_s[g, 2 * T - unit:2 * T, :]
            v_s[g, T - unit:T, :] = v_s[g, 2 * T - unit:2 * T, :]

    cos2, sin2 = cos_ref[...], sin_ref[...]
    for g in range(n_grp):
        q_s[g] = _rope(q_in[g][...].astype(F32), cos2, sin2) * (HD ** -0.5)
        k_s[g, T:2 * T, :] = _rope(k_in[g][...].astype(F32), cos2, sin2)
        v_s[g, T:2 * T, :] = v_in[g][...].astype(F32)

    qi = lax.broadcasted_iota(jnp.int32, (SWA_BLK, 2 * SWA_BLK), 0)
    ki = lax.broadcasted_iota(jnp.int32, (SWA_BLK, 2 * SWA_BLK), 1)
    delta = SWA_BLK + qi - ki

    def blocks(it, carry):
        todo = []
        for g, (win, dil) in enumerate(SWA_PATTERNS):
            for j in range(SWA_UNROLL):
                n = it * SWA_UNROLL + j
                unit = dil * SWA_BLK
                u = lax.shift_right_logical(n, dil.bit_length() - 1)
                r = lax.bitwise_and(n, dil - 1)

                def rows(start, size, dil=dil):
                    return pl.ds(start, size) if dil == 1 else pl.ds(start, size, stride=dil)

                q = q_s[g, rows(u * unit + r, SWA_BLK), :]
                kk = k_s[g, rows(T + (u - 1) * unit + r, 2 * SWA_BLK), :]
                s = _dot_nt(q.astype(BF16), kk.astype(BF16))
                ki_min = jnp.where((i * (T // unit) + u) == 0, SWA_BLK, 0)
                valid = (delta >= 0) & (delta <= win // dil) & (ki >= ki_min)
                todo.append((g, rows(u * unit + r, SWA_BLK), rows(T + (u - 1) * unit + r, 2 * SWA_BLK), s, valid))
        soft = []
        for g, q_rows, kv_rows, s, valid in todo:
            s = jnp.where(valid, s, NEG)
            m = jnp.max(s, axis=-1, keepdims=True)
            p = jnp.exp(s - m)
            den = jnp.sum(p, axis=-1, keepdims=True)
            soft.append((g, q_rows, kv_rows, p.astype(BF16), den, m + jnp.log(den)))
        for g, q_rows, kv_rows, p, den, lse in soft:
            o_s[g, q_rows, :] = _dot(p, v_s[g, kv_rows, :].astype(BF16)) / den
            l_s[g, q_rows, :] = jnp.broadcast_to(lse, (SWA_BLK, HD))
        return carry

    lax.fori_loop(0, T // SWA_BLK // SWA_UNROLL, blocks, 0)

    rc = 256

    def merge(c, carry):
        r0 = pl.multiple_of(c * rc, rc)
        rr = pl.ds(r0, rc)
        ls = [l_s[g, rr, :] for g in range(n_grp)]
        mx = functools.reduce(jnp.maximum, ls)
        es = [jnp.exp(l - mx) for l in ls]
        tot = functools.reduce(lambda a, b: a + b, es)
        ob = functools.reduce(lambda a, b: a + b, [(es[g] / tot) * o_s[g, rr, :] for g in range(n_grp)])
        u_ref[rr, :] = (ob * _silu(sr_ref[rr, :].astype(F32))).astype(u_ref.dtype)
        return carry

    lax.fori_loop(0, T // rc, merge, 0)

    @pl.when(i == pl.num_programs(2) - 1)
    def _():
        for g, (win, _) in enumerate(SWA_PATTERNS):
            kb[g][...] = k_s[g, 2 * T - win:2 * T, :]
            vb[g][...] = v_s[g, 2 * T - win:2 * T, :]


def _swa_prompt(z, cos2, sin2, batch, seq):
    T = SWA_STEP
    assert seq % T == 0 and all(w <= T for w, _ in SWA_PATTERNS)
    nt = seq // T
    w_grp = N_HEADS * HD

    def zspec(col0, g):
        cb0 = (col0 + g * w_grp) // HD
        return pl.BlockSpec((T, HD), lambda b, j, i: (b * nt + i, cb0 + j))

    in_specs = ([zspec(C_SQ, g) for g in range(3)] + [zspec(C_SK, g) for g in range(3)]
                + [zspec(C_SV, g) for g in range(3)] + [zspec(C_SR, 0)]
                + [pl.BlockSpec((T, HD), lambda b, j, i: (i, 0))] * 2)
    buf_specs = [pl.BlockSpec((None, w, HD), lambda b, j, i: (b, 0, j)) for w, _ in SWA_PATTERNS]
    buf_shapes = [jax.ShapeDtypeStruct((batch, w, w_grp), F32) for w, _ in SWA_PATTERNS]
    return pl.pallas_call(
        _swa_kernel,
        grid=(batch, N_HEADS, nt),
        in_specs=in_specs,
        out_specs=[pl.BlockSpec((T, HD), lambda b, j, i: (b * nt + i, j))] + buf_specs + buf_specs,
        out_shape=[jax.ShapeDtypeStruct((batch * seq, w_grp), BF16)] + buf_shapes + buf_shapes,
        scratch_shapes=[pltpu.VMEM((3, T, HD), F32), pltpu.VMEM((3, 2 * T, HD), F32),
                        pltpu.VMEM((3, 2 * T, HD), F32), pltpu.VMEM((3, T, HD), F32),
                        pltpu.VMEM((3, T, HD), F32)],
        compiler_params=_cparams(("arbitrary", "arbitrary", "arbitrary")),
        name="swa",
    )(*([z] * 10), cos2, sin2)


def _memattn_kernel(q_ref, mr_ref, kv_ref, u_ref):
    w = N_HEADS * HD
    for h in range(N_HEADS):
        cs = slice(h * HD, (h + 1) * HD)
        q = q_ref[:, cs].astype(F32) * (HD ** -0.5)
        k = kv_ref[:, cs]
        v = kv_ref[:, w + h * HD:w + (h + 1) * HD]
        s = _dot_nt(q.astype(BF16), k.astype(BF16))
        e = jnp.exp(s - jnp.max(s, axis=-1, keepdims=True))
        p = e / jnp.sum(e, axis=-1, keepdims=True)
        o = _dot(p.astype(BF16), v.astype(BF16))
        u_ref[:, cs] = (o * _silu(mr_ref[:, cs].astype(F32))).astype(u_ref.dtype)


def _memattn_prompt(z, mem_kv, batch, seq, *, t_blk):
    nt = seq // t_blk
    w = N_HEADS * HD
    return pl.pallas_call(
        _memattn_kernel,
        grid=(batch, nt),
        in_specs=[pl.BlockSpec((t_blk, w), lambda b, t: (b * nt + t, C_MQ // w)),
                  pl.BlockSpec((t_blk, w), lambda b, t: (b * nt + t, C_MR // w)),
                  pl.BlockSpec((N_MEM, 2 * w), lambda b, t: (b, 0))],
        out_specs=pl.BlockSpec((t_blk, w), lambda b, t: (b * nt + t, 0)),
        out_shape=jax.ShapeDtypeStruct((batch * seq, w), BF16),
        compiler_params=_cparams(("arbitrary", "arbitrary")),
        name="memattn",
    )(z, z, mem_kv)


def _final_kernel(ua_ref, ub_ref, uc_ref, gt_ref, x_ref, wa_ref, wb_ref, wc_ref, wo_ref, gf_ref, y_ref):
    d = x_ref.shape[1]
    ya = _dot(ua_ref[...].astype(BF16), wa_ref[...])
    yb = _dot(ub_ref[...].astype(BF16), wb_ref[...])
    yc = _dot(uc_ref[...].astype(BF16), wc_ref[...])
    mix = (jax.nn.sigmoid(gt_ref[:, 0:d].astype(F32)) * ya + jax.nn.sigmoid(gt_ref[:, d:2 * d].astype(F32)) * yb
           + jax.nn.sigmoid(gt_ref[:, 2 * d:3 * d].astype(F32)) * yc)
    xo = x_ref[...] + _dot(mix.astype(BF16), wo_ref[...])
    y = xo * lax.rsqrt(jnp.mean(xo * xo, axis=-1, keepdims=True) + EPS)
    y_ref[...] = y * gf_ref[...]


def _final(ua, ub, uc, z, x, wa, wb, wc, wo, g_final, *, tm):
    m, d = x.shape
    const = lambda i: (0, 0)
    resident = dict(pipeline_mode=pl.Buffered(1))
    return pl.pallas_call(
        _final_kernel,
        grid=(m // tm,),
        in_specs=[pl.BlockSpec((tm, ua.shape[1]), lambda i: (i, 0)),
                  pl.BlockSpec((tm, ub.shape[1]), lambda i: (i, 0)),
                  pl.BlockSpec((tm, uc.shape[1]), lambda i: (i, 0)),
                  pl.BlockSpec((tm, 3 * d), lambda i: (i, C_GT // (3 * d))),
                  pl.BlockSpec((tm, d), lambda i: (i, 0)),
                  pl.BlockSpec(wa.shape, const, **resident),
                  pl.BlockSpec(wb.shape, const, **resident),
                  pl.BlockSpec(wc.shape, const, **resident),
                  pl.BlockSpec(wo.shape, const, **resident),
                  pl.BlockSpec((1, d), const)],
        out_specs=pl.BlockSpec((tm, d), lambda i: (i, 0)),
        out_shape=jax.ShapeDtypeStruct((m, d), F32),
        compiler_params=_cparams(("arbitrary",)),
        name="final",
    )(ua, ub, uc, z, x, wa, wb, wc, wo, g_final.reshape(1, d))


def _heads_rows(row, col0):
    return jnp.concatenate([row[:, col0 + h * HD:col0 + (h + 1) * HD] for h in range(N_HEADS)], axis=0)


def _decode_attention(q4, kk, vv, k_new=None, v_new=None):
    s = jnp.sum(kk * q4[None], axis=-1, keepdims=True)
    m = jnp.max(s, axis=0)
    if k_new is not None:
        s_new = jnp.sum(k_new * q4, axis=-1, keepdims=True)
        m = jnp.maximum(m, s_new)
    p = jnp.exp(s - m[None])
    den = jnp.sum(p, axis=0)
    acc = jnp.sum(p * vv, axis=0)
    if k_new is not None:
        p_new = jnp.exp(s_new - m)
        den = den + p_new
        acc = acc + p_new * v_new
    return acc / den, m + jnp.log(den)


def _sample_kernel(*refs):
    (z_ref, za_ref, wa_ref, ba_ref, gg_ref, cos_ref, sin_ref, st_ref, cg0, cg1, cg2, cm_ref,
     ua_ref, ub_ref, uc_ref, sout_ref, nr0, nr1, nr2) = refs
    cg, nr = (cg0, cg1, cg2), (nr0, nr1, nr2)
    b = pl.program_id(0)
    zrow = z_ref[pl.ds(b, 1), :]

    ga8 = jnp.broadcast_to(za_ref[pl.ds(b, 1), :], (8, LANES))
    xa = _dot(ga8.astype(BF16), wa_ref[...])[0:1, :] + ba_ref[...]
    a_row = jnp.exp(_log_sigmoid(xa) / GLA_TAU)
    eye = (lax.broadcasted_iota(jnp.int32, (HD, HD), 0) == lax.broadcasted_iota(jnp.int32, (HD, HD), 1))

    def col(row_vec):
        return jnp.sum(jnp.where(eye, jnp.broadcast_to(row_vec, (HD, HD)), 0.0), axis=1, keepdims=True)

    for h in range(N_HEADS):
        q = zrow[:, C_GQ + h * HD:C_GQ + (h + 1) * HD] * (HD ** -0.5)
        k = zrow[:, C_GK + h * HD:C_GK + (h + 1) * HD]
        v = zrow[:, C_GV + h * GLA_DV:C_GV + (h + 1) * GLA_DV]
        s_new = col(a_row[:, h * HD:(h + 1) * HD]) * st_ref[h] + col(k) * v
        sout_ref[h] = s_new
        o = jnp.sum(col(q) * s_new, axis=0, keepdims=True)
        y = o * lax.rsqrt(jnp.mean(o * o, axis=-1, keepdims=True) + EPS)
        y = y * gg_ref[:, h * GLA_DV:(h + 1) * GLA_DV]
        gate = zrow[:, C_GR + h * GLA_DV:C_GR + (h + 1) * GLA_DV]
        ua_ref[:, h * GLA_DV:(h + 1) * GLA_DV] = y * _silu(gate)

    cos2, sin2 = cos_ref[...], sin_ref[...]
    outs, lses = [], []
    w_grp = N_HEADS * HD
    for g in range(3):
        q4 = _rope(_heads_rows(zrow, C_SQ + g * w_grp), cos2, sin2) * (HD ** -0.5)
        k4 = _rope(_heads_rows(zrow, C_SK + g * w_grp), cos2, sin2)
        v4 = _heads_rows(zrow, C_SV + g * w_grp)
        nr[g][0:N_HEADS, :] = k4
        nr[g][N_HEADS:KV_ROWS, :] = v4
        o, lse = _decode_attention(q4, cg[g][:, 0:N_HEADS, :], cg[g][:, N_HEADS:KV_ROWS, :], k4, v4)
        outs.append(o)
        lses.append(lse)
    mx = functools.reduce(jnp.maximum, lses)
    es = [jnp.exp(l - mx) for l in lses]
    tot = es[0] + es[1] + es[2]
    ob = (es[0] / tot) * outs[0] + (es[1] / tot) * outs[1] + (es[2] / tot) * outs[2]
    for h in range(N_HEADS):
        gate = zrow[:, C_SR + h * HD:C_SR + (h + 1) * HD]
        ub_ref[:, h * HD:(h + 1) * HD] = ob[h:h + 1, :] * _silu(gate)

    qm = _heads_rows(zrow, C_MQ) * (HD ** -0.5)
    oc, _ = _decode_attention(qm, cm_ref[:, 0:N_HEADS, :], cm_ref[:, N_HEADS:KV_ROWS, :])
    for h in range(N_HEADS):
        gate = zrow[:, C_MR + h * HD:C_MR + (h + 1) * HD]
        uc_ref[:, h * HD:(h + 1) * HD] = oc[h:h + 1, :] * _silu(gate)


def _sample_mixers(z, za, wa_pad, b_alpha, g_gla_out, cos2, sin2, state, caches, cache_mem):
    db = z.shape[0]
    kw, vw, w = N_HEADS * HD, N_HEADS * GLA_DV, N_HEADS * HD
    const2 = lambda b: (0, 0)
    gathered, gather_specs = [], []
    for c, (win, dil) in zip(caches, SWA_PATTERNS):
        assert c.shape == (1, db, win, 2, N_HEADS, HD) and PAST_LEN >= win
        n_keys = win // dil
        gathered.append(c.reshape(db, n_keys, dil * KV_ROWS, HD))
        gather_specs.append(pl.BlockSpec((None, n_keys, KV_ROWS, HD), lambda b: (b, 0, 0, 0)))
    row3 = lambda n: pl.BlockSpec((None, 1, n), lambda b: (b, 0, 0))
    new_spec = pl.BlockSpec((None, KV_ROWS, HD), lambda b: (b, 0, 0))
    res = pl.pallas_call(
        _sample_kernel,
        grid=(db,),
        in_specs=[pl.BlockSpec(z.shape, const2), pl.BlockSpec(za.shape, const2),
                  pl.BlockSpec(wa_pad.shape, const2), pl.BlockSpec((1, kw), const2),
                  pl.BlockSpec((1, vw), const2), pl.BlockSpec((1, HD), const2), pl.BlockSpec((1, HD), const2),
                  pl.BlockSpec((None, None, N_HEADS, HD, GLA_DV), lambda b: (0, b, 0, 0, 0))]
                 + gather_specs
                 + [pl.BlockSpec((None, N_MEM, KV_ROWS, HD), lambda b: (b, 0, 0, 0))],
        out_specs=[row3(vw), row3(w), row3(w),
                   pl.BlockSpec((None, None, N_HEADS, HD, GLA_DV), lambda b: (0, b, 0, 0, 0))] + [new_spec] * 3,
        out_shape=[jax.ShapeDtypeStruct((db, 1, vw), F32), jax.ShapeDtypeStruct((db, 1, w), F32),
                   jax.ShapeDtypeStruct((db, 1, w), F32), jax.ShapeDtypeStruct(state.shape, F32)]
                  + [jax.ShapeDtypeStruct((db, KV_ROWS, HD), F32)] * 3,
        compiler_params=_cparams(("arbitrary",)),
        name="sample_mixers",
    )(z, za, wa_pad, b_alpha.reshape(1, kw), g_gla_out.reshape(1, vw), cos2, sin2, state,
      *gathered, cache_mem.reshape(db, N_MEM, KV_ROWS, HD))
    ua, ub, uc, s_out = res[:4]
    return ua.reshape(db, vw), ub.reshape(db, w), uc.reshape(db, w), s_out, res[4:]


def _prep_w_in_kernel(wt_ref, wm_ref, wg_ref, buf, ga_buf, sem, ga_sem, *, c_ga, c_gt, rows):
    i = pl.program_id(0)
    n_gt = (wt_ref.shape[0] - c_gt) // rows
    n_lo = c_ga // rows

    def fetch(ii, slot):
        src = jnp.where(ii < n_gt, c_gt + ii * rows,
                        jnp.where(ii < n_gt + n_lo, (ii - n_gt) * rows, c_ga + GLA_RANK + (ii - n_gt - n_lo) * rows))
        return pltpu.make_async_copy(wt_ref.at[pl.ds(pl.multiple_of(src, GLA_RANK), rows)], buf.at[slot], sem.at[slot])

    ga_copy = pltpu.make_async_copy(wt_ref.at[pl.ds(c_ga, GLA_RANK)], ga_buf, ga_sem)

    @pl.when(i == 0)
    def _():
        fetch(0, 0).start()
        ga_copy.start()

    @pl.when(i + 1 < pl.num_programs(0))
    def _():
        fetch(i + 1, lax.rem(i + 1, 2)).start()

    @pl.when(i == 0)
    def _():
        ga_copy.wait()
        wg_ref[0:GLA_RANK, :] = ga_buf[...].astype(BF16)
        wg_ref[GLA_RANK:, :] = jnp.zeros((LANES - GLA_RANK, wg_ref.shape[1]), BF16)

    slot = lax.rem(i, 2)
    fetch(i, slot).wait()
    wm_ref[...] = buf[slot].astype(BF16)


def _prep_w_in(wt, d, *, rows):
    n_all, k = wt.shape
    c_ga = 2 * N_HEADS * HD + 2 * N_HEADS * GLA_DV
    c_gt = n_all - 3 * d
    assert n_all - GLA_RANK == N_MAIN and k == d
    assert (n_all - c_gt) % rows == 0 and c_ga % rows == 0 and (c_gt - c_ga - GLA_RANK) % rows == 0
    return pl.pallas_call(
        functools.partial(_prep_w_in_kernel, c_ga=c_ga, c_gt=c_gt, rows=rows),
        grid=(N_MAIN // rows,),
        in_specs=[pl.BlockSpec(memory_space=pl.ANY)],
        out_specs=[pl.BlockSpec((rows, d), lambda i: (i, 0)), pl.BlockSpec((LANES, d), lambda i: (0, 0))],
        out_shape=[jax.ShapeDtypeStruct((N_MAIN, d), BF16), jax.ShapeDtypeStruct((LANES, d), BF16)],
        scratch_shapes=[pltpu.VMEM((2, rows, d), F32), pltpu.VMEM((GLA_RANK, d), F32),
                        pltpu.SemaphoreType.DMA((2,)), pltpu.SemaphoreType.DMA(())],
        compiler_params=_cparams(("arbitrary",)),
        name="prep_w_in",
    )(wt)


def kernel(x_prompt, x_sample, mem_prompt, state_gla, cache_swa_w128, cache_swa_w512, cache_swa_w2048, cache_mem_kv, g_norm, w_in, w_alpha2, b_alpha, g_gla_out, g_mem, w_mem_kv, w_proj_a, w_proj_b, w_proj_c, w_out, g_final):
    batch, seq, d = x_prompt.shape
    db, dec_seq, _ = x_sample.shape
    assert g_norm.shape[0] == 1 and dec_seq == 1

    w_main, w_ga = _prep_w_in(jnp.swapaxes(w_in[0], 0, 1), d, rows=512)
    wa_pad = jnp.pad(w_alpha2[0], ((0, LANES - GLA_RANK), (0, 0))).astype(BF16)
    wpa, wpb, wpc, wo = (w[0].astype(BF16) for w in (w_proj_a, w_proj_b, w_proj_c, w_out))

    xp = x_prompt.reshape(batch * seq, d)
    xs = x_sample.reshape(db, d)

    zs, zas = _norm_matmul(xs, g_norm[0], w_main, w_ga, w_rows_out=True, tm=db, tn=1024)
    cos_s, sin_s = _rope_tables(jnp.full((1,), PAST_LEN, jnp.int32))
    caches = (cache_swa_w128, cache_swa_w512, cache_swa_w2048)
    uas, ubs, ucs, gla_s, new_rows = _sample_mixers(
        zs, zas, wa_pad, b_alpha[0], g_gla_out[0], cos_s, sin_s, state_gla, caches, cache_mem_kv)
    y_sample = _final(uas, ubs, ucs, zs, xs, wpa, wpb, wpc, wo, g_final, tm=db).reshape(db, 1, d)

    z, za = _norm_matmul(xp, g_norm[0], w_main, w_ga, w_rows_out=True, out_dtype=BF16, tm=1024, tn=2560)
    flat_caches = [c.reshape(db, w * KV_ROWS, HD) for c, (w, _) in zip(caches, SWA_PATTERNS)]
    ua, gla_p, *shifted = _gla_prompt(z, za, wa_pad, b_alpha[0], g_gla_out[0], batch, seq,
                                      (flat_caches, new_rows), t_blk=512)
    swa_s = [o.reshape(c.shape) for o, c in zip(shifted, caches)]
    mem_kv = _norm_matmul(mem_prompt.reshape(batch * N_MEM, d), g_mem[0], w_mem_kv[0].astype(BF16),
                          tm=batch * N_MEM, tn=512)
    cos_p, sin_p = _rope_tables(jnp.arange(seq, dtype=jnp.int32))
    swa_res = _swa_prompt(z, cos_p, sin_p, batch, seq)
    ub, kbufs, vbufs = swa_res[0], swa_res[1:4], swa_res[4:7]
    uc = _memattn_prompt(z, mem_kv, batch, seq, t_blk=512)
    y_prompt = _final(ua, ub, uc, z, xp, wpa, wpb, wpc, wo, g_final, tm=512).reshape(batch, seq, d)

    swa_p = [jnp.stack([k.reshape(batch, w, N_HEADS, HD), v.reshape(batch, w, N_HEADS, HD)], axis=2)[None]
             for k, v, (w, _) in zip(kbufs, vbufs, SWA_PATTERNS)]
    mem_kv_prompt = mem_kv.reshape(1, batch, N_MEM, 2, N_HEADS, HD)
    return (y_prompt, y_sample, gla_p[None], swa_p[0], swa_p[1], swa_p[2], mem_kv_prompt,
            gla_s, swa_s[0], swa_s[1], swa_s[2])
```

```python
import functools

import numpy as np
import jax
import jax.numpy as jnp
from jax import lax
from jax.experimental import pallas as pl
from jax.experimental.pallas import tpu as pltpu

F32 = jnp.float32
BF16 = jnp.bfloat16

EPS = 1e-6
ROPE_THETA = 10000.0
NEG = -1e30
PAST_LEN = 16384

HD = 128
N_HEADS = 4
GLA_DV = 256
GLA_RANK = 16
GLA_TAU = 16.0
LOG2_E = 1.4426950408889634
GLA_CHUNK = 64
GLA_SUB = 16
GLA_UNROLL = 4
SWA_PATTERNS = ((128, 1), (512, 4), (2048, 16))
SWA_BLK = 128
SWA_STEP = 2048
SWA_UNROLL = 2
N_MEM = 256

LANES = 128
VMEM_LIMIT = 60000 * 1024

C_GT = 0
C_GQ, C_GK, C_GV, C_GR = 6144, 6656, 7168, 8192
C_SQ, C_SK, C_SV, C_SR = 9216, 10752, 12288, 13824
C_MQ, C_MR = 14336, 14848
N_MAIN = 15360


def _cparams(sem):
    return pltpu.CompilerParams(dimension_semantics=sem, vmem_limit_bytes=VMEM_LIMIT)


def _dot(a, b):
    return jnp.dot(a, b, preferred_element_type=F32)


def _dot_nt(a, b):
    return lax.dot_general(a, b, (((1,), (1,)), ((), ())), preferred_element_type=F32)


def _dot_tn(a, b):
    return lax.dot_general(a, b, (((0,), (0,)), ((), ())), preferred_element_type=F32)


def _silu(x):
    return x * jax.nn.sigmoid(x)


def _log_sigmoid(x):
    return jnp.minimum(x, 0.0) - jnp.log1p(jnp.exp(-jnp.abs(x)))


KV_ROWS = 2 * N_HEADS
SHIFT_ROWS = tuple((w - 1) * KV_ROWS for w, _ in SWA_PATTERNS)
SHIFT_OFFS = tuple(sum(SHIFT_ROWS[:g]) for g in range(len(SWA_PATTERNS)))


def _cache_shift_step(step, n_steps, ca, nr, co, stage, in_sem, out_sem, row_sem):
    n_grp = len(ca)
    n_batch = ca[0].shape[0]
    assert n_steps >= n_batch
    slot = lax.rem(step, 2)

    def in_copy(g, bb, sl):
        return pltpu.make_async_copy(ca[g].at[bb, pl.ds(KV_ROWS, SHIFT_ROWS[g])],
                                     stage.at[sl, pl.ds(SHIFT_OFFS[g], SHIFT_ROWS[g])], in_sem.at[sl, g])

    def out_copy(g, bb, sl):
        return pltpu.make_async_copy(stage.at[sl, pl.ds(SHIFT_OFFS[g], SHIFT_ROWS[g])],
                                     co[g].at[bb, pl.ds(0, SHIFT_ROWS[g])], out_sem.at[sl, g])

    def row_copy(g, bb, sl):
        return pltpu.make_async_copy(nr[g].at[bb], co[g].at[bb, pl.ds(SHIFT_ROWS[g], KV_ROWS)], row_sem.at[sl, g])

    def finish_writes(bb, sl):
        for g in range(n_grp):
            out_copy(g, bb, sl).wait()
            row_copy(g, bb, sl).wait()

    @pl.when(step == 0)
    def _():
        for g in range(n_grp):
            in_copy(g, 0, 0).start()

    @pl.when((step >= 1) & (step <= n_batch))
    def _():
        finish_writes(step - 1, 1 - slot)

    @pl.when(step + 1 < n_batch)
    def _():
        for g in range(n_grp):
            in_copy(g, step + 1, 1 - slot).start()

    @pl.when(step < n_batch)
    def _():
        for g in range(n_grp):
            in_copy(g, step, slot).wait()
            out_copy(g, step, slot).start()
            row_copy(g, step, slot).start()

    if n_steps == n_batch:
        @pl.when(step == n_steps - 1)
        def _():
            finish_writes(step, slot)


def _norm_matmul_kernel(*refs, has_extra, row_chunk, w_rows_out):
    mm = _dot_nt if w_rows_out else _dot
    if has_extra:
        x_ref, g_ref, w_ref, wx_ref, o_ref, ox_ref, h_ref = refs
    else:
        x_ref, g_ref, w_ref, o_ref, h_ref = refs
    tm = x_ref.shape[0]

    @pl.when(pl.program_id(1) == 0)
    def _():
        for c in range(tm // row_chunk):
            rs = slice(c * row_chunk, (c + 1) * row_chunk)
            x = x_ref[rs, :]
            y = x * lax.rsqrt(jnp.mean(x * x, axis=-1, keepdims=True) + EPS)
            h_ref[rs, :] = (y * g_ref[...]).astype(BF16)
            if has_extra:
                ox_ref[rs, :] = mm(h_ref[rs, :], wx_ref[...])

    o_ref[...] = mm(h_ref[...], w_ref[...]).astype(o_ref.dtype)


def _norm_matmul(x, g, w, wx=None, *, w_rows_out=False, out_dtype=F32, tm, tn):
    m, d = x.shape
    n = w.shape[0] if w_rows_out else w.shape[1]
    assert m % tm == 0 and n % tn == 0
    has_extra = wx is not None
    row_chunk = min(tm, 128)
    w_spec = pl.BlockSpec((tn, d), lambda i, j: (j, 0)) if w_rows_out else pl.BlockSpec((d, tn), lambda i, j: (0, j))
    in_specs = [pl.BlockSpec((tm, d), lambda i, j: (i, 0)),
                pl.BlockSpec((1, d), lambda i, j: (0, 0)),
                w_spec]
    out_specs = [pl.BlockSpec((tm, tn), lambda i, j: (i, j))]
    out_shape = [jax.ShapeDtypeStruct((m, n), out_dtype)]
    args = [x, g.reshape(1, d), w]
    if has_extra:
        nx = wx.shape[0] if w_rows_out else wx.shape[1]
        in_specs.append(pl.BlockSpec(wx.shape, lambda i, j: (0, 0)))
        out_specs.append(pl.BlockSpec((tm, nx), lambda i, j: (i, 0)))
        out_shape.append(jax.ShapeDtypeStruct((m, nx), F32))
        args.append(wx)
    res = pl.pallas_call(
        functools.partial(_norm_matmul_kernel, has_extra=has_extra, row_chunk=row_chunk, w_rows_out=w_rows_out),
        grid=(m // tm, n // tn),
        in_specs=in_specs, out_specs=out_specs, out_shape=out_shape,
        scratch_shapes=[pltpu.VMEM((tm, d), BF16)],
        compiler_params=_cparams(("arbitrary", "arbitrary")),
        name="norm_matmul",
    )(*args)
    return res if has_extra else res[0]


def _diag_select_matrix():
    rows = np.arange(GLA_SUB * HD)[:, None] // HD
    cols = np.arange(LANES)[None, :] % GLA_SUB
    return jnp.asarray((rows == cols).astype(np.float32), dtype=BF16)


def _gla_chunks(rows_list, zg_ref, ga_ref, wa_ref, ba_ref, gg_ref, em_ref, st_ref, u_ref):
    C, SUB, n_sub = GLA_CHUNK, GLA_SUB, GLA_CHUNK // GLA_SUB
    kw = N_HEADS * HD
    c_k, c_v, c_r = kw, 2 * kw, 2 * kw + N_HEADS * GLA_DV
    row = lax.broadcasted_iota(jnp.int32, (C, C), 0)
    col = lax.broadcasted_iota(jnp.int32, (C, C), 1)
    tri = (col <= row).astype(F32)
    row_c = lax.broadcasted_iota(jnp.int32, (C, HD), 0)
    half = SUB // 2
    half_row = lax.broadcasted_iota(jnp.int32, (half, HD), 0)
    lane_c = lax.broadcasted_iota(jnp.int32, (SUB, C), 1)

    xas = [_dot(ga_ref[rows, :].astype(BF16), wa_ref[...]) + ba_ref[...] for rows in rows_list]
    b_alls = [jnp.dot(tri, _log_sigmoid(xa) * (LOG2_E / GLA_TAU), preferred_element_type=F32,
                      precision=lax.Precision.HIGHEST) for xa in xas]

    chunks = []
    for rows, b_all in zip(rows_list, b_alls):
        a_off = []
        per_head = []
        for h in range(N_HEADS):
            q = zg_ref[rows, h * HD:(h + 1) * HD].astype(F32) * (HD ** -0.5)
            k = zg_ref[rows, c_k + h * HD:c_k + (h + 1) * HD].astype(F32)
            v = zg_ref[rows, c_v + h * GLA_DV:c_v + (h + 1) * GLA_DV]
            b = b_all[:, h * HD:(h + 1) * HD]
            for i in range(n_sub):
                if i == 0:
                    a_off.append(jnp.zeros((SUB, C), F32))
                else:
                    sl = slice(i * SUB, (i + 1) * SUB)
                    b_ref_row = b[i * SUB - 1:i * SUB, :]
                    qs = q[sl] * jnp.exp2(b[sl] - b_ref_row)
                    ks = k * jnp.exp2(jnp.where(row_c < i * SUB, b_ref_row - b, -jnp.inf))
                    a_off.append(_dot_nt(qs.astype(BF16), ks.astype(BF16)))
            st = st_ref[h]
            o_inter = _dot_nt((q * jnp.exp2(b)).astype(BF16), st.astype(BF16))
            b_end = b[C - 1:C, :]
            kd = k * jnp.exp2(b_end - b)
            st_ref[h] = st * jnp.exp2(b_end) + _dot_tn(v.astype(BF16), kd.astype(BF16))
            per_head.append((q, k, v, b, o_inter))
        chunks.append((rows, a_off, per_head))

    p_rows = []
    for _, _, per_head in chunks:
        for h in range(N_HEADS):
            q, k, _, b, _ = per_head[h]
            for i in range(n_sub):
                top, bot = slice(i * SUB, i * SUB + half), slice(i * SUB + half, (i + 1) * SUB)
                b_mid = b[i * SUB + half - 1:i * SUB + half, :]
                qs_bot = q[bot] * jnp.exp2(b[bot] - b_mid)
                ks_top = k[top] * jnp.exp2(b_mid - b[top])
                slabs = []
                for s in range(SUB):
                    rs = top if s < half else bot
                    s0 = s % half
                    e = jnp.where(half_row >= s0, b[rs] - b[rs][s0:s0 + 1, :], -jnp.inf)
                    diag = (q[rs] * k[rs][s0:s0 + 1, :]) * jnp.exp2(e)
                    if s < half:
                        slabs.append(jnp.concatenate([diag, qs_bot * ks_top[s0:s0 + 1, :]], axis=0))
                    else:
                        slabs.append(jnp.concatenate([jnp.zeros((half, HD), F32), diag], axis=0))
                p_rows.append(jnp.concatenate(slabs, axis=1).astype(BF16))
    r_all = _dot(jnp.concatenate(p_rows, axis=0), em_ref[...])

    for ci, (rows, a_off, per_head) in enumerate(chunks):
        for h in range(N_HEADS):
            _, _, v, _, o_inter = per_head[h]
            a_rows = []
            for i in range(n_sub):
                idx = h * n_sub + i
                r0 = (ci * N_HEADS * n_sub + idx) * SUB
                in_blk = (lane_c >= i * SUB) & (lane_c < (i + 1) * SUB)
                a_rows.append(a_off[idx] + jnp.where(in_blk, r_all[r0:r0 + SUB, :C], 0.0))
            a = jnp.concatenate(a_rows, axis=0)
            o = _dot(a.astype(BF16), v.astype(BF16)) + o_inter
            y = o * lax.rsqrt(jnp.mean(o * o, axis=-1, keepdims=True) + EPS)
            y = y * gg_ref[:, h * GLA_DV:(h + 1) * GLA_DV]
            gate = zg_ref[rows, c_r + h * GLA_DV:c_r + (h + 1) * GLA_DV].astype(F32)
            u_ref[rows, h * GLA_DV:(h + 1) * GLA_DV] = (y * _silu(gate)).astype(u_ref.dtype)


def _gla_kernel(*refs, n_shift, n_steps):
    zg_ref, ga_ref, wa_ref, ba_ref, gg_ref, em_ref = refs[:6]
    n_in = 6 + 2 * n_shift
    u_ref, sout_ref = refs[n_in:n_in + 2]
    st_ref = refs[n_in + 2 + n_shift]
    t_blk = pl.program_id(1)
    n_chunks = zg_ref.shape[0] // GLA_CHUNK
    assert n_chunks % GLA_UNROLL == 0

    if n_shift:
        stage, in_sem, out_sem, row_sem = refs[n_in + 3 + n_shift:]
        _cache_shift_step(pl.program_id(0) * pl.num_programs(1) + t_blk, n_steps, refs[6:6 + n_shift],
                          refs[6 + n_shift:n_in], refs[n_in + 2:n_in + 2 + n_shift], stage, in_sem, out_sem, row_sem)

    @pl.when(t_blk == 0)
    def _():
        st_ref[...] = jnp.zeros_like(st_ref)

    def chunks(c, carry):
        rows_list = [pl.ds(pl.multiple_of((c * GLA_UNROLL + u) * GLA_CHUNK, GLA_CHUNK), GLA_CHUNK)
                     for u in range(GLA_UNROLL)]
        _gla_chunks(rows_list, zg_ref, ga_ref, wa_ref, ba_ref, gg_ref, em_ref, st_ref, u_ref)
        return carry

    lax.fori_loop(0, n_chunks // GLA_UNROLL, chunks, 0)

    @pl.when(t_blk == pl.num_programs(1) - 1)
    def _():
        for h in range(N_HEADS):
            sout_ref[h] = st_ref[h].T


def _gla_prompt(z, za, wa_pad, b_alpha, g_gla_out, batch, seq, shift=None, *, t_blk):
    nt = seq // t_blk
    assert seq % t_blk == 0 and t_blk % GLA_CHUNK == 0
    kw = N_HEADS * HD
    vw = N_HEADS * GLA_DV
    gla_w = 2 * kw + 2 * vw
    assert C_GQ % gla_w == 0 and (C_GK, C_GV, C_GR) == (C_GQ + kw, C_GQ + 2 * kw, C_GQ + 2 * kw + vw)
    const = lambda b, t: (0, 0)
    in_specs = [pl.BlockSpec((t_blk, gla_w), lambda b, t: (b * nt + t, C_GQ // gla_w)),
                pl.BlockSpec((t_blk, LANES), lambda b, t: (b * nt + t, 0)),
                pl.BlockSpec((LANES, kw), const), pl.BlockSpec((1, kw), const), pl.BlockSpec((1, vw), const),
                pl.BlockSpec((GLA_SUB * HD, LANES), const)]
    out_specs = [pl.BlockSpec((t_blk, vw), lambda b, t: (b * nt + t, 0)),
                 pl.BlockSpec((None, N_HEADS, HD, GLA_DV), lambda b, t: (b, 0, 0, 0))]
    out_shape = [jax.ShapeDtypeStruct((batch * seq, vw), BF16),
                 jax.ShapeDtypeStruct((batch, N_HEADS, HD, GLA_DV), F32)]
    scratch = [pltpu.VMEM((N_HEADS, GLA_DV, HD), F32)]
    args = [z, za, wa_pad, b_alpha.reshape(1, kw), g_gla_out.reshape(1, vw), _diag_select_matrix()]
    n_shift = 0
    if shift is not None:
        caches, new_rows = shift
        n_shift = len(caches)
        any_spec = pl.BlockSpec(memory_space=pl.ANY)
        in_specs += [any_spec] * n_shift + [pl.BlockSpec(r.shape, lambda b, t: (0, 0, 0)) for r in new_rows]
        out_specs += [any_spec] * n_shift
        out_shape += [jax.ShapeDtypeStruct(c.shape, c.dtype) for c in caches]
        args += list(caches) + list(new_rows)
        scratch += [pltpu.VMEM((2, sum(SHIFT_ROWS), HD), F32)] + [pltpu.SemaphoreType.DMA((2, n_shift))] * 3
    return pl.pallas_call(
        functools.partial(_gla_kernel, n_shift=n_shift, n_steps=batch * nt),
        grid=(batch, nt),
        in_specs=in_specs, out_specs=out_specs, out_shape=out_shape, scratch_shapes=scratch,
        compiler_params=_cparams(("arbitrary", "arbitrary")),
        name="gla",
    )(*args)


def _rope_tables(pos):
    half = HD // 2
    inv = ROPE_THETA ** (-jnp.arange(half, dtype=F32) / half)
    ang = pos.astype(F32)[:, None] * inv[None, :]
    cos, sin = jnp.cos(ang), jnp.sin(ang)
    return jnp.concatenate([cos, cos], axis=1), jnp.concatenate([-sin, sin], axis=1)


def _rope(x, cos2, sin2):
    return x * cos2 + pltpu.roll(x, HD // 2, axis=1) * sin2


def _swa_kernel(*refs):
    (q0, q1, q2, k0, k1, k2, v0, v1, v2, sr_ref, cos_ref, sin_ref,
     u_ref, kb0, kb1, kb2, vb0, vb1, vb2,
     q_s, k_s, v_s, o_s, l_s) = refs
    q_in, k_in, v_in = (q0, q1, q2), (k0, k1, k2), (v0, v1, v2)
    kb, vb = (kb0, kb1, kb2), (vb0, vb1, vb2)
    i = pl.program_id(2)
    T = SWA_STEP
    n_grp = len(SWA_PATTERNS)

    @pl.when(i == 0)
    def _():
        for g, (_, dil) in enumerate(SWA_PATTERNS):
            unit = dil * SWA_BLK
            k_s[g, T - unit:T, :] = jnp.zeros((unit, HD), F32)
            v_s[g, T - unit:T, :] = jnp.zeros((unit, HD), F32)

    @pl.when(i > 0)
    def _():
        for g, (_, dil) in enumerate(SWA_PATTERNS):
            unit = dil * SWA_BLK
            k_s[g, T - unit:T, :] = k_s[g, 2 * T - unit:2 * T, :]
            v_s[g, T - unit:T, :] = v_s[g, 2 * T - unit:2 * T, :]

    cos2, sin2 = cos_ref[...], sin_ref[...]
    for g in range(n_grp):
        q_s[g] = _rope(q_in[g][...].astype(F32), cos2, sin2) * (HD ** -0.5)
        k_s[g, T:2 * T, :] = _rope(k_in[g][...].astype(F32), cos2, sin2)
        v_s[g, T:2 * T, :] = v_in[g][...].astype(F32)

    qi = lax.broadcasted_iota(jnp.int32, (SWA_BLK, 2 * SWA_BLK), 0)
    ki = lax.broadcasted_iota(jnp.int32, (SWA_BLK, 2 * SWA_BLK), 1)
    delta = SWA_BLK + qi - ki

    def blocks(it, carry):
        todo = []
        for g, (win, dil) in enumerate(SWA_PATTERNS):
            for j in range(SWA_UNROLL):
                n = it * SWA_UNROLL + j
                unit = dil * SWA_BLK
                u = lax.shift_right_logical(n, dil.bit_length() - 1)
                r = lax.bitwise_and(n, dil - 1)

                def rows(start, size, dil=dil):
                    return pl.ds(start, size) if dil == 1 else pl.ds(start, size, stride=dil)

                q = q_s[g, rows(u * unit + r, SWA_BLK), :]
                kk = k_s[g, rows(T + (u - 1) * unit + r, 2 * SWA_BLK), :]
                s = _dot_nt(q.astype(BF16), kk.astype(BF16))
                ki_min = jnp.where((i * (T // unit) + u) == 0, SWA_BLK, 0)
                valid = (delta >= 0) & (delta <= win // dil) & (ki >= ki_min)
                todo.append((g, rows(u * unit + r, SWA_BLK), rows(T + (u - 1) * unit + r, 2 * SWA_BLK), s, valid))
        soft = []
        for g, q_rows, kv_rows, s, valid in todo:
            s = jnp.where(valid, s, NEG)
            m = jnp.max(s, axis=-1, keepdims=True)
            p = jnp.exp(s - m)
            den = jnp.sum(p, axis=-1, keepdims=True)
            soft.append((g, q_rows, kv_rows, p.astype(BF16), den, m + jnp.log(den)))
        for g, q_rows, kv_rows, p, den, lse in soft:
            o_s[g, q_rows, :] = _dot(p, v_s[g, kv_rows, :].astype(BF16)) / den
            l_s[g, q_rows, :] = jnp.broadcast_to(lse, (SWA_BLK, HD))
        return carry

    lax.fori_loop(0, T // SWA_BLK // SWA_UNROLL, blocks, 0)

    rc = 256

    def merge(c, carry):
        r0 = pl.multiple_of(c * rc, rc)
        rr = pl.ds(r0, rc)
        ls = [l_s[g, rr, :] for g in range(n_grp)]
        mx = functools.reduce(jnp.maximum, ls)
        es = [jnp.exp(l - mx) for l in ls]
        tot = functools.reduce(lambda a, b: a + b, es)
        ob = functools.reduce(lambda a, b: a + b, [(es[g] / tot) * o_s[g, rr, :] for g in range(n_grp)])
        u_ref[rr, :] = (ob * _silu(sr_ref[rr, :].astype(F32))).astype(u_ref.dtype)
        return carry

    lax.fori_loop(0, T // rc, merge, 0)

    @pl.when(i == pl.num_programs(2) - 1)
    def _():
        for g, (win, _) in enumerate(SWA_PATTERNS):
            kb[g][...] = k_s[g, 2 * T - win:2 * T, :]
            vb[g][...] = v_s[g, 2 * T - win:2 * T, :]


def _swa_prompt(z, cos2, sin2, batch, seq):
    T = SWA_STEP
    assert seq % T == 0 and all(w <= T for w, _ in SWA_PATTERNS)
    nt = seq // T
    w_grp = N_HEADS * HD

    def zspec(col0, g):
        cb0 = (col0 + g * w_grp) // HD
        return pl.BlockSpec((T, HD), lambda b, j, i: (b * nt + i, cb0 + j))

    in_specs = ([zspec(C_SQ, g) for g in range(3)] + [zspec(C_SK, g) for g in range(3)]
                + [zspec(C_SV, g) for g in range(3)] + [zspec(C_SR, 0)]
                + [pl.BlockSpec((T, HD), lambda b, j, i: (i, 0))] * 2)
    buf_specs = [pl.BlockSpec((None, w, HD), lambda b, j, i: (b, 0, j)) for w, _ in SWA_PATTERNS]
    buf_shapes = [jax.ShapeDtypeStruct((batch, w, w_grp), F32) for w, _ in SWA_PATTERNS]
    return pl.pallas_call(
        _swa_kernel,
        grid=(batch, N_HEADS, nt),
        in_specs=in_specs,
        out_specs=[pl.BlockSpec((T, HD), lambda b, j, i: (b * nt + i, j))] + buf_specs + buf_specs,
        out_shape=[jax.ShapeDtypeStruct((batch * seq, w_grp), BF16)] + buf_shapes + buf_shapes,
        scratch_shapes=[pltpu.VMEM((3, T, HD), F32), pltpu.VMEM((3, 2 * T, HD), F32),
                        pltpu.VMEM((3, 2 * T, HD), F32), pltpu.VMEM((3, T, HD), F32),
                        pltpu.VMEM((3, T, HD), F32)],
        compiler_params=_cparams(("arbitrary", "arbitrary", "arbitrary")),
        name="swa",
    )(*([z] * 10), cos2, sin2)


def _memattn_kernel(q_ref, mr_ref, kv_ref, u_ref):
    w = N_HEADS * HD
    for h in range(N_HEADS):
        cs = slice(h * HD, (h + 1) * HD)
        q = q_ref[:, cs].astype(F32) * (HD ** -0.5)
        k = kv_ref[:, cs]
        v = kv_ref[:, w + h * HD:w + (h + 1) * HD]
        s = _dot_nt(q.astype(BF16), k.astype(BF16))
        e = jnp.exp(s - jnp.max(s, axis=-1, keepdims=True))
        p = e / jnp.sum(e, axis=-1, keepdims=True)
        o = _dot(p.astype(BF16), v.astype(BF16))
        u_ref[:, cs] = (o * _silu(mr_ref[:, cs].astype(F32))).astype(u_ref.dtype)


def _memattn_prompt(z, mem_kv, batch, seq, *, t_blk):
    nt = seq // t_blk
    w = N_HEADS * HD
    return pl.pallas_call(
        _memattn_kernel,
        grid=(batch, nt),
        in_specs=[pl.BlockSpec((t_blk, w), lambda b, t: (b * nt + t, C_MQ // w)),
                  pl.BlockSpec((t_blk, w), lambda b, t: (b * nt + t, C_MR // w)),
                  pl.BlockSpec((N_MEM, 2 * w), lambda b, t: (b, 0))],
        out_specs=pl.BlockSpec((t_blk, w), lambda b, t: (b * nt + t, 0)),
        out_shape=jax.ShapeDtypeStruct((batch * seq, w), BF16),
        compiler_params=_cparams(("arbitrary", "arbitrary")),
        name="memattn",
    )(z, z, mem_kv)


def _final_kernel(ua_ref, ub_ref, uc_ref, gt_ref, x_ref, wa_ref, wb_ref, wc_ref, wo_ref, gf_ref, y_ref):
    d = x_ref.shape[1]
    ya = _dot(ua_ref[...].astype(BF16), wa_ref[...])
    yb = _dot(ub_ref[...].astype(BF16), wb_ref[...])
    yc = _dot(uc_ref[...].astype(BF16), wc_ref[...])
    mix = (jax.nn.sigmoid(gt_ref[:, 0:d].astype(F32)) * ya + jax.nn.sigmoid(gt_ref[:, d:2 * d].astype(F32)) * yb
           + jax.nn.sigmoid(gt_ref[:, 2 * d:3 * d].astype(F32)) * yc)
    xo = x_ref[...] + _dot(mix.astype(BF16), wo_ref[...])
    y = xo * lax.rsqrt(jnp.mean(xo * xo, axis=-1, keepdims=True) + EPS)
    y_ref[...] = y * gf_ref[...]


def _final(ua, ub, uc, z, x, wa, wb, wc, wo, g_final, *, tm):
    m, d = x.shape
    const = lambda i: (0, 0)
    resident = dict(pipeline_mode=pl.Buffered(1))
    return pl.pallas_call(
        _final_kernel,
        grid=(m // tm,),
        in_specs=[pl.BlockSpec((tm, ua.shape[1]), lambda i: (i, 0)),
                  pl.BlockSpec((tm, ub.shape[1]), lambda i: (i, 0)),
                  pl.BlockSpec((tm, uc.shape[1]), lambda i: (i, 0)),
                  pl.BlockSpec((tm, 3 * d), lambda i: (i, C_GT // (3 * d))),
                  pl.BlockSpec((tm, d), lambda i: (i, 0)),
                  pl.BlockSpec(wa.shape, const, **resident),
                  pl.BlockSpec(wb.shape, const, **resident),
                  pl.BlockSpec(wc.shape, const, **resident),
                  pl.BlockSpec(wo.shape, const, **resident),
                  pl.BlockSpec((1, d), const)],
        out_specs=pl.BlockSpec((tm, d), lambda i: (i, 0)),
        out_shape=jax.ShapeDtypeStruct((m, d), F32),
        compiler_params=_cparams(("arbitrary",)),
        name="final",
    )(ua, ub, uc, z, x, wa, wb, wc, wo, g_final.reshape(1, d))


def _heads_rows(row, col0):
    return jnp.concatenate([row[:, col0 + h * HD:col0 + (h + 1) * HD] for h in range(N_HEADS)], axis=0)


def _decode_attention(q4, kk, vv, k_new=None, v_new=None):
    s = jnp.sum(kk * q4[None], axis=-1, keepdims=True)
    m = jnp.max(s, axis=0)
    if k_new is not None:
        s_new = jnp.sum(k_new * q4, axis=-1, keepdims=True)
        m = jnp.maximum(m, s_new)
    p = jnp.exp(s - m[None])
    den = jnp.sum(p, axis=0)
    acc = jnp.sum(p * vv, axis=0)
    if k_new is not None:
        p_new = jnp.exp(s_new - m)
        den = den + p_new
        acc = acc + p_new * v_new
    return acc / den, m + jnp.log(den)


def _sample_kernel(*refs):
    (z_ref, za_ref, wa_ref, ba_ref, gg_ref, cos_ref, sin_ref, st_ref, cg0, cg1, cg2, cm_ref,
     ua_ref, ub_ref, uc_ref, sout_ref, nr0, nr1, nr2) = refs
    cg, nr = (cg0, cg1, cg2), (nr0, nr1, nr2)
    b = pl.program_id(0)
    zrow = z_ref[pl.ds(b, 1), :]

    ga8 = jnp.broadcast_to(za_ref[pl.ds(b, 1), :], (8, LANES))
    xa = _dot(ga8.astype(BF16), wa_ref[...])[0:1, :] + ba_ref[...]
    a_row = jnp.exp(_log_sigmoid(xa) / GLA_TAU)
    eye = (lax.broadcasted_iota(jnp.int32, (HD, HD), 0) == lax.broadcasted_iota(jnp.int32, (HD, HD), 1))

    def col(row_vec):
        return jnp.sum(jnp.where(eye, jnp.broadcast_to(row_vec, (HD, HD)), 0.0), axis=1, keepdims=True)

    for h in range(N_HEADS):
        q = zrow[:, C_GQ + h * HD:C_GQ + (h + 1) * HD] * (HD ** -0.5)
        k = zrow[:, C_GK + h * HD:C_GK + (h + 1) * HD]
        v = zrow[:, C_GV + h * GLA_DV:C_GV + (h + 1) * GLA_DV]
        s_new = col(a_row[:, h * HD:(h + 1) * HD]) * st_ref[h] + col(k) * v
        sout_ref[h] = s_new
        o = jnp.sum(col(q) * s_new, axis=0, keepdims=True)
        y = o * lax.rsqrt(jnp.mean(o * o, axis=-1, keepdims=True) + EPS)
        y = y * gg_ref[:, h * GLA_DV:(h + 1) * GLA_DV]
        gate = zrow[:, C_GR + h * GLA_DV:C_GR + (h + 1) * GLA_DV]
        ua_ref[:, h * GLA_DV:(h + 1) * GLA_DV] = y * _silu(gate)

    cos2, sin2 = cos_ref[...], sin_ref[...]
    outs, lses = [], []
    w_grp = N_HEADS * HD
    for g in range(3):
        q4 = _rope(_heads_rows(zrow, C_SQ + g * w_grp), cos2, sin2) * (HD ** -0.5)
        k4 = _rope(_heads_rows(zrow, C_SK + g * w_grp), cos2, sin2)
        v4 = _heads_rows(zrow, C_SV + g * w_grp)
        nr[g][0:N_HEADS, :] = k4
        nr[g][N_HEADS:KV_ROWS, :] = v4
        o, lse = _decode_attention(q4, cg[g][:, 0:N_HEADS, :], cg[g][:, N_HEADS:KV_ROWS, :], k4, v4)
        outs.append(o)
        lses.append(lse)
    mx = functools.reduce(jnp.maximum, lses)
    es = [jnp.exp(l - mx) for l in lses]
    tot = es[0] + es[1] + es[2]
    ob = (es[0] / tot) * outs[0] + (es[1] / tot) * outs[1] + (es[2] / tot) * outs[2]
    for h in range(N_HEADS):
        gate = zrow[:, C_SR + h * HD:C_SR + (h + 1) * HD]
        ub_ref[:, h * HD:(h + 1) * HD] = ob[h:h + 1, :] * _silu(gate)

    qm = _heads_rows(zrow, C_MQ) * (HD ** -0.5)
    oc, _ = _decode_attention(qm, cm_ref[:, 0:N_HEADS, :], cm_ref[:, N_HEADS:KV_ROWS, :])
    for h in range(N_HEADS):
        gate = zrow[:, C_MR + h * HD:C_MR + (h + 1) * HD]
        uc_ref[:, h * HD:(h + 1) * HD] = oc[h:h + 1, :] * _silu(gate)


def _sample_mixers(z, za, wa_pad, b_alpha, g_gla_out, cos2, sin2, state, caches, cache_mem):
    db = z.shape[0]
    kw, vw, w = N_HEADS * HD, N_HEADS * GLA_DV, N_HEADS * HD
    const2 = lambda b: (0, 0)
    gathered, gather_specs = [], []
    for c, (win, dil) in zip(caches, SWA_PATTERNS):
        assert c.shape == (1, db, win, 2, N_HEADS, HD) and PAST_LEN >= win
        n_keys = win // dil
        gathered.append(c.reshape(db, n_keys, dil * KV_ROWS, HD))
        gather_specs.append(pl.BlockSpec((None, n_keys, KV_ROWS, HD), lambda b: (b, 0, 0, 0)))
    row3 = lambda n: pl.BlockSpec((None, 1, n), lambda b: (b, 0, 0))
    new_spec = pl.BlockSpec((None, KV_ROWS, HD), lambda b: (b, 0, 0))
    res = pl.pallas_call(
        _sample_kernel,
        grid=(db,),
        in_specs=[pl.BlockSpec(z.shape, const2), pl.BlockSpec(za.shape, const2),
                  pl.BlockSpec(wa_pad.shape, const2), pl.BlockSpec((1, kw), const2),
                  pl.BlockSpec((1, vw), const2), pl.BlockSpec((1, HD), const2), pl.BlockSpec((1, HD), const2),
                  pl.BlockSpec((None, None, N_HEADS, HD, GLA_DV), lambda b: (0, b, 0, 0, 0))]
                 + gather_specs
                 + [pl.BlockSpec((None, N_MEM, KV_ROWS, HD), lambda b: (b, 0, 0, 0))],
        out_specs=[row3(vw), row3(w), row3(w),
                   pl.BlockSpec((None, None, N_HEADS, HD, GLA_DV), lambda b: (0, b, 0, 0, 0))] + [new_spec] * 3,
        out_shape=[jax.ShapeDtypeStruct((db, 1, vw), F32), jax.ShapeDtypeStruct((db, 1, w), F32),
                   jax.ShapeDtypeStruct((db, 1, w), F32), jax.ShapeDtypeStruct(state.shape, F32)]
                  + [jax.ShapeDtypeStruct((db, KV_ROWS, HD), F32)] * 3,
        compiler_params=_cparams(("arbitrary",)),
        name="sample_mixers",
    )(z, za, wa_pad, b_alpha.reshape(1, kw), g_gla_out.reshape(1, vw), cos2, sin2, state,
      *gathered, cache_mem.reshape(db, N_MEM, KV_ROWS, HD))
    ua, ub, uc, s_out = res[:4]
    return ua.reshape(db, vw), ub.reshape(db, w), uc.reshape(db, w), s_out, res[4:]


def _prep_w_in_kernel(wt_ref, wm_ref, wg_ref, buf, ga_buf, sem, ga_sem, *, c_ga, c_gt, rows):
    i = pl.program_id(0)
    n_gt = (wt_ref.shape[0] - c_gt) // rows
    n_lo = c_ga // rows

    def fetch(ii, slot):
        src = jnp.where(ii < n_gt, c_gt + ii * rows,
                        jnp.where(ii < n_gt + n_lo, (ii - n_gt) * rows, c_ga + GLA_RANK + (ii - n_gt - n_lo) * rows))
        return pltpu.make_async_copy(wt_ref.at[pl.ds(pl.multiple_of(src, GLA_RANK), rows)], buf.at[slot], sem.at[slot])

    ga_copy = pltpu.make_async_copy(wt_ref.at[pl.ds(c_ga, GLA_RANK)], ga_buf, ga_sem)

    @pl.when(i == 0)
    def _():
        fetch(0, 0).start()
        ga_copy.start()

    @pl.when(i + 1 < pl.num_programs(0))
    def _():
        fetch(i + 1, lax.rem(i + 1, 2)).start()

    @pl.when(i == 0)
    def _():
        ga_copy.wait()
        wg_ref[0:GLA_RANK, :] = ga_buf[...].astype(BF16)
        wg_ref[GLA_RANK:, :] = jnp.zeros((LANES - GLA_RANK, wg_ref.shape[1]), BF16)

    slot = lax.rem(i, 2)
    fetch(i, slot).wait()
    wm_ref[...] = buf[slot].astype(BF16)


def _prep_w_in(wt, d, *, rows):
    n_all, k = wt.shape
    c_ga = 2 * N_HEADS * HD + 2 * N_HEADS * GLA_DV
    c_gt = n_all - 3 * d
    assert n_all - GLA_RANK == N_MAIN and k == d
    assert (n_all - c_gt) % rows == 0 and c_ga % rows == 0 and (c_gt - c_ga - GLA_RANK) % rows == 0
    return pl.pallas_call(
        functools.partial(_prep_w_in_kernel, c_ga=c_ga, c_gt=c_gt, rows=rows),
        grid=(N_MAIN // rows,),
        in_specs=[pl.BlockSpec(memory_space=pl.ANY)],
        out_specs=[pl.BlockSpec((rows, d), lambda i: (i, 0)), pl.BlockSpec((LANES, d), lambda i: (0, 0))],
        out_shape=[jax.ShapeDtypeStruct((N_MAIN, d), BF16), jax.ShapeDtypeStruct((LANES, d), BF16)],
        scratch_shapes=[pltpu.VMEM((2, rows, d), F32), pltpu.VMEM((GLA_RANK, d), F32),
                        pltpu.SemaphoreType.DMA((2,)), pltpu.SemaphoreType.DMA(())],
        compiler_params=_cparams(("arbitrary",)),
        name="prep_w_in",
    )(wt)


def kernel(x_prompt, x_sample, mem_prompt, state_gla, cache_swa_w128, cache_swa_w512, cache_swa_w2048, cache_mem_kv, g_norm, w_in, w_alpha2, b_alpha, g_gla_out, g_mem, w_mem_kv, w_proj_a, w_proj_b, w_proj_c, w_out, g_final):
    batch, seq, d = x_prompt.shape
    db, dec_seq, _ = x_sample.shape
    assert g_norm.shape[0] == 1 and dec_seq == 1

    w_main, w_ga = _prep_w_in(jnp.swapaxes(w_in[0], 0, 1), d, rows=512)
    wa_pad = jnp.pad(w_alpha2[0], ((0, LANES - GLA_RANK), (0, 0))).astype(BF16)
    wpa, wpb, wpc, wo = (w[0].astype(BF16) for w in (w_proj_a, w_proj_b, w_proj_c, w_out))

    xp = x_prompt.reshape(batch * seq, d)
    xs = x_sample.reshape(db, d)

    zs, zas = _norm_matmul(xs, g_norm[0], w_main, w_ga, w_rows_out=True, tm=db, tn=2560)
    cos_s, sin_s = _rope_tables(jnp.full((1,), PAST_LEN, jnp.int32))
    caches = (cache_swa_w128, cache_swa_w512, cache_swa_w2048)
    uas, ubs, ucs, gla_s, new_rows = _sample_mixers(
        zs, zas, wa_pad, b_alpha[0], g_gla_out[0], cos_s, sin_s, state_gla, caches, cache_mem_kv)
    y_sample = _final(uas, ubs, ucs, zs, xs, wpa, wpb, wpc, wo, g_final, tm=db).reshape(db, 1, d)

    z, za = _norm_matmul(xp, g_norm[0], w_main, w_ga, w_rows_out=True, out_dtype=BF16, tm=1024, tn=2560)
    flat_caches = [c.reshape(db, w * KV_ROWS, HD) for c, (w, _) in zip(caches, SWA_PATTERNS)]
    ua, gla_p, *shifted = _gla_prompt(z, za, wa_pad, b_alpha[0], g_gla_out[0], batch, seq,
                                      (flat_caches, new_rows), t_blk=512)
    swa_s = [o.reshape(c.shape) for o, c in zip(shifted, caches)]
    mem_kv = _norm_matmul(mem_prompt.reshape(batch * N_MEM, d), g_mem[0], w_mem_kv[0].astype(BF16),
                          tm=batch * N_MEM, tn=512)
    cos_p, sin_p = _rope_tables(jnp.arange(seq, dtype=jnp.int32))
    swa_res = _swa_prompt(z, cos_p, sin_p, batch, seq)
    ub, kbufs, vbufs = swa_res[0], swa_res[1:4], swa_res[4:7]
    uc = _memattn_prompt(z, mem_kv, batch, seq, t_blk=1024)
    y_prompt = _final(ua, ub, uc, z, xp, wpa, wpb, wpc, wo, g_final, tm=512).reshape(batch, seq, d)

    swa_p = [jnp.stack([k.reshape(batch, w, N_HEADS, HD), v.reshape(batch, w, N_HEADS, HD)], axis=2)[None]
             for k, v, (w, _) in zip(kbufs, vbufs, SWA_PATTERNS)]
    mem_kv_prompt = mem_kv.reshape(1, batch, N_MEM, 2, N_HEADS, HD)
    return (y_prompt, y_sample, gla_p[None], swa_p[0], swa_p[1], swa_p[2], mem_kv_prompt,
            gla_s, swa_s[0], swa_s[1], swa_s[2])
```

```python
import functools

import numpy as np
import jax
import jax.numpy as jnp
from jax import lax
from jax.experimental import pallas as pl
from jax.experimental.pallas import tpu as pltpu

F32 = jnp.float32
BF16 = jnp.bfloat16

EPS = 1e-6
ROPE_THETA = 10000.0
NEG = -1e30
PAST_LEN = 16384

HD = 128
N_HEADS = 4
GLA_DV = 256
GLA_RANK = 16
GLA_TAU = 16.0
LOG2_E = 1.4426950408889634
GLA_CHUNK = 64
GLA_SUB = 16
GLA_UNROLL = 4
SWA_PATTERNS = ((128, 1), (512, 4), (2048, 16))
SWA_BLK = 128
SWA_STEP = 2048
SWA_UNROLL = 2
N_MEM = 256

LANES = 128
VMEM_LIMIT = 60000 * 1024

C_GT = 0
C_GQ, C_GK, C_GV, C_GR = 6144, 6656, 7168, 8192
C_SQ, C_SK, C_SV, C_SR = 9216, 10752, 12288, 13824
C_MQ, C_MR = 14336, 14848
N_MAIN = 15360


def _cparams(sem):
    return pltpu.CompilerParams(dimension_semantics=sem, vmem_limit_bytes=VMEM_LIMIT)


def _dot(a, b):
    return jnp.dot(a, b, preferred_element_type=F32)


def _dot_nt(a, b):
    return lax.dot_general(a, b, (((1,), (1,)), ((), ())), preferred_element_type=F32)


def _dot_tn(a, b):
    return lax.dot_general(a, b, (((0,), (0,)), ((), ())), preferred_element_type=F32)


def _silu(x):
    return x * jax.nn.sigmoid(x)


def _log_sigmoid(x):
    return jnp.minimum(x, 0.0) - jnp.log1p(jnp.exp(-jnp.abs(x)))


KV_ROWS = 2 * N_HEADS
SHIFT_ROWS = tuple((w - 1) * KV_ROWS for w, _ in SWA_PATTERNS)
SHIFT_OFFS = tuple(sum(SHIFT_ROWS[:g]) for g in range(len(SWA_PATTERNS)))


def _cache_shift_step(step, n_steps, ca, nr, co, stage, in_sem, out_sem, row_sem):
    n_grp = len(ca)
    n_batch = ca[0].shape[0]
    assert n_steps >= n_batch
    slot = lax.rem(step, 2)

    def in_copy(g, bb, sl):
        return pltpu.make_async_copy(ca[g].at[bb, pl.ds(KV_ROWS, SHIFT_ROWS[g])],
                                     stage.at[sl, pl.ds(SHIFT_OFFS[g], SHIFT_ROWS[g])], in_sem.at[sl, g])

    def out_copy(g, bb, sl):
        return pltpu.make_async_copy(stage.at[sl, pl.ds(SHIFT_OFFS[g], SHIFT_ROWS[g])],
                                     co[g].at[bb, pl.ds(0, SHIFT_ROWS[g])], out_sem.at[sl, g])

    def row_copy(g, bb, sl):
        return pltpu.make_async_copy(nr[g].at[bb], co[g].at[bb, pl.ds(SHIFT_ROWS[g], KV_ROWS)], row_sem.at[sl, g])

    def finish_writes(bb, sl):
        for g in range(n_grp):
            out_copy(g, bb, sl).wait()
            row_copy(g, bb, sl).wait()

    @pl.when(step == 0)
    def _():
        for g in range(n_grp):
            in_copy(g, 0, 0).start()

    @pl.when((step >= 1) & (step <= n_batch))
    def _():
        finish_writes(step - 1, 1 - slot)

    @pl.when(step + 1 < n_batch)
    def _():
        for g in range(n_grp):
            in_copy(g, step + 1, 1 - slot).start()

    @pl.when(step < n_batch)
    def _():
        for g in range(n_grp):
            in_copy(g, step, slot).wait()
            out_copy(g, step, slot).start()
            row_copy(g, step, slot).start()

    if n_steps == n_batch:
        @pl.when(step == n_steps - 1)
        def _():
            finish_writes(step, slot)


def _norm_matmul_kernel(*refs, has_extra, row_chunk, w_rows_out):
    mm = _dot_nt if w_rows_out else _dot
    if has_extra:
        x_ref, g_ref, w_ref, wx_ref, o_ref, ox_ref, h_ref = refs
    else:
        x_ref, g_ref, w_ref, o_ref, h_ref = refs
    tm = x_ref.shape[0]

    @pl.when(pl.program_id(1) == 0)
    def _():
        for c in range(tm // row_chunk):
            rs = slice(c * row_chunk, (c + 1) * row_chunk)
            x = x_ref[rs, :]
            y = x * lax.rsqrt(jnp.mean(x * x, axis=-1, keepdims=True) + EPS)
            h_ref[rs, :] = (y * g_ref[...]).astype(BF16)
            if has_extra:
                ox_ref[rs, :] = mm(h_ref[rs, :], wx_ref[...])

    o_ref[...] = mm(h_ref[...], w_ref[...]).astype(o_ref.dtype)


def _norm_matmul(x, g, w, wx=None, *, w_rows_out=False, out_dtype=F32, tm, tn):
    m, d = x.shape
    n = w.shape[0] if w_rows_out else w.shape[1]
    assert m % tm == 0 and n % tn == 0
    has_extra = wx is not None
    row_chunk = min(tm, 128)
    w_spec = pl.BlockSpec((tn, d), lambda i, j: (j, 0)) if w_rows_out else pl.BlockSpec((d, tn), lambda i, j: (0, j))
    in_specs = [pl.BlockSpec((tm, d), lambda i, j: (i, 0)),
                pl.BlockSpec((1, d), lambda i, j: (0, 0)),
                w_spec]
    out_specs = [pl.BlockSpec((tm, tn), lambda i, j: (i, j))]
    out_shape = [jax.ShapeDtypeStruct((m, n), out_dtype)]
    args = [x, g.reshape(1, d), w]
    if has_extra:
        nx = wx.shape[0] if w_rows_out else wx.shape[1]
        in_specs.append(pl.BlockSpec(wx.shape, lambda i, j: (0, 0)))
        out_specs.append(pl.BlockSpec((tm, nx), lambda i, j: (i, 0)))
        out_shape.append(jax.ShapeDtypeStruct((m, nx), F32))
        args.append(wx)
    res = pl.pallas_call(
        functools.partial(_norm_matmul_kernel, has_extra=has_extra, row_chunk=row_chunk, w_rows_out=w_rows_out),
        grid=(m // tm, n // tn),
        in_specs=in_specs, out_specs=out_specs, out_shape=out_shape,
        scratch_shapes=[pltpu.VMEM((tm, d), BF16)],
        compiler_params=_cparams(("arbitrary", "arbitrary")),
        name="norm_matmul",
    )(*args)
    return res if has_extra else res[0]


def _diag_select_matrix():
    rows = np.arange(GLA_SUB * HD)[:, None] // HD
    cols = np.arange(LANES)[None, :] % GLA_SUB
    return jnp.asarray((rows == cols).astype(np.float32), dtype=BF16)


def _gla_chunks(rows_list, zg_ref, ga_ref, wa_ref, ba_ref, gg_ref, em_ref, st_ref, u_ref):
    C, SUB, n_sub = GLA_CHUNK, GLA_SUB, GLA_CHUNK // GLA_SUB
    kw = N_HEADS * HD
    c_k, c_v, c_r = kw, 2 * kw, 2 * kw + N_HEADS * GLA_DV
    row = lax.broadcasted_iota(jnp.int32, (C, C), 0)
    col = lax.broadcasted_iota(jnp.int32, (C, C), 1)
    tri = (col <= row).astype(F32)
    row_c = lax.broadcasted_iota(jnp.int32, (C, HD), 0)
    half = SUB // 2
    half_row = lax.broadcasted_iota(jnp.int32, (half, HD), 0)
    lane_c = lax.broadcasted_iota(jnp.int32, (SUB, C), 1)

    xas = [_dot(ga_ref[rows, :].astype(BF16), wa_ref[...]) + ba_ref[...] for rows in rows_list]
    b_alls = [jnp.dot(tri, _log_sigmoid(xa) * (LOG2_E / GLA_TAU), preferred_element_type=F32,
                      precision=lax.Precision.HIGHEST) for xa in xas]

    chunks = []
    for rows, b_all in zip(rows_list, b_alls):
        a_off = []
        per_head = []
        for h in range(N_HEADS):
            q = zg_ref[rows, h * HD:(h + 1) * HD].astype(F32) * (HD ** -0.5)
            k = zg_ref[rows, c_k + h * HD:c_k + (h + 1) * HD].astype(F32)
            v = zg_ref[rows, c_v + h * GLA_DV:c_v + (h + 1) * GLA_DV]
            b = b_all[:, h * HD:(h + 1) * HD]
            for i in range(n_sub):
                if i == 0:
                    a_off.append(jnp.zeros((SUB, C), F32))
                else:
                    sl = slice(i * SUB, (i + 1) * SUB)
                    b_ref_row = b[i * SUB - 1:i * SUB, :]
                    qs = q[sl] * jnp.exp2(b[sl] - b_ref_row)
                    ks = k * jnp.exp2(jnp.where(row_c < i * SUB, b_ref_row - b, -jnp.inf))
                    a_off.append(_dot_nt(qs.astype(BF16), ks.astype(BF16)))
            st = st_ref[h]
            o_inter = _dot_nt((q * jnp.exp2(b)).astype(BF16), st.astype(BF16))
            b_end = b[C - 1:C, :]
            kd = k * jnp.exp2(b_end - b)
            st_ref[h] = st * jnp.exp2(b_end) + _dot_tn(v.astype(BF16), kd.astype(BF16))
            per_head.append((q, k, v, b, o_inter))
        chunks.append((rows, a_off, per_head))

    p_rows = []
    for _, _, per_head in chunks:
        for h in range(N_HEADS):
            q, k, _, b, _ = per_head[h]
            for i in range(n_sub):
                top, bot = slice(i * SUB, i * SUB + half), slice(i * SUB + half, (i + 1) * SUB)
                b_mid = b[i * SUB + half - 1:i * SUB + half, :]
                qs_bot = q[bot] * jnp.exp2(b[bot] - b_mid)
                ks_top = k[top] * jnp.exp2(b_mid - b[top])
                slabs = []
                for s in range(SUB):
                    rs = top if s < half else bot
                    s0 = s % half
                    e = jnp.where(half_row >= s0, b[rs] - b[rs][s0:s0 + 1, :], -jnp.inf)
                    diag = (q[rs] * k[rs][s0:s0 + 1, :]) * jnp.exp2(e)
                    if s < half:
                        slabs.append(jnp.concatenate([diag, qs_bot * ks_top[s0:s0 + 1, :]], axis=0))
                    else:
                        slabs.append(jnp.concatenate([jnp.zeros((half, HD), F32), diag], axis=0))
                p_rows.append(jnp.concatenate(slabs, axis=1).astype(BF16))
    r_all = _dot(jnp.concatenate(p_rows, axis=0), em_ref[...])

    for ci, (rows, a_off, per_head) in enumerate(chunks):
        for h in range(N_HEADS):
            _, _, v, _, o_inter = per_head[h]
            a_rows = []
            for i in range(n_sub):
                idx = h * n_sub + i
                r0 = (ci * N_HEADS * n_sub + idx) * SUB
                in_blk = (lane_c >= i * SUB) & (lane_c < (i + 1) * SUB)
                a_rows.append(a_off[idx] + jnp.where(in_blk, r_all[r0:r0 + SUB, :C], 0.0))
            a = jnp.concatenate(a_rows, axis=0)
            o = _dot(a.astype(BF16), v.astype(BF16)) + o_inter
            y = o * lax.rsqrt(jnp.mean(o * o, axis=-1, keepdims=True) + EPS)
            y = y * gg_ref[:, h * GLA_DV:(h + 1) * GLA_DV]
            gate = zg_ref[rows, c_r + h * GLA_DV:c_r + (h + 1) * GLA_DV].astype(F32)
            u_ref[rows, h * GLA_DV:(h + 1) * GLA_DV] = (y * _silu(gate)).astype(u_ref.dtype)


def _gla_kernel(*refs, n_shift, n_steps):
    zg_ref, ga_ref, wa_ref, ba_ref, gg_ref, em_ref = refs[:6]
    n_in = 6 + 2 * n_shift
    u_ref, sout_ref = refs[n_in:n_in + 2]
    st_ref = refs[n_in + 2 + n_shift]
    t_blk = pl.program_id(1)
    n_chunks = zg_ref.shape[0] // GLA_CHUNK
    assert n_chunks % GLA_UNROLL == 0

    if n_shift:
        stage, in_sem, out_sem, row_sem = refs[n_in + 3 + n_shift:]
        _cache_shift_step(pl.program_id(0) * pl.num_programs(1) + t_blk, n_steps, refs[6:6 + n_shift],
                          refs[6 + n_shift:n_in], refs[n_in + 2:n_in + 2 + n_shift], stage, in_sem, out_sem, row_sem)

    @pl.when(t_blk == 0)
    def _():
        st_ref[...] = jnp.zeros_like(st_ref)

    def chunks(c, carry):
        rows_list = [pl.ds(pl.multiple_of((c * GLA_UNROLL + u) * GLA_CHUNK, GLA_CHUNK), GLA_CHUNK)
                     for u in range(GLA_UNROLL)]
        _gla_chunks(rows_list, zg_ref, ga_ref, wa_ref, ba_ref, gg_ref, em_ref, st_ref, u_ref)
        return carry

    lax.fori_loop(0, n_chunks // GLA_UNROLL, chunks, 0)

    @pl.when(t_blk == pl.num_programs(1) - 1)
    def _():
        for h in range(N_HEADS):
            sout_ref[h] = st_ref[h].T


def _gla_prompt(z, za, wa_pad, b_alpha, g_gla_out, batch, seq, shift=None, *, t_blk):
    nt = seq // t_blk
    assert seq % t_blk == 0 and t_blk % GLA_CHUNK == 0
    kw = N_HEADS * HD
    vw = N_HEADS * GLA_DV
    gla_w = 2 * kw + 2 * vw
    assert C_GQ % gla_w == 0 and (C_GK, C_GV, C_GR) == (C_GQ + kw, C_GQ + 2 * kw, C_GQ + 2 * kw + vw)
    const = lambda b, t: (0, 0)
    in_specs = [pl.BlockSpec((t_blk, gla_w), lambda b, t: (b * nt + t, C_GQ // gla_w)),
                pl.BlockSpec((t_blk, LANES), lambda b, t: (b * nt + t, 0)),
                pl.BlockSpec((LANES, kw), const), pl.BlockSpec((1, kw), const), pl.BlockSpec((1, vw), const),
                pl.BlockSpec((GLA_SUB * HD, LANES), const)]
    out_specs = [pl.BlockSpec((t_blk, vw), lambda b, t: (b * nt + t, 0)),
                 pl.BlockSpec((None, N_HEADS, HD, GLA_DV), lambda b, t: (b, 0, 0, 0))]
    out_shape = [jax.ShapeDtypeStruct((batch * seq, vw), BF16),
                 jax.ShapeDtypeStruct((batch, N_HEADS, HD, GLA_DV), F32)]
    scratch = [pltpu.VMEM((N_HEADS, GLA_DV, HD), F32)]
    args = [z, za, wa_pad, b_alpha.reshape(1, kw), g_gla_out.reshape(1, vw), _diag_select_matrix()]
    n_shift = 0
    if shift is not None:
        caches, new_rows = shift
        n_shift = len(caches)
        any_spec = pl.BlockSpec(memory_space=pl.ANY)
        in_specs += [any_spec] * n_shift + [pl.BlockSpec(r.shape, lambda b, t: (0, 0, 0)) for r in new_rows]
        out_specs += [any_spec] * n_shift
        out_shape += [jax.ShapeDtypeStruct(c.shape, c.dtype) for c in caches]
        args += list(caches) + list(new_rows)
        scratch += [pltpu.VMEM((2, sum(SHIFT_ROWS), HD), F32)] + [pltpu.SemaphoreType.DMA((2, n_shift))] * 3
    return pl.pallas_call(
        functools.partial(_gla_kernel, n_shift=n_shift, n_steps=batch * nt),
        grid=(batch, nt),
        in_specs=in_specs, out_specs=out_specs, out_shape=out_shape, scratch_shapes=scratch,
        compiler_params=_cparams(("arbitrary", "arbitrary")),
        name="gla",
    )(*args)


def _rope_tables(pos):
    half = HD // 2
    inv = ROPE_THETA ** (-jnp.arange(half, dtype=F32) / half)
    ang = pos.astype(F32)[:, None] * inv[None, :]
    cos, sin = jnp.cos(ang), jnp.sin(ang)
    return jnp.concatenate([cos, cos], axis=1), jnp.concatenate([-sin, sin], axis=1)


def _rope_tables_range(n, blk=128):
    assert n % blk == 0
    half = HD // 2
    inv = ROPE_THETA ** (-jnp.arange(half, dtype=F32) / half)
    ang_a = (jnp.arange(n // blk, dtype=jnp.int32) * blk).astype(F32)[:, None] * inv[None, :]
    ang_b = jnp.arange(blk, dtype=jnp.int32).astype(F32)[:, None] * inv[None, :]
    ca, sa, cb, sb = jnp.cos(ang_a)[:, None], jnp.sin(ang_a)[:, None], jnp.cos(ang_b)[None], jnp.sin(ang_b)[None]
    cos = (ca * cb - sa * sb).reshape(n, half)
    sin = (sa * cb + ca * sb).reshape(n, half)
    return jnp.concatenate([cos, cos], axis=1), jnp.concatenate([-sin, sin], axis=1)


def _rope(x, cos2, sin2):
    return x * cos2 + pltpu.roll(x, HD // 2, axis=1) * sin2


def _swa_kernel(*refs):
    (q0, q1, q2, k0, k1, k2, v0, v1, v2, sr_ref, cos_ref, sin_ref,
     u_ref, kv0, kv1, kv2,
     q_s, k_s, v_s, o_s, l_s) = refs
    q_in, k_in, v_in = (q0, q1, q2), (k0, k1, k2), (v0, v1, v2)
    kv_out = (kv0, kv1, kv2)
    i = pl.program_id(2)
    T = SWA_STEP
    n_grp = len(SWA_PATTERNS)

    @pl.when(i == 0)
    def _():
        for g, (_, dil) in enumerate(SWA_PATTERNS):
            unit = dil * SWA_BLK
            k_s[g, T - unit:T, :] = jnp.zeros((unit, HD), F32)
            v_s[g, T - unit:T, :] = jnp.zeros((unit, HD), F32)

    @pl.when(i > 0)
    def _():
        for g, (_, dil) in enumerate(SWA_PATTERNS):
            unit = dil * SWA_BLK
            k_s[g, T - unit:T, :] = k_s[g, 2 * T - unit:2 * T, :]
            v_s[g, T - unit:T, :] = v_s[g, 2 * T - unit:2 * T, :]

    cos2, sin2 = cos_ref[...], sin_ref[...]
    for g in range(n_grp):
        q_s[g] = _rope(q_in[g][...].astype(F32), cos2, sin2) * (HD ** -0.5)
        k_s[g, T:2 * T, :] = _rope(k_in[g][...].astype(F32), cos2, sin2)
        v_s[g, T:2 * T, :] = v_in[g][...].astype(F32)

    qi = lax.broadcasted_iota(jnp.int32, (SWA_BLK, 2 * SWA_BLK), 0)
    ki = lax.broadcasted_iota(jnp.int32, (SWA_BLK, 2 * SWA_BLK), 1)
    delta = SWA_BLK + qi - ki

    def blocks(it, carry):
        todo = []
        for g, (win, dil) in enumerate(SWA_PATTERNS):
            for j in range(SWA_UNROLL):
                n = it * SWA_UNROLL + j
                unit = dil * SWA_BLK
                u = lax.shift_right_logical(n, dil.bit_length() - 1)
                r = lax.bitwise_and(n, dil - 1)

                def rows(start, size, dil=dil):
                    return pl.ds(start, size) if dil == 1 else pl.ds(start, size, stride=dil)

                q = q_s[g, rows(u * unit + r, SWA_BLK), :]
                kk = k_s[g, rows(T + (u - 1) * unit + r, 2 * SWA_BLK), :]
                s = _dot_nt(q.astype(BF16), kk.astype(BF16))
                ki_min = jnp.where((i * (T // unit) + u) == 0, SWA_BLK, 0)
                valid = (delta >= 0) & (delta <= win // dil) & (ki >= ki_min)
                todo.append((g, rows(u * unit + r, SWA_BLK), rows(T + (u - 1) * unit + r, 2 * SWA_BLK), s, valid))
        soft = []
        for g, q_rows, kv_rows, s, valid in todo:
            s = jnp.where(valid, s, NEG)
            m = jnp.max(s, axis=-1, keepdims=True)
            p = jnp.exp(s - m)
            den = jnp.sum(p, axis=-1, keepdims=True)
            soft.append((g, q_rows, kv_rows, p.astype(BF16), den, m + jnp.log(den)))
        for g, q_rows, kv_rows, p, den, lse in soft:
            o_s[g, q_rows, :] = _dot(p, v_s[g, kv_rows, :].astype(BF16)) / den
            l_s[g, q_rows, :] = jnp.broadcast_to(lse, (SWA_BLK, HD))
        return carry

    lax.fori_loop(0, T // SWA_BLK // SWA_UNROLL, blocks, 0)

    rc = 256

    def merge(c, carry):
        r0 = pl.multiple_of(c * rc, rc)
        rr = pl.ds(r0, rc)
        ls = [l_s[g, rr, :] for g in range(n_grp)]
        mx = functools.reduce(jnp.maximum, ls)
        es = [jnp.exp(l - mx) for l in ls]
        tot = functools.reduce(lambda a, b: a + b, es)
        ob = functools.reduce(lambda a, b: a + b, [(es[g] / tot) * o_s[g, rr, :] for g in range(n_grp)])
        u_ref[rr, :] = (ob * _silu(sr_ref[rr, :].astype(F32))).astype(u_ref.dtype)
        return carry

    lax.fori_loop(0, T // rc, merge, 0)

    @pl.when(i == pl.num_programs(2) - 1)
    def _():
        for g, (win, _) in enumerate(SWA_PATTERNS):
            head = pl.program_id(1)
            kv_out[g][pl.ds(head, win, stride=KV_ROWS), :] = k_s[g, 2 * T - win:2 * T, :]
            kv_out[g][pl.ds(N_HEADS + head, win, stride=KV_ROWS), :] = v_s[g, 2 * T - win:2 * T, :]


def _swa_prompt(z, cos2, sin2, batch, seq):
    T = SWA_STEP
    assert seq % T == 0 and all(w <= T for w, _ in SWA_PATTERNS)
    nt = seq // T
    w_grp = N_HEADS * HD

    def zspec(col0, g):
        cb0 = (col0 + g * w_grp) // HD
        return pl.BlockSpec((T, HD), lambda b, j, i: (b * nt + i, cb0 + j))

    in_specs = ([zspec(C_SQ, g) for g in range(3)] + [zspec(C_SK, g) for g in range(3)]
                + [zspec(C_SV, g) for g in range(3)] + [zspec(C_SR, 0)]
                + [pl.BlockSpec((T, HD), lambda b, j, i: (i, 0))] * 2)
    buf_specs = [pl.BlockSpec((None, w * KV_ROWS, HD), lambda b, j, i: (b, 0, 0), pipeline_mode=pl.Buffered(1))
                 for w, _ in SWA_PATTERNS]
    buf_shapes = [jax.ShapeDtypeStruct((batch, w * KV_ROWS, HD), F32) for w, _ in SWA_PATTERNS]
    return pl.pallas_call(
        _swa_kernel,
        grid=(batch, N_HEADS, nt),
        in_specs=in_specs,
        out_specs=[pl.BlockSpec((T, HD), lambda b, j, i: (b * nt + i, j))] + buf_specs,
        out_shape=[jax.ShapeDtypeStruct((batch * seq, w_grp), BF16)] + buf_shapes,
        scratch_shapes=[pltpu.VMEM((3, T, HD), F32), pltpu.VMEM((3, 2 * T, HD), F32),
                        pltpu.VMEM((3, 2 * T, HD), F32), pltpu.VMEM((3, T, HD), F32),
                        pltpu.VMEM((3, T, HD), F32)],
        compiler_params=_cparams(("arbitrary", "arbitrary", "arbitrary")),
        name="swa",
    )(*([z] * 10), cos2, sin2)


def _memattn_kernel(q_ref, mr_ref, kv_ref, u_ref):
    w = N_HEADS * HD
    for h in range(N_HEADS):
        cs = slice(h * HD, (h + 1) * HD)
        q = q_ref[:, cs].astype(F32) * (HD ** -0.5)
        k = kv_ref[:, cs]
        v = kv_ref[:, w + h * HD:w + (h + 1) * HD]
        s = _dot_nt(q.astype(BF16), k.astype(BF16))
        e = jnp.exp(s - jnp.max(s, axis=-1, keepdims=True))
        p = e / jnp.sum(e, axis=-1, keepdims=True)
        o = _dot(p.astype(BF16), v.astype(BF16))
        u_ref[:, cs] = (o * _silu(mr_ref[:, cs].astype(F32))).astype(u_ref.dtype)


def _memattn_prompt(z, mem_kv, batch, seq, *, t_blk):
    nt = seq // t_blk
    w = N_HEADS * HD
    return pl.pallas_call(
        _memattn_kernel,
        grid=(batch, nt),
        in_specs=[pl.BlockSpec((t_blk, w), lambda b, t: (b * nt + t, C_MQ // w)),
                  pl.BlockSpec((t_blk, w), lambda b, t: (b * nt + t, C_MR // w)),
                  pl.BlockSpec((N_MEM, 2 * w), lambda b, t: (b, 0))],
        out_specs=pl.BlockSpec((t_blk, w), lambda b, t: (b * nt + t, 0)),
        out_shape=jax.ShapeDtypeStruct((batch * seq, w), BF16),
        compiler_params=_cparams(("arbitrary", "arbitrary")),
        name="memattn",
    )(z, z, mem_kv)


def _final_kernel(ua_ref, ub_ref, uc_ref, gt_ref, x_ref, wa_ref, wb_ref, wc_ref, wo_ref, gf_ref, y_ref):
    d = x_ref.shape[1]
    ya = _dot(ua_ref[...].astype(BF16), wa_ref[...])
    yb = _dot(ub_ref[...].astype(BF16), wb_ref[...])
    yc = _dot(uc_ref[...].astype(BF16), wc_ref[...])
    mix = (jax.nn.sigmoid(gt_ref[:, 0:d].astype(F32)) * ya + jax.nn.sigmoid(gt_ref[:, d:2 * d].astype(F32)) * yb
           + jax.nn.sigmoid(gt_ref[:, 2 * d:3 * d].astype(F32)) * yc)
    xo = x_ref[...] + _dot(mix.astype(BF16), wo_ref[...])
    y = xo * lax.rsqrt(jnp.mean(xo * xo, axis=-1, keepdims=True) + EPS)
    y_ref[...] = y * gf_ref[...]


def _final(ua, ub, uc, z, x, wa, wb, wc, wo, g_final, *, tm):
    m, d = x.shape
    const = lambda i: (0, 0)
    resident = dict(pipeline_mode=pl.Buffered(1))
    return pl.pallas_call(
        _final_kernel,
        grid=(m // tm,),
        in_specs=[pl.BlockSpec((tm, ua.shape[1]), lambda i: (i, 0)),
                  pl.BlockSpec((tm, ub.shape[1]), lambda i: (i, 0)),
                  pl.BlockSpec((tm, uc.shape[1]), lambda i: (i, 0)),
                  pl.BlockSpec((tm, 3 * d), lambda i: (i, C_GT // (3 * d))),
                  pl.BlockSpec((tm, d), lambda i: (i, 0)),
                  pl.BlockSpec(wa.shape, const, **resident),
                  pl.BlockSpec(wb.shape, const, **resident),
                  pl.BlockSpec(wc.shape, const, **resident),
                  pl.BlockSpec(wo.shape, const, **resident),
                  pl.BlockSpec((1, d), const)],
        out_specs=pl.BlockSpec((tm, d), lambda i: (i, 0)),
        out_shape=jax.ShapeDtypeStruct((m, d), F32),
        compiler_params=_cparams(("arbitrary",)),
        name="final",
    )(ua, ub, uc, z, x, wa, wb, wc, wo, g_final.reshape(1, d))


def _heads_rows(row, col0):
    return jnp.concatenate([row[:, col0 + h * HD:col0 + (h + 1) * HD] for h in range(N_HEADS)], axis=0)


def _decode_attention(q4, kk, vv, k_new=None, v_new=None):
    s = jnp.sum(kk * q4[None], axis=-1, keepdims=True)
    m = jnp.max(s, axis=0)
    if k_new is not None:
        s_new = jnp.sum(k_new * q4, axis=-1, keepdims=True)
        m = jnp.maximum(m, s_new)
    p = jnp.exp(s - m[None])
    den = jnp.sum(p, axis=0)
    acc = jnp.sum(p * vv, axis=0)
    if k_new is not None:
        p_new = jnp.exp(s_new - m)
        den = den + p_new
        acc = acc + p_new * v_new
    return acc / den, m + jnp.log(den)


def _sample_kernel(*refs):
    (z_ref, za_ref, wa_ref, ba_ref, gg_ref, cos_ref, sin_ref, st_ref, cg0, cg1, cg2, cm_ref,
     ua_ref, ub_ref, uc_ref, sout_ref, nr0, nr1, nr2) = refs
    cg, nr = (cg0, cg1, cg2), (nr0, nr1, nr2)
    b = pl.program_id(0)
    zrow = z_ref[pl.ds(b, 1), :]

    ga8 = jnp.broadcast_to(za_ref[pl.ds(b, 1), :], (8, LANES))
    xa = _dot(ga8.astype(BF16), wa_ref[...])[0:1, :] + ba_ref[...]
    a_row = jnp.exp(_log_sigmoid(xa) / GLA_TAU)
    eye = (lax.broadcasted_iota(jnp.int32, (HD, HD), 0) == lax.broadcasted_iota(jnp.int32, (HD, HD), 1))

    def col(row_vec):
        return jnp.sum(jnp.where(eye, jnp.broadcast_to(row_vec, (HD, HD)), 0.0), axis=1, keepdims=True)

    for h in range(N_HEADS):
        q = zrow[:, C_GQ + h * HD:C_GQ + (h + 1) * HD] * (HD ** -0.5)
        k = zrow[:, C_GK + h * HD:C_GK + (h + 1) * HD]
        v = zrow[:, C_GV + h * GLA_DV:C_GV + (h + 1) * GLA_DV]
        s_new = col(a_row[:, h * HD:(h + 1) * HD]) * st_ref[h] + col(k) * v
        sout_ref[h] = s_new
        o = jnp.sum(col(q) * s_new, axis=0, keepdims=True)
        y = o * lax.rsqrt(jnp.mean(o * o, axis=-1, keepdims=True) + EPS)
        y = y * gg_ref[:, h * GLA_DV:(h + 1) * GLA_DV]
        gate = zrow[:, C_GR + h * GLA_DV:C_GR + (h + 1) * GLA_DV]
        ua_ref[:, h * GLA_DV:(h + 1) * GLA_DV] = y * _silu(gate)

    cos2, sin2 = cos_ref[...], sin_ref[...]
    outs, lses = [], []
    w_grp = N_HEADS * HD
    for g in range(3):
        q4 = _rope(_heads_rows(zrow, C_SQ + g * w_grp), cos2, sin2) * (HD ** -0.5)
        k4 = _rope(_heads_rows(zrow, C_SK + g * w_grp), cos2, sin2)
        v4 = _heads_rows(zrow, C_SV + g * w_grp)
        nr[g][0:N_HEADS, :] = k4
        nr[g][N_HEADS:KV_ROWS, :] = v4
        o, lse = _decode_attention(q4, cg[g][:, 0:N_HEADS, :], cg[g][:, N_HEADS:KV_ROWS, :], k4, v4)
        outs.append(o)
        lses.append(lse)
    mx = functools.reduce(jnp.maximum, lses)
    es = [jnp.exp(l - mx) for l in lses]
    tot = es[0] + es[1] + es[2]
    ob = (es[0] / tot) * outs[0] + (es[1] / tot) * outs[1] + (es[2] / tot) * outs[2]
    for h in range(N_HEADS):
        gate = zrow[:, C_SR + h * HD:C_SR + (h + 1) * HD]
        ub_ref[:, h * HD:(h + 1) * HD] = ob[h:h + 1, :] * _silu(gate)

    qm = _heads_rows(zrow, C_MQ) * (HD ** -0.5)
    oc, _ = _decode_attention(qm, cm_ref[:, 0:N_HEADS, :], cm_ref[:, N_HEADS:KV_ROWS, :])
    for h in range(N_HEADS):
        gate = zrow[:, C_MR + h * HD:C_MR + (h + 1) * HD]
        uc_ref[:, h * HD:(h + 1) * HD] = oc[h:h + 1, :] * _silu(gate)


def _sample_mixers(z, za, wa_pad, b_alpha, g_gla_out, cos2, sin2, state, caches, cache_mem):
    db = z.shape[0]
    kw, vw, w = N_HEADS * HD, N_HEADS * GLA_DV, N_HEADS * HD
    const2 = lambda b: (0, 0)
    gathered, gather_specs = [], []
    for c, (win, dil) in zip(caches, SWA_PATTERNS):
        assert c.shape == (1, db, win, 2, N_HEADS, HD) and PAST_LEN >= win
        n_keys = win // dil
        gathered.append(c.reshape(db, n_keys, dil * KV_ROWS, HD))
        gather_specs.append(pl.BlockSpec((None, n_keys, KV_ROWS, HD), lambda b: (b, 0, 0, 0)))
    row3 = lambda n: pl.BlockSpec((None, 1, n), lambda b: (b, 0, 0))
    new_spec = pl.BlockSpec((None, KV_ROWS, HD), lambda b: (b, 0, 0))
    res = pl.pallas_call(
        _sample_kernel,
        grid=(db,),
        in_specs=[pl.BlockSpec(z.shape, const2), pl.BlockSpec(za.shape, const2),
                  pl.BlockSpec(wa_pad.shape, const2), pl.BlockSpec((1, kw), const2),
                  pl.BlockSpec((1, vw), const2), pl.BlockSpec((1, HD), const2), pl.BlockSpec((1, HD), const2),
                  pl.BlockSpec((None, None, N_HEADS, HD, GLA_DV), lambda b: (0, b, 0, 0, 0))]
                 + gather_specs
                 + [pl.BlockSpec((None, N_MEM, KV_ROWS, HD), lambda b: (b, 0, 0, 0))],
        out_specs=[row3(vw), row3(w), row3(w),
                   pl.BlockSpec((None, None, N_HEADS, HD, GLA_DV), lambda b: (0, b, 0, 0, 0))] + [new_spec] * 3,
        out_shape=[jax.ShapeDtypeStruct((db, 1, vw), F32), jax.ShapeDtypeStruct((db, 1, w), F32),
                   jax.ShapeDtypeStruct((db, 1, w), F32), jax.ShapeDtypeStruct(state.shape, F32)]
                  + [jax.ShapeDtypeStruct((db, KV_ROWS, HD), F32)] * 3,
        compiler_params=_cparams(("arbitrary",)),
        name="sample_mixers",
    )(z, za, wa_pad, b_alpha.reshape(1, kw), g_gla_out.reshape(1, vw), cos2, sin2, state,
      *gathered, cache_mem.reshape(db, N_MEM, KV_ROWS, HD))
    ua, ub, uc, s_out = res[:4]
    return ua.reshape(db, vw), ub.reshape(db, w), uc.reshape(db, w), s_out, res[4:]


def _prep_w_in_kernel(wt_ref, wm_ref, wg_ref, buf, ga_buf, sem, ga_sem, *, c_ga, c_gt, rows):
    i = pl.program_id(0)
    n_gt = (wt_ref.shape[0] - c_gt) // rows
    n_lo = c_ga // rows

    def fetch(ii, slot):
        src = jnp.where(ii < n_gt, c_gt + ii * rows,
                        jnp.where(ii < n_gt + n_lo, (ii - n_gt) * rows, c_ga + GLA_RANK + (ii - n_gt - n_lo) * rows))
        return pltpu.make_async_copy(wt_ref.at[pl.ds(pl.multiple_of(src, GLA_RANK), rows)], buf.at[slot], sem.at[slot])

    ga_copy = pltpu.make_async_copy(wt_ref.at[pl.ds(c_ga, GLA_RANK)], ga_buf, ga_sem)

    @pl.when(i == 0)
    def _():
        fetch(0, 0).start()
        ga_copy.start()

    @pl.when(i + 1 < pl.num_programs(0))
    def _():
        fetch(i + 1, lax.rem(i + 1, 2)).start()

    @pl.when(i == 0)
    def _():
        ga_copy.wait()
        wg_ref[0:GLA_RANK, :] = ga_buf[...].astype(BF16)
        wg_ref[GLA_RANK:, :] = jnp.zeros((LANES - GLA_RANK, wg_ref.shape[1]), BF16)

    slot = lax.rem(i, 2)
    fetch(i, slot).wait()
    wm_ref[...] = buf[slot].astype(BF16)


def _prep_w_in(wt, d, *, rows):
    n_all, k = wt.shape
    c_ga = 2 * N_HEADS * HD + 2 * N_HEADS * GLA_DV
    c_gt = n_all - 3 * d
    assert n_all - GLA_RANK == N_MAIN and k == d
    assert (n_all - c_gt) % rows == 0 and c_ga % rows == 0 and (c_gt - c_ga - GLA_RANK) % rows == 0
    return pl.pallas_call(
        functools.partial(_prep_w_in_kernel, c_ga=c_ga, c_gt=c_gt, rows=rows),
        grid=(N_MAIN // rows,),
        in_specs=[pl.BlockSpec(memory_space=pl.ANY)],
        out_specs=[pl.BlockSpec((rows, d), lambda i: (i, 0)), pl.BlockSpec((LANES, d), lambda i: (0, 0))],
        out_shape=[jax.ShapeDtypeStruct((N_MAIN, d), BF16), jax.ShapeDtypeStruct((LANES, d), BF16)],
        scratch_shapes=[pltpu.VMEM((2, rows, d), F32), pltpu.VMEM((GLA_RANK, d), F32),
                        pltpu.SemaphoreType.DMA((2,)), pltpu.SemaphoreType.DMA(())],
        compiler_params=_cparams(("arbitrary",)),
        name="prep_w_in",
    )(wt)


def kernel(x_prompt, x_sample, mem_prompt, state_gla, cache_swa_w128, cache_swa_w512, cache_swa_w2048, cache_mem_kv, g_norm, w_in, w_alpha2, b_alpha, g_gla_out, g_mem, w_mem_kv, w_proj_a, w_proj_b, w_proj_c, w_out, g_final):
    batch, seq, d = x_prompt.shape
    db, dec_seq, _ = x_sample.shape
    assert g_norm.shape[0] == 1 and dec_seq == 1

    w_main, w_ga = _prep_w_in(jnp.swapaxes(w_in[0], 0, 1), d, rows=512)
    wa_pad = jnp.pad(w_alpha2[0], ((0, LANES - GLA_RANK), (0, 0))).astype(BF16)
    wpa, wpb, wpc, wo = (w[0].astype(BF16) for w in (w_proj_a, w_proj_b, w_proj_c, w_out))

    xp = x_prompt.reshape(batch * seq, d)
    xs = x_sample.reshape(db, d)

    zs, zas = _norm_matmul(xs, g_norm[0], w_main, w_ga, w_rows_out=True, tm=db, tn=2560)
    cos_s, sin_s = _rope_tables(jnp.full((1,), PAST_LEN, jnp.int32))
    caches = (cache_swa_w128, cache_swa_w512, cache_swa_w2048)
    uas, ubs, ucs, gla_s, new_rows = _sample_mixers(
        zs, zas, wa_pad, b_alpha[0], g_gla_out[0], cos_s, sin_s, state_gla, caches, cache_mem_kv)
    y_sample = _final(uas, ubs, ucs, zs, xs, wpa, wpb, wpc, wo, g_final, tm=db).reshape(db, 1, d)

    z, za = _norm_matmul(xp, g_norm[0], w_main, w_ga, w_rows_out=True, out_dtype=BF16, tm=1024, tn=2560)
    flat_caches = [c.reshape(db, w * KV_ROWS, HD) for c, (w, _) in zip(caches, SWA_PATTERNS)]
    ua, gla_p, *shifted = _gla_prompt(z, za, wa_pad, b_alpha[0], g_gla_out[0], batch, seq,
                                      (flat_caches, new_rows), t_blk=512)
    swa_s = [o.reshape(c.shape) for o, c in zip(shifted, caches)]
    mem_kv = _norm_matmul(mem_prompt.reshape(batch * N_MEM, d), g_mem[0], w_mem_kv[0].astype(BF16),
                          tm=batch * N_MEM, tn=512)
    cos_p, sin_p = _rope_tables_range(seq)
    swa_res = _swa_prompt(z, cos_p, sin_p, batch, seq)
    ub, kv_bufs = swa_res[0], swa_res[1:4]
    uc = _memattn_prompt(z, mem_kv, batch, seq, t_blk=1024)
    y_prompt = _final(ua, ub, uc, z, xp, wpa, wpb, wpc, wo, g_final, tm=512).reshape(batch, seq, d)

    swa_p = [kv.reshape(1, batch, w, 2, N_HEADS, HD) for kv, (w, _) in zip(kv_bufs, SWA_PATTERNS)]
    mem_kv_prompt = mem_kv.reshape(1, batch, N_MEM, 2, N_HEADS, HD)
    return (y_prompt, y_sample, gla_p[None], swa_p[0], swa_p[1], swa_p[2], mem_kv_prompt,
            gla_s, swa_s[0], swa_s[1], swa_s[2])
```

```python
import functools

import numpy as np
import jax
import jax.numpy as jnp
from jax import lax
from jax.experimental import pallas as pl
from jax.experimental.pallas import tpu as pltpu

F32 = jnp.float32
BF16 = jnp.bfloat16

EPS = 1e-6
ROPE_THETA = 10000.0
NEG = -1e30
PAST_LEN = 16384

HD = 128
N_HEADS = 4
GLA_DV = 256
GLA_RANK = 16
GLA_TAU = 16.0
LOG2_E = 1.4426950408889634
GLA_CHUNK = 64
GLA_SUB = 16
GLA_UNROLL = 4
SWA_PATTERNS = ((128, 1), (512, 4), (2048, 16))
SWA_BLK = 128
SWA_STEP = 2048
SWA_UNROLL = 2
N_MEM = 256

LANES = 128
VMEM_LIMIT = 60000 * 1024
VMEM_LIMIT_FINAL = 62 * 1024 * 1024

C_GT = 0
C_GQ, C_GK, C_GV, C_GR = 6144, 6656, 7168, 8192
C_SQ, C_SK, C_SV, C_SR = 9216, 10752, 12288, 13824
C_MQ, C_MR = 14336, 14848
N_MAIN = 15360


def _cparams(sem, vmem_limit=VMEM_LIMIT):
    return pltpu.CompilerParams(dimension_semantics=sem, vmem_limit_bytes=vmem_limit)


def _dot(a, b):
    return jnp.dot(a, b, preferred_element_type=F32)


def _dot_nt(a, b):
    return lax.dot_general(a, b, (((1,), (1,)), ((), ())), preferred_element_type=F32)


def _dot_tn(a, b):
    return lax.dot_general(a, b, (((0,), (0,)), ((), ())), preferred_element_type=F32)


def _silu(x):
    return x * jax.nn.sigmoid(x)


def _log_sigmoid(x):
    return jnp.minimum(x, 0.0) - jnp.log1p(jnp.exp(-jnp.abs(x)))


KV_ROWS = 2 * N_HEADS
SHIFT_ROWS = tuple((w - 1) * KV_ROWS for w, _ in SWA_PATTERNS)
SHIFT_OFFS = tuple(sum(SHIFT_ROWS[:g]) for g in range(len(SWA_PATTERNS)))


def _cache_shift_step(step, n_steps, ca, nr, co, stage, in_sem, out_sem, row_sem):
    n_grp = len(ca)
    n_batch = ca[0].shape[0]
    assert n_steps >= n_batch
    slot = lax.rem(step, 2)

    def in_copy(g, bb, sl):
        return pltpu.make_async_copy(ca[g].at[bb, pl.ds(KV_ROWS, SHIFT_ROWS[g])],
                                     stage.at[sl, pl.ds(SHIFT_OFFS[g], SHIFT_ROWS[g])], in_sem.at[sl, g])

    def out_copy(g, bb, sl):
        return pltpu.make_async_copy(stage.at[sl, pl.ds(SHIFT_OFFS[g], SHIFT_ROWS[g])],
                                     co[g].at[bb, pl.ds(0, SHIFT_ROWS[g])], out_sem.at[sl, g])

    def row_copy(g, bb, sl):
        return pltpu.make_async_copy(nr[g].at[bb], co[g].at[bb, pl.ds(SHIFT_ROWS[g], KV_ROWS)], row_sem.at[sl, g])

    def finish_writes(bb, sl):
        for g in range(n_grp):
            out_copy(g, bb, sl).wait()
            row_copy(g, bb, sl).wait()

    @pl.when(step == 0)
    def _():
        for g in range(n_grp):
            in_copy(g, 0, 0).start()

    @pl.when((step >= 1) & (step <= n_batch))
    def _():
        finish_writes(step - 1, 1 - slot)

    @pl.when(step + 1 < n_batch)
    def _():
        for g in range(n_grp):
            in_copy(g, step + 1, 1 - slot).start()

    @pl.when(step < n_batch)
    def _():
        for g in range(n_grp):
            in_copy(g, step, slot).wait()
            out_copy(g, step, slot).start()
            row_copy(g, step, slot).start()

    if n_steps == n_batch:
        @pl.when(step == n_steps - 1)
        def _():
            finish_writes(step, slot)


def _norm_matmul_kernel(*refs, has_extra, row_chunk, w_rows_out):
    mm = _dot_nt if w_rows_out else _dot
    if has_extra:
        x_ref, g_ref, w_ref, wx_ref, o_ref, ox_ref, h_ref = refs
    else:
        x_ref, g_ref, w_ref, o_ref, h_ref = refs
    tm = x_ref.shape[0]

    @pl.when(pl.program_id(1) == 0)
    def _():
        for c in range(tm // row_chunk):
            rs = slice(c * row_chunk, (c + 1) * row_chunk)
            x = x_ref[rs, :]
            y = x * lax.rsqrt(jnp.mean(x * x, axis=-1, keepdims=True) + EPS)
            h_ref[rs, :] = (y * g_ref[...]).astype(BF16)
            if has_extra:
                ox_ref[rs, :] = mm(h_ref[rs, :], wx_ref[...])

    o_ref[...] = mm(h_ref[...], w_ref[...]).astype(o_ref.dtype)


def _norm_matmul(x, g, w, wx=None, *, w_rows_out=False, out_dtype=F32, tm, tn):
    m, d = x.shape
    n = w.shape[0] if w_rows_out else w.shape[1]
    assert m % tm == 0 and n % tn == 0
    has_extra = wx is not None
    row_chunk = min(tm, 128)
    w_spec = pl.BlockSpec((tn, d), lambda i, j: (j, 0)) if w_rows_out else pl.BlockSpec((d, tn), lambda i, j: (0, j))
    in_specs = [pl.BlockSpec((tm, d), lambda i, j: (i, 0)),
                pl.BlockSpec((1, d), lambda i, j: (0, 0)),
                w_spec]
    out_specs = [pl.BlockSpec((tm, tn), lambda i, j: (i, j))]
    out_shape = [jax.ShapeDtypeStruct((m, n), out_dtype)]
    args = [x, g.reshape(1, d), w]
    if has_extra:
        nx = wx.shape[0] if w_rows_out else wx.shape[1]
        in_specs.append(pl.BlockSpec(wx.shape, lambda i, j: (0, 0)))
        out_specs.append(pl.BlockSpec((tm, nx), lambda i, j: (i, 0)))
        out_shape.append(jax.ShapeDtypeStruct((m, nx), F32))
        args.append(wx)
    res = pl.pallas_call(
        functools.partial(_norm_matmul_kernel, has_extra=has_extra, row_chunk=row_chunk, w_rows_out=w_rows_out),
        grid=(m // tm, n // tn),
        in_specs=in_specs, out_specs=out_specs, out_shape=out_shape,
        scratch_shapes=[pltpu.VMEM((tm, d), BF16)],
        compiler_params=_cparams(("arbitrary", "arbitrary")),
        name="norm_matmul",
    )(*args)
    return res if has_extra else res[0]


def _diag_select_matrix():
    rows = np.arange(GLA_SUB * HD)[:, None] // HD
    cols = np.arange(LANES)[None, :] % GLA_SUB
    return jnp.asarray((rows == cols).astype(np.float32), dtype=BF16)


def _gla_chunks(rows_list, zg_ref, ga_ref, wa_ref, ba_ref, gg_ref, em_ref, st_ref, u_ref):
    C, SUB, n_sub = GLA_CHUNK, GLA_SUB, GLA_CHUNK // GLA_SUB
    kw = N_HEADS * HD
    c_k, c_v, c_r = kw, 2 * kw, 2 * kw + N_HEADS * GLA_DV
    row = lax.broadcasted_iota(jnp.int32, (C, C), 0)
    col = lax.broadcasted_iota(jnp.int32, (C, C), 1)
    tri = (col <= row).astype(F32)
    row_c = lax.broadcasted_iota(jnp.int32, (C, HD), 0)
    half = SUB // 2
    half_row = lax.broadcasted_iota(jnp.int32, (half, HD), 0)
    lane_c = lax.broadcasted_iota(jnp.int32, (SUB, C), 1)

    xas = [_dot(ga_ref[rows, :].astype(BF16), wa_ref[...]) + ba_ref[...] for rows in rows_list]
    b_alls = [jnp.dot(tri, _log_sigmoid(xa) * (LOG2_E / GLA_TAU), preferred_element_type=F32,
                      precision=lax.Precision.HIGHEST) for xa in xas]

    chunks = []
    for rows, b_all in zip(rows_list, b_alls):
        a_off = []
        per_head = []
        for h in range(N_HEADS):
            q = zg_ref[rows, h * HD:(h + 1) * HD].astype(F32) * (HD ** -0.5)
            k = zg_ref[rows, c_k + h * HD:c_k + (h + 1) * HD].astype(F32)
            v = zg_ref[rows, c_v + h * GLA_DV:c_v + (h + 1) * GLA_DV]
            b = b_all[:, h * HD:(h + 1) * HD]
            for i in range(n_sub):
                if i == 0:
                    a_off.append(jnp.zeros((SUB, C), F32))
                else:
                    sl = slice(i * SUB, (i + 1) * SUB)
                    b_ref_row = b[i * SUB - 1:i * SUB, :]
                    qs = q[sl] * jnp.exp2(b[sl] - b_ref_row)
                    ks = k * jnp.exp2(jnp.where(row_c < i * SUB, b_ref_row - b, -jnp.inf))
                    a_off.append(_dot_nt(qs.astype(BF16), ks.astype(BF16)))
            st = st_ref[h]
            o_inter = _dot_nt((q * jnp.exp2(b)).astype(BF16), st.astype(BF16))
            b_end = b[C - 1:C, :]
            kd = k * jnp.exp2(b_end - b)
            st_ref[h] = st * jnp.exp2(b_end) + _dot_tn(v.astype(BF16), kd.astype(BF16))
            per_head.append((q, k, v, b, o_inter))
        chunks.append((rows, a_off, per_head))

    p_rows = []
    for _, _, per_head in chunks:
        for h in range(N_HEADS):
            q, k, _, b, _ = per_head[h]
            for i in range(n_sub):
                top, bot = slice(i * SUB, i * SUB + half), slice(i * SUB + half, (i + 1) * SUB)
                b_mid = b[i * SUB + half - 1:i * SUB + half, :]
                qs_bot = q[bot] * jnp.exp2(b[bot] - b_mid)
                ks_top = k[top] * jnp.exp2(b_mid - b[top])
                slabs = []
                for s in range(SUB):
                    rs = top if s < half else bot
                    s0 = s % half
                    e = jnp.where(half_row >= s0, b[rs] - b[rs][s0:s0 + 1, :], -jnp.inf)
                    diag = (q[rs] * k[rs][s0:s0 + 1, :]) * jnp.exp2(e)
                    if s < half:
                        slabs.append(jnp.concatenate([diag, qs_bot * ks_top[s0:s0 + 1, :]], axis=0))
                    else:
                        slabs.append(jnp.concatenate([jnp.zeros((half, HD), F32), diag], axis=0))
                p_rows.append(jnp.concatenate(slabs, axis=1).astype(BF16))
    r_all = _dot(jnp.concatenate(p_rows, axis=0), em_ref[...])

    for ci, (rows, a_off, per_head) in enumerate(chunks):
        for h in range(N_HEADS):
            _, _, v, _, o_inter = per_head[h]
            a_rows = []
            for i in range(n_sub):
                idx = h * n_sub + i
                r0 = (ci * N_HEADS * n_sub + idx) * SUB
                in_blk = (lane_c >= i * SUB) & (lane_c < (i + 1) * SUB)
                a_rows.append(a_off[idx] + jnp.where(in_blk, r_all[r0:r0 + SUB, :C], 0.0))
            a = jnp.concatenate(a_rows, axis=0)
            o = _dot(a.astype(BF16), v.astype(BF16)) + o_inter
            y = o * lax.rsqrt(jnp.mean(o * o, axis=-1, keepdims=True) + EPS)
            y = y * gg_ref[:, h * GLA_DV:(h + 1) * GLA_DV]
            gate = zg_ref[rows, c_r + h * GLA_DV:c_r + (h + 1) * GLA_DV].astype(F32)
            u_ref[rows, h * GLA_DV:(h + 1) * GLA_DV] = (y * _silu(gate)).astype(u_ref.dtype)


def _gla_kernel(*refs, n_shift, n_steps):
    zg_ref, ga_ref, wa_ref, ba_ref, gg_ref, em_ref = refs[:6]
    n_in = 6 + 2 * n_shift
    u_ref, sout_ref = refs[n_in:n_in + 2]
    st_ref = refs[n_in + 2 + n_shift]
    t_blk = pl.program_id(1)
    n_chunks = zg_ref.shape[0] // GLA_CHUNK
    assert n_chunks % GLA_UNROLL == 0

    if n_shift:
        stage, in_sem, out_sem, row_sem = refs[n_in + 3 + n_shift:]
        _cache_shift_step(pl.program_id(0) * pl.num_programs(1) + t_blk, n_steps, refs[6:6 + n_shift],
                          refs[6 + n_shift:n_in], refs[n_in + 2:n_in + 2 + n_shift], stage, in_sem, out_sem, row_sem)

    @pl.when(t_blk == 0)
    def _():
        st_ref[...] = jnp.zeros_like(st_ref)

    def chunks(c, carry):
        rows_list = [pl.ds(pl.multiple_of((c * GLA_UNROLL + u) * GLA_CHUNK, GLA_CHUNK), GLA_CHUNK)
                     for u in range(GLA_UNROLL)]
        _gla_chunks(rows_list, zg_ref, ga_ref, wa_ref, ba_ref, gg_ref, em_ref, st_ref, u_ref)
        return carry

    lax.fori_loop(0, n_chunks // GLA_UNROLL, chunks, 0)

    @pl.when(t_blk == pl.num_programs(1) - 1)
    def _():
        for h in range(N_HEADS):
            sout_ref[h] = st_ref[h].T


def _gla_prompt(z, za, wa_pad, b_alpha, g_gla_out, batch, seq, shift=None, *, t_blk):
    nt = seq // t_blk
    assert seq % t_blk == 0 and t_blk % GLA_CHUNK == 0
    kw = N_HEADS * HD
    vw = N_HEADS * GLA_DV
    gla_w = 2 * kw + 2 * vw
    assert C_GQ % gla_w == 0 and (C_GK, C_GV, C_GR) == (C_GQ + kw, C_GQ + 2 * kw, C_GQ + 2 * kw + vw)
    const = lambda b, t: (0, 0)
    in_specs = [pl.BlockSpec((t_blk, gla_w), lambda b, t: (b * nt + t, C_GQ // gla_w)),
                pl.BlockSpec((t_blk, LANES), lambda b, t: (b * nt + t, 0)),
                pl.BlockSpec((LANES, kw), const), pl.BlockSpec((1, kw), const), pl.BlockSpec((1, vw), const),
                pl.BlockSpec((GLA_SUB * HD, LANES), const)]
    out_specs = [pl.BlockSpec((t_blk, vw), lambda b, t: (b * nt + t, 0)),
                 pl.BlockSpec((None, N_HEADS, HD, GLA_DV), lambda b, t: (b, 0, 0, 0))]
    out_shape = [jax.ShapeDtypeStruct((batch * seq, vw), BF16),
                 jax.ShapeDtypeStruct((batch, N_HEADS, HD, GLA_DV), F32)]
    scratch = [pltpu.VMEM((N_HEADS, GLA_DV, HD), F32)]
    args = [z, za, wa_pad, b_alpha.reshape(1, kw), g_gla_out.reshape(1, vw), _diag_select_matrix()]
    n_shift = 0
    if shift is not None:
        caches, new_rows = shift
        n_shift = len(caches)
        any_spec = pl.BlockSpec(memory_space=pl.ANY)
        in_specs += [any_spec] * n_shift + [pl.BlockSpec(r.shape, lambda b, t: (0, 0, 0)) for r in new_rows]
        out_specs += [any_spec] * n_shift
        out_shape += [jax.ShapeDtypeStruct(c.shape, c.dtype) for c in caches]
        args += list(caches) + list(new_rows)
        scratch += [pltpu.VMEM((2, sum(SHIFT_ROWS), HD), F32)] + [pltpu.SemaphoreType.DMA((2, n_shift))] * 3
    return pl.pallas_call(
        functools.partial(_gla_kernel, n_shift=n_shift, n_steps=batch * nt),
        grid=(batch, nt),
        in_specs=in_specs, out_specs=out_specs, out_shape=out_shape, scratch_shapes=scratch,
        compiler_params=_cparams(("arbitrary", "arbitrary")),
        name="gla",
    )(*args)


def _rope_tables(pos):
    half = HD // 2
    inv = ROPE_THETA ** (-jnp.arange(half, dtype=F32) / half)
    ang = pos.astype(F32)[:, None] * inv[None, :]
    cos, sin = jnp.cos(ang), jnp.sin(ang)
    return jnp.concatenate([cos, cos], axis=1), jnp.concatenate([-sin, sin], axis=1)


def _rope_tables_range(n, blk=128):
    assert n % blk == 0
    half = HD // 2
    inv = ROPE_THETA ** (-jnp.arange(half, dtype=F32) / half)
    inv2 = jnp.concatenate([inv, inv])
    sign = jnp.concatenate([-jnp.ones((half,), F32), jnp.ones((half,), F32)])
    ang_a = (jnp.arange(n // blk, dtype=jnp.int32) * blk).astype(F32)[:, None] * inv2[None, :]
    ang_b = jnp.arange(blk, dtype=jnp.int32).astype(F32)[:, None] * inv2[None, :]
    ca, sa, cb, sb = jnp.cos(ang_a)[:, None], jnp.sin(ang_a)[:, None], jnp.cos(ang_b)[None], jnp.sin(ang_b)[None]
    return (ca * cb - sa * sb).reshape(n, HD), ((sa * cb + ca * sb) * sign).reshape(n, HD)


def _rope(x, cos2, sin2):
    return x * cos2 + pltpu.roll(x, HD // 2, axis=1) * sin2


def _swa_kernel(*refs):
    (q0, q1, q2, k0, k1, k2, v0, v1, v2, sr_ref, cos_ref, sin_ref,
     u_ref, kv0, kv1, kv2,
     q_s, k_s, v_s, o_s, l_s) = refs
    q_in, k_in, v_in = (q0, q1, q2), (k0, k1, k2), (v0, v1, v2)
    kv_out = (kv0, kv1, kv2)
    i = pl.program_id(2)
    T = SWA_STEP
    n_grp = len(SWA_PATTERNS)

    @pl.when(i == 0)
    def _():
        for g, (_, dil) in enumerate(SWA_PATTERNS):
            unit = dil * SWA_BLK
            k_s[g, T - unit:T, :] = jnp.zeros((unit, HD), F32)
            v_s[g, T - unit:T, :] = jnp.zeros((unit, HD), F32)

    @pl.when(i > 0)
    def _():
        for g, (_, dil) in enumerate(SWA_PATTERNS):
            unit = dil * SWA_BLK
            k_s[g, T - unit:T, :] = k_s[g, 2 * T - unit:2 * T, :]
            v_s[g, T - unit:T, :] = v_s[g, 2 * T - unit:2 * T, :]

    cos2, sin2 = cos_ref[...], sin_ref[...]
    for g in range(n_grp):
        q_s[g] = _rope(q_in[g][...].astype(F32), cos2, sin2) * (HD ** -0.5)
        k_s[g, T:2 * T, :] = _rope(k_in[g][...].astype(F32), cos2, sin2)
        v_s[g, T:2 * T, :] = v_in[g][...].astype(F32)

    qi = lax.broadcasted_iota(jnp.int32, (SWA_BLK, 2 * SWA_BLK), 0)
    ki = lax.broadcasted_iota(jnp.int32, (SWA_BLK, 2 * SWA_BLK), 1)
    delta = SWA_BLK + qi - ki

    def blocks(it, carry):
        todo = []
        for g, (win, dil) in enumerate(SWA_PATTERNS):
            for j in range(SWA_UNROLL):
                n = it * SWA_UNROLL + j
                unit = dil * SWA_BLK
                u = lax.shift_right_logical(n, dil.bit_length() - 1)
                r = lax.bitwise_and(n, dil - 1)

                def rows(start, size, dil=dil):
                    return pl.ds(start, size) if dil == 1 else pl.ds(start, size, stride=dil)

                q = q_s[g, rows(u * unit + r, SWA_BLK), :]
                kk = k_s[g, rows(T + (u - 1) * unit + r, 2 * SWA_BLK), :]
                s = _dot_nt(q.astype(BF16), kk.astype(BF16))
                ki_min = jnp.where((i * (T // unit) + u) == 0, SWA_BLK, 0)
                valid = (delta >= 0) & (delta <= win // dil) & (ki >= ki_min)
                todo.append((g, rows(u * unit + r, SWA_BLK), rows(T + (u - 1) * unit + r, 2 * SWA_BLK), s, valid))
        soft = []
        for g, q_rows, kv_rows, s, valid in todo:
            s = jnp.where(valid, s, NEG)
            m = jnp.max(s, axis=-1, keepdims=True)
            p = jnp.exp(s - m)
            den = jnp.sum(p, axis=-1, keepdims=True)
            soft.append((g, q_rows, kv_rows, p.astype(BF16), den, m + jnp.log(den)))
        for g, q_rows, kv_rows, p, den, lse in soft:
            o_s[g, q_rows, :] = _dot(p, v_s[g, kv_rows, :].astype(BF16)) / den
            l_s[g, q_rows, :] = jnp.broadcast_to(lse, (SWA_BLK, HD))
        return carry

    lax.fori_loop(0, T // SWA_BLK // SWA_UNROLL, blocks, 0)

    rc = 256

    def merge(c, carry):
        r0 = pl.multiple_of(c * rc, rc)
        rr = pl.ds(r0, rc)
        ls = [l_s[g, rr, :] for g in range(n_grp)]
        mx = functools.reduce(jnp.maximum, ls)
        es = [jnp.exp(l - mx) for l in ls]
        tot = functools.reduce(lambda a, b: a + b, es)
        ob = functools.reduce(lambda a, b: a + b, [(es[g] / tot) * o_s[g, rr, :] for g in range(n_grp)])
        u_ref[rr, :] = (ob * _silu(sr_ref[rr, :].astype(F32))).astype(u_ref.dtype)
        return carry

    lax.fori_loop(0, T // rc, merge, 0)

    @pl.when(i == pl.num_programs(2) - 1)
    def _():
        for g, (win, _) in enumerate(SWA_PATTERNS):
            head = pl.program_id(1)
            kv_out[g][pl.ds(head, win, stride=KV_ROWS), :] = k_s[g, 2 * T - win:2 * T, :]
            kv_out[g][pl.ds(N_HEADS + head, win, stride=KV_ROWS), :] = v_s[g, 2 * T - win:2 * T, :]


def _swa_prompt(z, cos2, sin2, batch, seq):
    T = SWA_STEP
    assert seq % T == 0 and all(w <= T for w, _ in SWA_PATTERNS)
    nt = seq // T
    w_grp = N_HEADS * HD

    def zspec(col0, g):
        cb0 = (col0 + g * w_grp) // HD
        return pl.BlockSpec((T, HD), lambda b, j, i: (b * nt + i, cb0 + j))

    in_specs = ([zspec(C_SQ, g) for g in range(3)] + [zspec(C_SK, g) for g in range(3)]
                + [zspec(C_SV, g) for g in range(3)] + [zspec(C_SR, 0)]
                + [pl.BlockSpec((T, HD), lambda b, j, i: (i, 0))] * 2)
    buf_specs = [pl.BlockSpec((None, w * KV_ROWS, HD), lambda b, j, i: (b, 0, 0), pipeline_mode=pl.Buffered(1))
                 for w, _ in SWA_PATTERNS]
    buf_shapes = [jax.ShapeDtypeStruct((batch, w * KV_ROWS, HD), F32) for w, _ in SWA_PATTERNS]
    return pl.pallas_call(
        _swa_kernel,
        grid=(batch, N_HEADS, nt),
        in_specs=in_specs,
        out_specs=[pl.BlockSpec((T, HD), lambda b, j, i: (b * nt + i, j))] + buf_specs,
        out_shape=[jax.ShapeDtypeStruct((batch * seq, w_grp), BF16)] + buf_shapes,
        scratch_shapes=[pltpu.VMEM((3, T, HD), F32), pltpu.VMEM((3, 2 * T, HD), F32),
                        pltpu.VMEM((3, 2 * T, HD), F32), pltpu.VMEM((3, T, HD), F32),
                        pltpu.VMEM((3, T, HD), F32)],
        compiler_params=_cparams(("arbitrary", "arbitrary", "arbitrary")),
        name="swa",
    )(*([z] * 10), cos2, sin2)


def _final_kernel(*refs, fuse_mem):
    if fuse_mem:
        ua_ref, ub_ref, mq_ref, mr_ref, kv_ref, gt_ref, x_ref, wa_ref, wb_ref, wc_ref, wo_ref, gf_ref, y_ref = refs
    else:
        ua_ref, ub_ref, uc_ref, gt_ref, x_ref, wa_ref, wb_ref, wc_ref, wo_ref, gf_ref, y_ref = refs
    d = x_ref.shape[1]
    w = N_HEADS * HD
    if fuse_mem:
        scores = [_dot_nt((mq_ref[:, h * HD:(h + 1) * HD].astype(F32) * (HD ** -0.5)).astype(BF16),
                          kv_ref[:, h * HD:(h + 1) * HD].astype(BF16)) for h in range(N_HEADS)]
    ya = _dot(ua_ref[...].astype(BF16), wa_ref[...])
    yb = _dot(ub_ref[...].astype(BF16), wb_ref[...])
    if fuse_mem:
        ucs = []
        for h, s in enumerate(scores):
            e = jnp.exp(s - jnp.max(s, axis=-1, keepdims=True))
            p = e / jnp.sum(e, axis=-1, keepdims=True)
            o = _dot(p.astype(BF16), kv_ref[:, w + h * HD:w + (h + 1) * HD].astype(BF16))
            ucs.append((o * _silu(mr_ref[:, h * HD:(h + 1) * HD].astype(F32))).astype(BF16))
        uc = jnp.concatenate(ucs, axis=1)
    else:
        uc = uc_ref[...].astype(BF16)
    yc = _dot(uc, wc_ref[...])
    mix = (jax.nn.sigmoid(gt_ref[:, 0:d].astype(F32)) * ya + jax.nn.sigmoid(gt_ref[:, d:2 * d].astype(F32)) * yb
           + jax.nn.sigmoid(gt_ref[:, 2 * d:3 * d].astype(F32)) * yc)
    xo = x_ref[...] + _dot(mix.astype(BF16), wo_ref[...])
    y = xo * lax.rsqrt(jnp.mean(xo * xo, axis=-1, keepdims=True) + EPS)
    y_ref[...] = y * gf_ref[...]


def _final(ua, ub, uc, z, x, wa, wb, wc, wo, g_final, *, tm, mem_kv=None, rows_per_batch=None):
    m, d = x.shape
    w = N_HEADS * HD
    const = lambda i: (0, 0)
    resident = dict(pipeline_mode=pl.Buffered(1))
    fuse_mem = uc is None
    if fuse_mem:
        assert rows_per_batch % tm == 0
        mem_specs = [pl.BlockSpec((tm, w), lambda i: (i, C_MQ // w)), pl.BlockSpec((tm, w), lambda i: (i, C_MR // w)),
                     pl.BlockSpec((N_MEM, 2 * w), lambda i: (i // (rows_per_batch // tm), 0))]
        mem_args = [z, z, mem_kv]
    else:
        mem_specs = [pl.BlockSpec((tm, uc.shape[1]), lambda i: (i, 0))]
        mem_args = [uc]
    return pl.pallas_call(
        functools.partial(_final_kernel, fuse_mem=fuse_mem),
        grid=(m // tm,),
        in_specs=[pl.BlockSpec((tm, ua.shape[1]), lambda i: (i, 0)),
                  pl.BlockSpec((tm, ub.shape[1]), lambda i: (i, 0))] + mem_specs + [
                  pl.BlockSpec((tm, 3 * d), lambda i: (i, C_GT // (3 * d))),
                  pl.BlockSpec((tm, d), lambda i: (i, 0)),
                  pl.BlockSpec(wa.shape, const, **resident),
                  pl.BlockSpec(wb.shape, const, **resident),
                  pl.BlockSpec(wc.shape, const, **resident),
                  pl.BlockSpec(wo.shape, const, **resident),
                  pl.BlockSpec((1, d), const)],
        out_specs=pl.BlockSpec((tm, d), lambda i: (i, 0)),
        out_shape=jax.ShapeDtypeStruct((m, d), F32),
        compiler_params=_cparams(("arbitrary",), VMEM_LIMIT_FINAL),
        name="final",
    )(ua, ub, *mem_args, z, x, wa, wb, wc, wo, g_final.reshape(1, d))


def _heads_rows(row, col0):
    return jnp.concatenate([row[:, col0 + h * HD:col0 + (h + 1) * HD] for h in range(N_HEADS)], axis=0)


def _decode_attention(q4, kk, vv, k_new=None, v_new=None):
    s = jnp.sum(kk * q4[None], axis=-1, keepdims=True)
    m = jnp.max(s, axis=0)
    if k_new is not None:
        s_new = jnp.sum(k_new * q4, axis=-1, keepdims=True)
        m = jnp.maximum(m, s_new)
    p = jnp.exp(s - m[None])
    den = jnp.sum(p, axis=0)
    acc = jnp.sum(p * vv, axis=0)
    if k_new is not None:
        p_new = jnp.exp(s_new - m)
        den = den + p_new
        acc = acc + p_new * v_new
    return acc / den, m + jnp.log(den)


def _sample_kernel(*refs):
    (z_ref, za_ref, wa_ref, ba_ref, gg_ref, cos_ref, sin_ref, st_ref, cg0, cg1, cg2, cm_ref,
     ua_ref, ub_ref, uc_ref, sout_ref, nr0, nr1, nr2) = refs
    cg, nr = (cg0, cg1, cg2), (nr0, nr1, nr2)
    b = pl.program_id(0)
    zrow = z_ref[pl.ds(b, 1), :]

    ga8 = jnp.broadcast_to(za_ref[pl.ds(b, 1), :], (8, LANES))
    xa = _dot(ga8.astype(BF16), wa_ref[...])[0:1, :] + ba_ref[...]
    a_row = jnp.exp(_log_sigmoid(xa) / GLA_TAU)
    eye = (lax.broadcasted_iota(jnp.int32, (HD, HD), 0) == lax.broadcasted_iota(jnp.int32, (HD, HD), 1))

    def col(row_vec):
        return jnp.sum(jnp.where(eye, jnp.broadcast_to(row_vec, (HD, HD)), 0.0), axis=1, keepdims=True)

    for h in range(N_HEADS):
        q = zrow[:, C_GQ + h * HD:C_GQ + (h + 1) * HD] * (HD ** -0.5)
        k = zrow[:, C_GK + h * HD:C_GK + (h + 1) * HD]
        v = zrow[:, C_GV + h * GLA_DV:C_GV + (h + 1) * GLA_DV]
        s_new = col(a_row[:, h * HD:(h + 1) * HD]) * st_ref[h] + col(k) * v
        sout_ref[h] = s_new
        o = jnp.sum(col(q) * s_new, axis=0, keepdims=True)
        y = o * lax.rsqrt(jnp.mean(o * o, axis=-1, keepdims=True) + EPS)
        y = y * gg_ref[:, h * GLA_DV:(h + 1) * GLA_DV]
        gate = zrow[:, C_GR + h * GLA_DV:C_GR + (h + 1) * GLA_DV]
        ua_ref[:, h * GLA_DV:(h + 1) * GLA_DV] = y * _silu(gate)

    cos2, sin2 = cos_ref[...], sin_ref[...]
    outs, lses = [], []
    w_grp = N_HEADS * HD
    for g in range(3):
        q4 = _rope(_heads_rows(zrow, C_SQ + g * w_grp), cos2, sin2) * (HD ** -0.5)
        k4 = _rope(_heads_rows(zrow, C_SK + g * w_grp), cos2, sin2)
        v4 = _heads_rows(zrow, C_SV + g * w_grp)
        nr[g][0:N_HEADS, :] = k4
        nr[g][N_HEADS:KV_ROWS, :] = v4
        o, lse = _decode_attention(q4, cg[g][:, 0:N_HEADS, :], cg[g][:, N_HEADS:KV_ROWS, :], k4, v4)
        outs.append(o)
        lses.append(lse)
    mx = functools.reduce(jnp.maximum, lses)
    es = [jnp.exp(l - mx) for l in lses]
    tot = es[0] + es[1] + es[2]
    ob = (es[0] / tot) * outs[0] + (es[1] / tot) * outs[1] + (es[2] / tot) * outs[2]
    for h in range(N_HEADS):
        gate = zrow[:, C_SR + h * HD:C_SR + (h + 1) * HD]
        ub_ref[:, h * HD:(h + 1) * HD] = ob[h:h + 1, :] * _silu(gate)

    qm = _heads_rows(zrow, C_MQ) * (HD ** -0.5)
    oc, _ = _decode_attention(qm, cm_ref[:, 0:N_HEADS, :], cm_ref[:, N_HEADS:KV_ROWS, :])
    for h in range(N_HEADS):
        gate = zrow[:, C_MR + h * HD:C_MR + (h + 1) * HD]
        uc_ref[:, h * HD:(h + 1) * HD] = oc[h:h + 1, :] * _silu(gate)


def _sample_mixers(z, za, wa_pad, b_alpha, g_gla_out, cos2, sin2, state, caches, cache_mem):
    db = z.shape[0]
    kw, vw, w = N_HEADS * HD, N_HEADS * GLA_DV, N_HEADS * HD
    const2 = lambda b: (0, 0)
    gathered, gather_specs = [], []
    for c, (win, dil) in zip(caches, SWA_PATTERNS):
        assert c.shape == (1, db, win, 2, N_HEADS, HD) and PAST_LEN >= win
        n_keys = win // dil
        gathered.append(c.reshape(db, n_keys, dil * KV_ROWS, HD))
        gather_specs.append(pl.BlockSpec((None, n_keys, KV_ROWS, HD), lambda b: (b, 0, 0, 0)))
    row3 = lambda n: pl.BlockSpec((None, 1, n), lambda b: (b, 0, 0))
    new_spec = pl.BlockSpec((None, KV_ROWS, HD), lambda b: (b, 0, 0))
    res = pl.pallas_call(
        _sample_kernel,
        grid=(db,),
        in_specs=[pl.BlockSpec(z.shape, const2), pl.BlockSpec(za.shape, const2),
                  pl.BlockSpec(wa_pad.shape, const2), pl.BlockSpec((1, kw), const2),
                  pl.BlockSpec((1, vw), const2), pl.BlockSpec((1, HD), const2), pl.BlockSpec((1, HD), const2),
                  pl.BlockSpec((None, None, N_HEADS, HD, GLA_DV), lambda b: (0, b, 0, 0, 0))]
                 + gather_specs
                 + [pl.BlockSpec((None, N_MEM, KV_ROWS, HD), lambda b: (b, 0, 0, 0))],
        out_specs=[row3(vw), row3(w), row3(w),
                   pl.BlockSpec((None, None, N_HEADS, HD, GLA_DV), lambda b: (0, b, 0, 0, 0))] + [new_spec] * 3,
        out_shape=[jax.ShapeDtypeStruct((db, 1, vw), F32), jax.ShapeDtypeStruct((db, 1, w), F32),
                   jax.ShapeDtypeStruct((db, 1, w), F32), jax.ShapeDtypeStruct(state.shape, F32)]
                  + [jax.ShapeDtypeStruct((db, KV_ROWS, HD), F32)] * 3,
        compiler_params=_cparams(("arbitrary",)),
        name="sample_mixers",
    )(z, za, wa_pad, b_alpha.reshape(1, kw), g_gla_out.reshape(1, vw), cos2, sin2, state,
      *gathered, cache_mem.reshape(db, N_MEM, KV_ROWS, HD))
    ua, ub, uc, s_out = res[:4]
    return ua.reshape(db, vw), ub.reshape(db, w), uc.reshape(db, w), s_out, res[4:]


def _prep_w_in_kernel(wt_ref, wm_ref, wg_ref, buf, ga_buf, sem, ga_sem, *, c_ga, c_gt, rows):
    i = pl.program_id(0)
    n_gt = (wt_ref.shape[0] - c_gt) // rows
    n_lo = c_ga // rows

    def fetch(ii, slot):
        src = jnp.where(ii < n_gt, c_gt + ii * rows,
                        jnp.where(ii < n_gt + n_lo, (ii - n_gt) * rows, c_ga + GLA_RANK + (ii - n_gt - n_lo) * rows))
        return pltpu.make_async_copy(wt_ref.at[pl.ds(pl.multiple_of(src, GLA_RANK), rows)], buf.at[slot], sem.at[slot])

    ga_copy = pltpu.make_async_copy(wt_ref.at[pl.ds(c_ga, GLA_RANK)], ga_buf, ga_sem)

    @pl.when(i == 0)
    def _():
        fetch(0, 0).start()
        ga_copy.start()

    @pl.when(i + 1 < pl.num_programs(0))
    def _():
        fetch(i + 1, lax.rem(i + 1, 2)).start()

    @pl.when(i == 0)
    def _():
        ga_copy.wait()
        wg_ref[0:GLA_RANK, :] = ga_buf[...].astype(BF16)
        wg_ref[GLA_RANK:, :] = jnp.zeros((LANES - GLA_RANK, wg_ref.shape[1]), BF16)

    slot = lax.rem(i, 2)
    fetch(i, slot).wait()
    wm_ref[...] = buf[slot].astype(BF16)


def _prep_w_in(wt, d, *, rows):
    n_all, k = wt.shape
    c_ga = 2 * N_HEADS * HD + 2 * N_HEADS * GLA_DV
    c_gt = n_all - 3 * d
    assert n_all - GLA_RANK == N_MAIN and k == d
    assert (n_all - c_gt) % rows == 0 and c_ga % rows == 0 and (c_gt - c_ga - GLA_RANK) % rows == 0
    return pl.pallas_call(
        functools.partial(_prep_w_in_kernel, c_ga=c_ga, c_gt=c_gt, rows=rows),
        grid=(N_MAIN // rows,),
        in_specs=[pl.BlockSpec(memory_space=pl.ANY)],
        out_specs=[pl.BlockSpec((rows, d), lambda i: (i, 0)), pl.BlockSpec((LANES, d), lambda i: (0, 0))],
        out_shape=[jax.ShapeDtypeStruct((N_MAIN, d), BF16), jax.ShapeDtypeStruct((LANES, d), BF16)],
        scratch_shapes=[pltpu.VMEM((2, rows, d), F32), pltpu.VMEM((GLA_RANK, d), F32),
                        pltpu.SemaphoreType.DMA((2,)), pltpu.SemaphoreType.DMA(())],
        compiler_params=_cparams(("arbitrary",)),
        name="prep_w_in",
    )(wt)


def kernel(x_prompt, x_sample, mem_prompt, state_gla, cache_swa_w128, cache_swa_w512, cache_swa_w2048, cache_mem_kv, g_norm, w_in, w_alpha2, b_alpha, g_gla_out, g_mem, w_mem_kv, w_proj_a, w_proj_b, w_proj_c, w_out, g_final):
    batch, seq, d = x_prompt.shape
    db, dec_seq, _ = x_sample.shape
    assert g_norm.shape[0] == 1 and dec_seq == 1

    w_main, w_ga = _prep_w_in(jnp.swapaxes(w_in[0], 0, 1), d, rows=512)
    wa_pad = jnp.pad(w_alpha2[0], ((0, LANES - GLA_RANK), (0, 0))).astype(BF16)
    wpa, wpb, wpc, wo = (w[0].astype(BF16) for w in (w_proj_a, w_proj_b, w_proj_c, w_out))

    xp = x_prompt.reshape(batch * seq, d)
    xs = x_sample.reshape(db, d)

    zs, zas = _norm_matmul(xs, g_norm[0], w_main, w_ga, w_rows_out=True, tm=db, tn=2560)
    cos_s, sin_s = _rope_tables(jnp.full((1,), PAST_LEN, jnp.int32))
    caches = (cache_swa_w128, cache_swa_w512, cache_swa_w2048)
    uas, ubs, ucs, gla_s, new_rows = _sample_mixers(
        zs, zas, wa_pad, b_alpha[0], g_gla_out[0], cos_s, sin_s, state_gla, caches, cache_mem_kv)
    y_sample = _final(uas, ubs, ucs, zs, xs, wpa, wpb, wpc, wo, g_final, tm=db).reshape(db, 1, d)

    z, za = _norm_matmul(xp, g_norm[0], w_main, w_ga, w_rows_out=True, out_dtype=BF16, tm=1024, tn=2560)
    flat_caches = [c.reshape(db, w * KV_ROWS, HD) for c, (w, _) in zip(caches, SWA_PATTERNS)]
    ua, gla_p, *shifted = _gla_prompt(z, za, wa_pad, b_alpha[0], g_gla_out[0], batch, seq,
                                      (flat_caches, new_rows), t_blk=512)
    swa_s = [o.reshape(c.shape) for o, c in zip(shifted, caches)]
    mem_kv = _norm_matmul(mem_prompt.reshape(batch * N_MEM, d), g_mem[0], w_mem_kv[0].astype(BF16),
                          tm=batch * N_MEM, tn=512)
    cos_p, sin_p = _rope_tables_range(seq)
    swa_res = _swa_prompt(z, cos_p, sin_p, batch, seq)
    ub, kv_bufs = swa_res[0], swa_res[1:4]
    y_prompt = _final(ua, ub, None, z, xp, wpa, wpb, wpc, wo, g_final, tm=512, mem_kv=mem_kv,
                      rows_per_batch=seq).reshape(batch, seq, d)

    swa_p = [kv.reshape(1, batch, w, 2, N_HEADS, HD) for kv, (w, _) in zip(kv_bufs, SWA_PATTERNS)]
    mem_kv_prompt = mem_kv.reshape(1, batch, N_MEM, 2, N_HEADS, HD)
    return (y_prompt, y_sample, gla_p[None], swa_p[0], swa_p[1], swa_p[2], mem_kv_prompt,
            gla_s, swa_s[0], swa_s[1], swa_s[2])
```

```python
import functools

import numpy as np
import jax
import jax.numpy as jnp
from jax import lax
from jax.experimental import pallas as pl
from jax.experimental.pallas import tpu as pltpu

F32 = jnp.float32
BF16 = jnp.bfloat16

EPS = 1e-6
ROPE_THETA = 10000.0
NEG = -1e30
PAST_LEN = 16384

HD = 128
N_HEADS = 4
GLA_DV = 256
GLA_RANK = 16
GLA_TAU = 16.0
LOG2_E = 1.4426950408889634
GLA_CHUNK = 64
GLA_SUB = 16
GLA_UNROLL = 4
SWA_PATTERNS = ((128, 1), (512, 4), (2048, 16))
SWA_BLK = 128
SWA_STEP = 2048
SWA_UNROLL = 2
N_MEM = 256

LANES = 128
VMEM_LIMIT = 60000 * 1024
VMEM_LIMIT_FINAL = 62 * 1024 * 1024

TILES = dict(prep_rows=512, proj_rows=1024, proj_cols=2560, gla_rows=512, mem_cols=512, final_rows=512)

C_GT = 0
C_GQ, C_GK, C_GV, C_GR = 6144, 6656, 7168, 8192
C_SQ, C_SK, C_SV, C_SR = 9216, 10752, 12288, 13824
C_MQ, C_MR = 14336, 14848
N_MAIN = 15360


def _cparams(sem, vmem_limit=VMEM_LIMIT):
    return pltpu.CompilerParams(dimension_semantics=sem, vmem_limit_bytes=vmem_limit)


def _dot(a, b):
    return jnp.dot(a, b, preferred_element_type=F32)


def _dot_nt(a, b):
    return lax.dot_general(a, b, (((1,), (1,)), ((), ())), preferred_element_type=F32)


def _dot_tn(a, b):
    return lax.dot_general(a, b, (((0,), (0,)), ((), ())), preferred_element_type=F32)


def _silu(x):
    return x * jax.nn.sigmoid(x)


def _log_sigmoid(x):
    return jnp.minimum(x, 0.0) - jnp.log1p(jnp.exp(-jnp.abs(x)))


KV_ROWS = 2 * N_HEADS
SHIFT_ROWS = tuple((w - 1) * KV_ROWS for w, _ in SWA_PATTERNS)
SHIFT_OFFS = tuple(sum(SHIFT_ROWS[:g]) for g in range(len(SWA_PATTERNS)))


def _cache_shift_step(step, n_steps, ca, nr, co, stage, in_sem, out_sem, row_sem):
    n_grp = len(ca)
    n_batch = ca[0].shape[0]
    assert n_steps >= n_batch
    slot = lax.rem(step, 2)

    def in_copy(g, bb, sl):
        return pltpu.make_async_copy(ca[g].at[bb, pl.ds(KV_ROWS, SHIFT_ROWS[g])],
                                     stage.at[sl, pl.ds(SHIFT_OFFS[g], SHIFT_ROWS[g])], in_sem.at[sl, g])

    def out_copy(g, bb, sl):
        return pltpu.make_async_copy(stage.at[sl, pl.ds(SHIFT_OFFS[g], SHIFT_ROWS[g])],
                                     co[g].at[bb, pl.ds(0, SHIFT_ROWS[g])], out_sem.at[sl, g])

    def row_copy(g, bb, sl):
        return pltpu.make_async_copy(nr[g].at[bb], co[g].at[bb, pl.ds(SHIFT_ROWS[g], KV_ROWS)], row_sem.at[sl, g])

    def finish_writes(bb, sl):
        for g in range(n_grp):
            out_copy(g, bb, sl).wait()
            row_copy(g, bb, sl).wait()

    @pl.when(step == 0)
    def _():
        for g in range(n_grp):
            in_copy(g, 0, 0).start()

    @pl.when((step >= 1) & (step <= n_batch))
    def _():
        finish_writes(step - 1, 1 - slot)

    @pl.when(step + 1 < n_batch)
    def _():
        for g in range(n_grp):
            in_copy(g, step + 1, 1 - slot).start()

    @pl.when(step < n_batch)
    def _():
        for g in range(n_grp):
            in_copy(g, step, slot).wait()
            out_copy(g, step, slot).start()
            row_copy(g, step, slot).start()

    if n_steps == n_batch:
        @pl.when(step == n_steps - 1)
        def _():
            finish_writes(step, slot)


def _norm_matmul_kernel(*refs, has_extra, row_chunk, w_rows_out):
    mm = _dot_nt if w_rows_out else _dot
    if has_extra:
        x_ref, g_ref, w_ref, wx_ref, o_ref, ox_ref, h_ref = refs
    else:
        x_ref, g_ref, w_ref, o_ref, h_ref = refs
    tm = x_ref.shape[0]

    @pl.when(pl.program_id(1) == 0)
    def _():
        for c in range(tm // row_chunk):
            rs = slice(c * row_chunk, (c + 1) * row_chunk)
            x = x_ref[rs, :]
            y = x * lax.rsqrt(jnp.mean(x * x, axis=-1, keepdims=True) + EPS)
            h_ref[rs, :] = (y * g_ref[...]).astype(BF16)
            if has_extra:
                ox_ref[rs, :] = mm(h_ref[rs, :], wx_ref[...])

    o_ref[...] = mm(h_ref[...], w_ref[...]).astype(o_ref.dtype)


def _norm_matmul(x, g, w, wx=None, *, w_rows_out=False, out_dtype=F32, tm, tn):
    m, d = x.shape
    n = w.shape[0] if w_rows_out else w.shape[1]
    assert m % tm == 0 and n % tn == 0
    has_extra = wx is not None
    row_chunk = min(tm, 128)
    w_spec = pl.BlockSpec((tn, d), lambda i, j: (j, 0)) if w_rows_out else pl.BlockSpec((d, tn), lambda i, j: (0, j))
    in_specs = [pl.BlockSpec((tm, d), lambda i, j: (i, 0)),
                pl.BlockSpec((1, d), lambda i, j: (0, 0)),
                w_spec]
    out_specs = [pl.BlockSpec((tm, tn), lambda i, j: (i, j))]
    out_shape = [jax.ShapeDtypeStruct((m, n), out_dtype)]
    args = [x, g.reshape(1, d), w]
    if has_extra:
        nx = wx.shape[0] if w_rows_out else wx.shape[1]
        in_specs.append(pl.BlockSpec(wx.shape, lambda i, j: (0, 0)))
        out_specs.append(pl.BlockSpec((tm, nx), lambda i, j: (i, 0)))
        out_shape.append(jax.ShapeDtypeStruct((m, nx), F32))
        args.append(wx)
    res = pl.pallas_call(
        functools.partial(_norm_matmul_kernel, has_extra=has_extra, row_chunk=row_chunk, w_rows_out=w_rows_out),
        grid=(m // tm, n // tn),
        in_specs=in_specs, out_specs=out_specs, out_shape=out_shape,
        scratch_shapes=[pltpu.VMEM((tm, d), BF16)],
        compiler_params=_cparams(("arbitrary", "arbitrary")),
        name="norm_matmul",
    )(*args)
    return res if has_extra else res[0]


def _diag_select_matrix():
    rows = np.arange(GLA_SUB * HD)[:, None] // HD
    cols = np.arange(LANES)[None, :] % GLA_SUB
    return jnp.asarray((rows == cols).astype(np.float32), dtype=BF16)


def _gla_chunks(rows_list, zg_ref, ga_ref, wa_ref, ba_ref, gg_ref, em_ref, st_ref, u_ref):
    C, SUB, n_sub = GLA_CHUNK, GLA_SUB, GLA_CHUNK // GLA_SUB
    kw = N_HEADS * HD
    c_k, c_v, c_r = kw, 2 * kw, 2 * kw + N_HEADS * GLA_DV
    row = lax.broadcasted_iota(jnp.int32, (C, C), 0)
    col = lax.broadcasted_iota(jnp.int32, (C, C), 1)
    tri = (col <= row).astype(F32)
    row_c = lax.broadcasted_iota(jnp.int32, (C, HD), 0)
    half = SUB // 2
    half_row = lax.broadcasted_iota(jnp.int32, (half, HD), 0)
    lane_c = lax.broadcasted_iota(jnp.int32, (SUB, C), 1)

    xas = [_dot(ga_ref[rows, :].astype(BF16), wa_ref[...]) + ba_ref[...] for rows in rows_list]
    b_alls = [jnp.dot(tri, _log_sigmoid(xa) * (LOG2_E / GLA_TAU), preferred_element_type=F32,
                      precision=lax.Precision.HIGHEST) for xa in xas]

    chunks = []
    for rows, b_all in zip(rows_list, b_alls):
        a_off = []
        per_head = []
        for h in range(N_HEADS):
            q = zg_ref[rows, h * HD:(h + 1) * HD].astype(F32) * (HD ** -0.5)
            k = zg_ref[rows, c_k + h * HD:c_k + (h + 1) * HD].astype(F32)
            v = zg_ref[rows, c_v + h * GLA_DV:c_v + (h + 1) * GLA_DV]
            b = b_all[:, h * HD:(h + 1) * HD]
            for i in range(n_sub):
                if i == 0:
                    a_off.append(jnp.zeros((SUB, C), F32))
                else:
                    sl = slice(i * SUB, (i + 1) * SUB)
                    b_ref_row = b[i * SUB - 1:i * SUB, :]
                    qs = q[sl] * jnp.exp2(b[sl] - b_ref_row)
                    ks = k * jnp.exp2(jnp.where(row_c < i * SUB, b_ref_row - b, -jnp.inf))
                    a_off.append(_dot_nt(qs.astype(BF16), ks.astype(BF16)))
            st = st_ref[h]
            o_inter = _dot_nt((q * jnp.exp2(b)).astype(BF16), st.astype(BF16))
            b_end = b[C - 1:C, :]
            kd = k * jnp.exp2(b_end - b)
            st_ref[h] = st * jnp.exp2(b_end) + _dot_tn(v.astype(BF16), kd.astype(BF16))
            per_head.append((q, k, v, b, o_inter))
        chunks.append((rows, a_off, per_head))

    p_rows = []
    for _, _, per_head in chunks:
        for h in range(N_HEADS):
            q, k, _, b, _ = per_head[h]
            for i in range(n_sub):
                top, bot = slice(i * SUB, i * SUB + half), slice(i * SUB + half, (i + 1) * SUB)
                b_mid = b[i * SUB + half - 1:i * SUB + half, :]
                qs_bot = q[bot] * jnp.exp2(b[bot] - b_mid)
                ks_top = k[top] * jnp.exp2(b_mid - b[top])
                slabs = []
                for s in range(SUB):
                    rs = top if s < half else bot
                    s0 = s % half
                    e = jnp.where(half_row >= s0, b[rs] - b[rs][s0:s0 + 1, :], -jnp.inf)
                    diag = (q[rs] * k[rs][s0:s0 + 1, :]) * jnp.exp2(e)
                    if s < half:
                        slabs.append(jnp.concatenate([diag, qs_bot * ks_top[s0:s0 + 1, :]], axis=0))
                    else:
                        slabs.append(jnp.concatenate([jnp.zeros((half, HD), F32), diag], axis=0))
                p_rows.append(jnp.concatenate(slabs, axis=1).astype(BF16))
    r_all = _dot(jnp.concatenate(p_rows, axis=0), em_ref[...])

    for ci, (rows, a_off, per_head) in enumerate(chunks):
        for h in range(N_HEADS):
            _, _, v, _, o_inter = per_head[h]
            a_rows = []
            for i in range(n_sub):
                idx = h * n_sub + i
                r0 = (ci * N_HEADS * n_sub + idx) * SUB
                in_blk = (lane_c >= i * SUB) & (lane_c < (i + 1) * SUB)
                a_rows.append(a_off[idx] + jnp.where(in_blk, r_all[r0:r0 + SUB, :C], 0.0))
            a = jnp.concatenate(a_rows, axis=0)
            o = _dot(a.astype(BF16), v.astype(BF16)) + o_inter
            y = o * lax.rsqrt(jnp.mean(o * o, axis=-1, keepdims=True) + EPS)
            y = y * gg_ref[:, h * GLA_DV:(h + 1) * GLA_DV]
            gate = zg_ref[rows, c_r + h * GLA_DV:c_r + (h + 1) * GLA_DV].astype(F32)
            u_ref[rows, h * GLA_DV:(h + 1) * GLA_DV] = (y * _silu(gate)).astype(u_ref.dtype)


def _gla_kernel(*refs, n_shift, n_steps):
    zg_ref, ga_ref, wa_ref, ba_ref, gg_ref, em_ref = refs[:6]
    n_in = 6 + 2 * n_shift
    u_ref, sout_ref = refs[n_in:n_in + 2]
    st_ref = refs[n_in + 2 + n_shift]
    t_blk = pl.program_id(1)
    n_chunks = zg_ref.shape[0] // GLA_CHUNK
    assert n_chunks % GLA_UNROLL == 0

    if n_shift:
        stage, in_sem, out_sem, row_sem = refs[n_in + 3 + n_shift:]
        _cache_shift_step(pl.program_id(0) * pl.num_programs(1) + t_blk, n_steps, refs[6:6 + n_shift],
                          refs[6 + n_shift:n_in], refs[n_in + 2:n_in + 2 + n_shift], stage, in_sem, out_sem, row_sem)

    @pl.when(t_blk == 0)
    def _():
        st_ref[...] = jnp.zeros_like(st_ref)

    def chunks(c, carry):
        rows_list = [pl.ds(pl.multiple_of((c * GLA_UNROLL + u) * GLA_CHUNK, GLA_CHUNK), GLA_CHUNK)
                     for u in range(GLA_UNROLL)]
        _gla_chunks(rows_list, zg_ref, ga_ref, wa_ref, ba_ref, gg_ref, em_ref, st_ref, u_ref)
        return carry

    lax.fori_loop(0, n_chunks // GLA_UNROLL, chunks, 0)

    @pl.when(t_blk == pl.num_programs(1) - 1)
    def _():
        for h in range(N_HEADS):
            sout_ref[h] = st_ref[h].T


def _gla_prompt(z, za, wa_pad, b_alpha, g_gla_out, batch, seq, shift=None, *, t_blk):
    nt = seq // t_blk
    assert seq % t_blk == 0 and t_blk % GLA_CHUNK == 0
    kw = N_HEADS * HD
    vw = N_HEADS * GLA_DV
    gla_w = 2 * kw + 2 * vw
    assert C_GQ % gla_w == 0 and (C_GK, C_GV, C_GR) == (C_GQ + kw, C_GQ + 2 * kw, C_GQ + 2 * kw + vw)
    const = lambda b, t: (0, 0)
    in_specs = [pl.BlockSpec((t_blk, gla_w), lambda b, t: (b * nt + t, C_GQ // gla_w)),
                pl.BlockSpec((t_blk, LANES), lambda b, t: (b * nt + t, 0)),
                pl.BlockSpec((LANES, kw), const), pl.BlockSpec((1, kw), const), pl.BlockSpec((1, vw), const),
                pl.BlockSpec((GLA_SUB * HD, LANES), const)]
    out_specs = [pl.BlockSpec((t_blk, vw), lambda b, t: (b * nt + t, 0)),
                 pl.BlockSpec((None, N_HEADS, HD, GLA_DV), lambda b, t: (b, 0, 0, 0))]
    out_shape = [jax.ShapeDtypeStruct((batch * seq, vw), BF16),
                 jax.ShapeDtypeStruct((batch, N_HEADS, HD, GLA_DV), F32)]
    scratch = [pltpu.VMEM((N_HEADS, GLA_DV, HD), F32)]
    args = [z, za, wa_pad, b_alpha.reshape(1, kw), g_gla_out.reshape(1, vw), _diag_select_matrix()]
    n_shift = 0
    if shift is not None:
        caches, new_rows = shift
        n_shift = len(caches)
        any_spec = pl.BlockSpec(memory_space=pl.ANY)
        in_specs += [any_spec] * n_shift + [pl.BlockSpec(r.shape, lambda b, t: (0, 0, 0)) for r in new_rows]
        out_specs += [any_spec] * n_shift
        out_shape += [jax.ShapeDtypeStruct(c.shape, c.dtype) for c in caches]
        args += list(caches) + list(new_rows)
        scratch += [pltpu.VMEM((2, sum(SHIFT_ROWS), HD), F32)] + [pltpu.SemaphoreType.DMA((2, n_shift))] * 3
    return pl.pallas_call(
        functools.partial(_gla_kernel, n_shift=n_shift, n_steps=batch * nt),
        grid=(batch, nt),
        in_specs=in_specs, out_specs=out_specs, out_shape=out_shape, scratch_shapes=scratch,
        compiler_params=_cparams(("arbitrary", "arbitrary")),
        name="gla",
    )(*args)


def _rope_tables(pos):
    half = HD // 2
    inv = ROPE_THETA ** (-jnp.arange(half, dtype=F32) / half)
    ang = pos.astype(F32)[:, None] * inv[None, :]
    cos, sin = jnp.cos(ang), jnp.sin(ang)
    return jnp.concatenate([cos, cos], axis=1), jnp.concatenate([-sin, sin], axis=1)


def _rope_tables_range(n, blk=128):
    assert n % blk == 0
    half = HD // 2
    inv = ROPE_THETA ** (-jnp.arange(half, dtype=F32) / half)
    inv2 = jnp.concatenate([inv, inv])
    sign = jnp.concatenate([-jnp.ones((half,), F32), jnp.ones((half,), F32)])
    ang_a = (jnp.arange(n // blk, dtype=jnp.int32) * blk).astype(F32)[:, None] * inv2[None, :]
    ang_b = jnp.arange(blk, dtype=jnp.int32).astype(F32)[:, None] * inv2[None, :]
    ca, sa, cb, sb = jnp.cos(ang_a)[:, None], jnp.sin(ang_a)[:, None], jnp.cos(ang_b)[None], jnp.sin(ang_b)[None]
    return (ca * cb - sa * sb).reshape(n, HD), ((sa * cb + ca * sb) * sign).reshape(n, HD)


def _rope(x, cos2, sin2):
    return x * cos2 + pltpu.roll(x, HD // 2, axis=1) * sin2


def _swa_kernel(*refs):
    (q0, q1, q2, k0, k1, k2, v0, v1, v2, sr_ref, cos_ref, sin_ref,
     u_ref, kv0, kv1, kv2,
     q_s, k_s, v_s, o_s, l_s) = refs
    q_in, k_in, v_in = (q0, q1, q2), (k0, k1, k2), (v0, v1, v2)
    kv_out = (kv0, kv1, kv2)
    i = pl.program_id(2)
    T = SWA_STEP
    n_grp = len(SWA_PATTERNS)

    @pl.when(i == 0)
    def _():
        for g, (_, dil) in enumerate(SWA_PATTERNS):
            unit = dil * SWA_BLK
            k_s[g, T - unit:T, :] = jnp.zeros((unit, HD), F32)
            v_s[g, T - unit:T, :] = jnp.zeros((unit, HD), F32)

    @pl.when(i > 0)
    def _():
        for g, (_, dil) in enumerate(SWA_PATTERNS):
            unit = dil * SWA_BLK
            k_s[g, T - unit:T, :] = k_s[g, 2 * T - unit:2 * T, :]
            v_s[g, T - unit:T, :] = v_s[g, 2 * T - unit:2 * T, :]

    cos2, sin2 = cos_ref[...], sin_ref[...]
    for g in range(n_grp):
        q_s[g] = _rope(q_in[g][...].astype(F32), cos2, sin2) * (HD ** -0.5)
        k_s[g, T:2 * T, :] = _rope(k_in[g][...].astype(F32), cos2, sin2)
        v_s[g, T:2 * T, :] = v_in[g][...].astype(F32)

    qi = lax.broadcasted_iota(jnp.int32, (SWA_BLK, 2 * SWA_BLK), 0)
    ki = lax.broadcasted_iota(jnp.int32, (SWA_BLK, 2 * SWA_BLK), 1)
    delta = SWA_BLK + qi - ki

    def blocks(it, carry):
        todo = []
        for g, (win, dil) in enumerate(SWA_PATTERNS):
            for j in range(SWA_UNROLL):
                n = it * SWA_UNROLL + j
                unit = dil * SWA_BLK
                u = lax.shift_right_logical(n, dil.bit_length() - 1)
                r = lax.bitwise_and(n, dil - 1)

                def rows(start, size, dil=dil):
                    return pl.ds(start, size) if dil == 1 else pl.ds(start, size, stride=dil)

                q = q_s[g, rows(u * unit + r, SWA_BLK), :]
                kk = k_s[g, rows(T + (u - 1) * unit + r, 2 * SWA_BLK), :]
                s = _dot_nt(q.astype(BF16), kk.astype(BF16))
                ki_min = jnp.where((i * (T // unit) + u) == 0, SWA_BLK, 0)
                valid = (delta >= 0) & (delta <= win // dil) & (ki >= ki_min)
                todo.append((g, rows(u * unit + r, SWA_BLK), rows(T + (u - 1) * unit + r, 2 * SWA_BLK), s, valid))
        soft = []
        for g, q_rows, kv_rows, s, valid in todo:
            s = jnp.where(valid, s, NEG)
            m = jnp.max(s, axis=-1, keepdims=True)
            p = jnp.exp(s - m)
            den = jnp.sum(p, axis=-1, keepdims=True)
            soft.append((g, q_rows, kv_rows, p.astype(BF16), den, m + jnp.log(den)))
        for g, q_rows, kv_rows, p, den, lse in soft:
            o_s[g, q_rows, :] = _dot(p, v_s[g, kv_rows, :].astype(BF16)) / den
            l_s[g, q_rows, :] = jnp.broadcast_to(lse, (SWA_BLK, HD))
        return carry

    lax.fori_loop(0, T // SWA_BLK // SWA_UNROLL, blocks, 0)

    rc = 256

    def merge(c, carry):
        r0 = pl.multiple_of(c * rc, rc)
        rr = pl.ds(r0, rc)
        ls = [l_s[g, rr, :] for g in range(n_grp)]
        mx = functools.reduce(jnp.maximum, ls)
        es = [jnp.exp(l - mx) for l in ls]
        tot = functools.reduce(lambda a, b: a + b, es)
        ob = functools.reduce(lambda a, b: a + b, [(es[g] / tot) * o_s[g, rr, :] for g in range(n_grp)])
        u_ref[rr, :] = (ob * _silu(sr_ref[rr, :].astype(F32))).astype(u_ref.dtype)
        return carry

    lax.fori_loop(0, T // rc, merge, 0)

    @pl.when(i == pl.num_programs(2) - 1)
    def _():
        for g, (win, _) in enumerate(SWA_PATTERNS):
            head = pl.program_id(1)
            kv_out[g][pl.ds(head, win, stride=KV_ROWS), :] = k_s[g, 2 * T - win:2 * T, :]
            kv_out[g][pl.ds(N_HEADS + head, win, stride=KV_ROWS), :] = v_s[g, 2 * T - win:2 * T, :]


def _swa_prompt(z, cos2, sin2, batch, seq):
    T = SWA_STEP
    assert seq % T == 0 and all(w <= T for w, _ in SWA_PATTERNS)
    nt = seq // T
    w_grp = N_HEADS * HD

    def zspec(col0, g):
        cb0 = (col0 + g * w_grp) // HD
        return pl.BlockSpec((T, HD), lambda b, j, i: (b * nt + i, cb0 + j))

    in_specs = ([zspec(C_SQ, g) for g in range(3)] + [zspec(C_SK, g) for g in range(3)]
                + [zspec(C_SV, g) for g in range(3)] + [zspec(C_SR, 0)]
                + [pl.BlockSpec((T, HD), lambda b, j, i: (i, 0))] * 2)
    buf_specs = [pl.BlockSpec((None, w * KV_ROWS, HD), lambda b, j, i: (b, 0, 0), pipeline_mode=pl.Buffered(1))
                 for w, _ in SWA_PATTERNS]
    buf_shapes = [jax.ShapeDtypeStruct((batch, w * KV_ROWS, HD), F32) for w, _ in SWA_PATTERNS]
    return pl.pallas_call(
        _swa_kernel,
        grid=(batch, N_HEADS, nt),
        in_specs=in_specs,
        out_specs=[pl.BlockSpec((T, HD), lambda b, j, i: (b * nt + i, j))] + buf_specs,
        out_shape=[jax.ShapeDtypeStruct((batch * seq, w_grp), BF16)] + buf_shapes,
        scratch_shapes=[pltpu.VMEM((3, T, HD), F32), pltpu.VMEM((3, 2 * T, HD), F32),
                        pltpu.VMEM((3, 2 * T, HD), F32), pltpu.VMEM((3, T, HD), F32),
                        pltpu.VMEM((3, T, HD), F32)],
        compiler_params=_cparams(("arbitrary", "arbitrary", "arbitrary")),
        name="swa",
    )(*([z] * 10), cos2, sin2)


def _final_kernel(*refs, fuse_mem):
    if fuse_mem:
        ua_ref, ub_ref, mq_ref, mr_ref, kv_ref, gt_ref, x_ref, wa_ref, wb_ref, wc_ref, wo_ref, gf_ref, y_ref = refs
    else:
        ua_ref, ub_ref, uc_ref, gt_ref, x_ref, wa_ref, wb_ref, wc_ref, wo_ref, gf_ref, y_ref = refs
    d = x_ref.shape[1]
    w = N_HEADS * HD
    if fuse_mem:
        scores = [_dot_nt((mq_ref[:, h * HD:(h + 1) * HD].astype(F32) * (HD ** -0.5)).astype(BF16),
                          kv_ref[:, h * HD:(h + 1) * HD].astype(BF16)) for h in range(N_HEADS)]
    ya = _dot(ua_ref[...].astype(BF16), wa_ref[...])
    yb = _dot(ub_ref[...].astype(BF16), wb_ref[...])
    if fuse_mem:
        ucs = []
        for h, s in enumerate(scores):
            e = jnp.exp(s - jnp.max(s, axis=-1, keepdims=True))
            p = e / jnp.sum(e, axis=-1, keepdims=True)
            o = _dot(p.astype(BF16), kv_ref[:, w + h * HD:w + (h + 1) * HD].astype(BF16))
            ucs.append((o * _silu(mr_ref[:, h * HD:(h + 1) * HD].astype(F32))).astype(BF16))
        uc = jnp.concatenate(ucs, axis=1)
    else:
        uc = uc_ref[...].astype(BF16)
    yc = _dot(uc, wc_ref[...])
    mix = (jax.nn.sigmoid(gt_ref[:, 0:d].astype(F32)) * ya + jax.nn.sigmoid(gt_ref[:, d:2 * d].astype(F32)) * yb
           + jax.nn.sigmoid(gt_ref[:, 2 * d:3 * d].astype(F32)) * yc)
    xo = x_ref[...] + _dot(mix.astype(BF16), wo_ref[...])
    y = xo * lax.rsqrt(jnp.mean(xo * xo, axis=-1, keepdims=True) + EPS)
    y_ref[...] = y * gf_ref[...]


def _final(ua, ub, uc, z, x, wa, wb, wc, wo, g_final, *, tm, mem_kv=None, rows_per_batch=None):
    m, d = x.shape
    w = N_HEADS * HD
    const = lambda i: (0, 0)
    resident = dict(pipeline_mode=pl.Buffered(1))
    fuse_mem = uc is None
    if fuse_mem:
        assert rows_per_batch % tm == 0
        mem_specs = [pl.BlockSpec((tm, w), lambda i: (i, C_MQ // w)), pl.BlockSpec((tm, w), lambda i: (i, C_MR // w)),
                     pl.BlockSpec((N_MEM, 2 * w), lambda i: (i // (rows_per_batch // tm), 0))]
        mem_args = [z, z, mem_kv]
    else:
        mem_specs = [pl.BlockSpec((tm, uc.shape[1]), lambda i: (i, 0))]
        mem_args = [uc]
    return pl.pallas_call(
        functools.partial(_final_kernel, fuse_mem=fuse_mem),
        grid=(m // tm,),
        in_specs=[pl.BlockSpec((tm, ua.shape[1]), lambda i: (i, 0)),
                  pl.BlockSpec((tm, ub.shape[1]), lambda i: (i, 0))] + mem_specs + [
                  pl.BlockSpec((tm, 3 * d), lambda i: (i, C_GT // (3 * d))),
                  pl.BlockSpec((tm, d), lambda i: (i, 0)),
                  pl.BlockSpec(wa.shape, const, **resident),
                  pl.BlockSpec(wb.shape, const, **resident),
                  pl.BlockSpec(wc.shape, const, **resident),
                  pl.BlockSpec(wo.shape, const, **resident),
                  pl.BlockSpec((1, d), const)],
        out_specs=pl.BlockSpec((tm, d), lambda i: (i, 0)),
        out_shape=jax.ShapeDtypeStruct((m, d), F32),
        compiler_params=_cparams(("arbitrary",), VMEM_LIMIT_FINAL),
        name="final",
    )(ua, ub, *mem_args, z, x, wa, wb, wc, wo, g_final.reshape(1, d))


def _heads_rows(row, col0):
    return jnp.concatenate([row[:, col0 + h * HD:col0 + (h + 1) * HD] for h in range(N_HEADS)], axis=0)


def _decode_attention(q4, kk, vv, k_new=None, v_new=None):
    s = jnp.sum(kk * q4[None], axis=-1, keepdims=True)
    m = jnp.max(s, axis=0)
    if k_new is not None:
        s_new = jnp.sum(k_new * q4, axis=-1, keepdims=True)
        m = jnp.maximum(m, s_new)
    p = jnp.exp(s - m[None])
    den = jnp.sum(p, axis=0)
    acc = jnp.sum(p * vv, axis=0)
    if k_new is not None:
        p_new = jnp.exp(s_new - m)
        den = den + p_new
        acc = acc + p_new * v_new
    return acc / den, m + jnp.log(den)


def _sample_kernel(*refs):
    (z_ref, za_ref, wa_ref, ba_ref, gg_ref, cos_ref, sin_ref, st_ref, cg0, cg1, cg2, cm_ref,
     ua_ref, ub_ref, uc_ref, sout_ref, nr0, nr1, nr2) = refs
    cg, nr = (cg0, cg1, cg2), (nr0, nr1, nr2)
    b = pl.program_id(0)
    zrow = z_ref[pl.ds(b, 1), :]

    ga8 = jnp.broadcast_to(za_ref[pl.ds(b, 1), :], (8, LANES))
    xa = _dot(ga8.astype(BF16), wa_ref[...])[0:1, :] + ba_ref[...]
    a_row = jnp.exp(_log_sigmoid(xa) / GLA_TAU)
    eye = (lax.broadcasted_iota(jnp.int32, (HD, HD), 0) == lax.broadcasted_iota(jnp.int32, (HD, HD), 1))

    def col(row_vec):
        return jnp.sum(jnp.where(eye, jnp.broadcast_to(row_vec, (HD, HD)), 0.0), axis=1, keepdims=True)

    for h in range(N_HEADS):
        q = zrow[:, C_GQ + h * HD:C_GQ + (h + 1) * HD] * (HD ** -0.5)
        k = zrow[:, C_GK + h * HD:C_GK + (h + 1) * HD]
        v = zrow[:, C_GV + h * GLA_DV:C_GV + (h + 1) * GLA_DV]
        s_new = col(a_row[:, h * HD:(h + 1) * HD]) * st_ref[h] + col(k) * v
        sout_ref[h] = s_new
        o = jnp.sum(col(q) * s_new, axis=0, keepdims=True)
        y = o * lax.rsqrt(jnp.mean(o * o, axis=-1, keepdims=True) + EPS)
        y = y * gg_ref[:, h * GLA_DV:(h + 1) * GLA_DV]
        gate = zrow[:, C_GR + h * GLA_DV:C_GR + (h + 1) * GLA_DV]
        ua_ref[:, h * GLA_DV:(h + 1) * GLA_DV] = y * _silu(gate)

    cos2, sin2 = cos_ref[...], sin_ref[...]
    outs, lses = [], []
    w_grp = N_HEADS * HD
    for g in range(3):
        q4 = _rope(_heads_rows(zrow, C_SQ + g * w_grp), cos2, sin2) * (HD ** -0.5)
        k4 = _rope(_heads_rows(zrow, C_SK + g * w_grp), cos2, sin2)
        v4 = _heads_rows(zrow, C_SV + g * w_grp)
        nr[g][0:N_HEADS, :] = k4
        nr[g][N_HEADS:KV_ROWS, :] = v4
        o, lse = _decode_attention(q4, cg[g][:, 0:N_HEADS, :], cg[g][:, N_HEADS:KV_ROWS, :], k4, v4)
        outs.append(o)
        lses.append(lse)
    mx = functools.reduce(jnp.maximum, lses)
    es = [jnp.exp(l - mx) for l in lses]
    tot = es[0] + es[1] + es[2]
    ob = (es[0] / tot) * outs[0] + (es[1] / tot) * outs[1] + (es[2] / tot) * outs[2]
    for h in range(N_HEADS):
        gate = zrow[:, C_SR + h * HD:C_SR + (h + 1) * HD]
        ub_ref[:, h * HD:(h + 1) * HD] = ob[h:h + 1, :] * _silu(gate)

    qm = _heads_rows(zrow, C_MQ) * (HD ** -0.5)
    oc, _ = _decode_attention(qm, cm_ref[:, 0:N_HEADS, :], cm_ref[:, N_HEADS:KV_ROWS, :])
    for h in range(N_HEADS):
        gate = zrow[:, C_MR + h * HD:C_MR + (h + 1) * HD]
        uc_ref[:, h * HD:(h + 1) * HD] = oc[h:h + 1, :] * _silu(gate)


def _sample_mixers(z, za, wa_pad, b_alpha, g_gla_out, cos2, sin2, state, caches, cache_mem):
    db = z.shape[0]
    kw, vw, w = N_HEADS * HD, N_HEADS * GLA_DV, N_HEADS * HD
    const2 = lambda b: (0, 0)
    gathered, gather_specs = [], []
    for c, (win, dil) in zip(caches, SWA_PATTERNS):
        assert c.shape == (1, db, win, 2, N_HEADS, HD) and PAST_LEN >= win
        n_keys = win // dil
        gathered.append(c.reshape(db, n_keys, dil * KV_ROWS, HD))
        gather_specs.append(pl.BlockSpec((None, n_keys, KV_ROWS, HD), lambda b: (b, 0, 0, 0)))
    row3 = lambda n: pl.BlockSpec((None, 1, n), lambda b: (b, 0, 0))
    new_spec = pl.BlockSpec((None, KV_ROWS, HD), lambda b: (b, 0, 0))
    res = pl.pallas_call(
        _sample_kernel,
        grid=(db,),
        in_specs=[pl.BlockSpec(z.shape, const2), pl.BlockSpec(za.shape, const2),
                  pl.BlockSpec(wa_pad.shape, const2), pl.BlockSpec((1, kw), const2),
                  pl.BlockSpec((1, vw), const2), pl.BlockSpec((1, HD), const2), pl.BlockSpec((1, HD), const2),
                  pl.BlockSpec((None, None, N_HEADS, HD, GLA_DV), lambda b: (0, b, 0, 0, 0))]
                 + gather_specs
                 + [pl.BlockSpec((None, N_MEM, KV_ROWS, HD), lambda b: (b, 0, 0, 0))],
        out_specs=[row3(vw), row3(w), row3(w),
                   pl.BlockSpec((None, None, N_HEADS, HD, GLA_DV), lambda b: (0, b, 0, 0, 0))] + [new_spec] * 3,
        out_shape=[jax.ShapeDtypeStruct((db, 1, vw), F32), jax.ShapeDtypeStruct((db, 1, w), F32),
                   jax.ShapeDtypeStruct((db, 1, w), F32), jax.ShapeDtypeStruct(state.shape, F32)]
                  + [jax.ShapeDtypeStruct((db, KV_ROWS, HD), F32)] * 3,
        compiler_params=_cparams(("arbitrary",)),
        name="sample_mixers",
    )(z, za, wa_pad, b_alpha.reshape(1, kw), g_gla_out.reshape(1, vw), cos2, sin2, state,
      *gathered, cache_mem.reshape(db, N_MEM, KV_ROWS, HD))
    ua, ub, uc, s_out = res[:4]
    return ua.reshape(db, vw), ub.reshape(db, w), uc.reshape(db, w), s_out, res[4:]


def _prep_w_in_kernel(wt_ref, xs_ref, g_ref, wm_ref, wg_ref, zs_ref, zas_ref, buf, ga_buf, hs_ref, sem, ga_sem,
                      *, c_ga, c_gt, rows):
    i = pl.program_id(0)
    n_gt = (wt_ref.shape[0] - c_gt) // rows
    n_lo = c_ga // rows

    def fetch(ii, slot):
        src = jnp.where(ii < n_gt, c_gt + ii * rows,
                        jnp.where(ii < n_gt + n_lo, (ii - n_gt) * rows, c_ga + GLA_RANK + (ii - n_gt - n_lo) * rows))
        return pltpu.make_async_copy(wt_ref.at[pl.ds(pl.multiple_of(src, GLA_RANK), rows)], buf.at[slot], sem.at[slot])

    ga_copy = pltpu.make_async_copy(wt_ref.at[pl.ds(c_ga, GLA_RANK)], ga_buf, ga_sem)

    @pl.when(i == 0)
    def _():
        fetch(0, 0).start()
        ga_copy.start()

    @pl.when(i + 1 < pl.num_programs(0))
    def _():
        fetch(i + 1, lax.rem(i + 1, 2)).start()

    @pl.when(i == 0)
    def _():
        x = xs_ref[...]
        y = x * lax.rsqrt(jnp.mean(x * x, axis=-1, keepdims=True) + EPS)
        hs_ref[...] = (y * g_ref[...]).astype(BF16)
        ga_copy.wait()
        wg_ref[0:GLA_RANK, :] = ga_buf[...].astype(BF16)
        wg_ref[GLA_RANK:, :] = jnp.zeros((LANES - GLA_RANK, wg_ref.shape[1]), BF16)
        zas_ref[...] = _dot_nt(hs_ref[...], wg_ref[...])

    slot = lax.rem(i, 2)
    fetch(i, slot).wait()
    wm_ref[...] = buf[slot].astype(BF16)
    zs_ref[...] = _dot_nt(hs_ref[...], wm_ref[...])


def _prep_w_in(wt, xs, g, *, rows):
    n_all, d = wt.shape
    db = xs.shape[0]
    c_ga = 2 * N_HEADS * HD + 2 * N_HEADS * GLA_DV
    c_gt = n_all - 3 * d
    assert n_all - GLA_RANK == N_MAIN and xs.shape[1] == d
    assert (n_all - c_gt) % rows == 0 and c_ga % rows == 0 and (c_gt - c_ga - GLA_RANK) % rows == 0
    const = lambda i: (0, 0)
    return pl.pallas_call(
        functools.partial(_prep_w_in_kernel, c_ga=c_ga, c_gt=c_gt, rows=rows),
        grid=(N_MAIN // rows,),
        in_specs=[pl.BlockSpec(memory_space=pl.ANY), pl.BlockSpec((db, d), const), pl.BlockSpec((1, d), const)],
        out_specs=[pl.BlockSpec((rows, d), lambda i: (i, 0)), pl.BlockSpec((LANES, d), const),
                   pl.BlockSpec((db, rows), lambda i: (0, i)), pl.BlockSpec((db, LANES), const)],
        out_shape=[jax.ShapeDtypeStruct((N_MAIN, d), BF16), jax.ShapeDtypeStruct((LANES, d), BF16),
                   jax.ShapeDtypeStruct((db, N_MAIN), F32), jax.ShapeDtypeStruct((db, LANES), F32)],
        scratch_shapes=[pltpu.VMEM((2, rows, d), F32), pltpu.VMEM((GLA_RANK, d), F32), pltpu.VMEM((db, d), BF16),
                        pltpu.SemaphoreType.DMA((2,)), pltpu.SemaphoreType.DMA(())],
        compiler_params=_cparams(("arbitrary",)),
        name="prep_w_in",
    )(wt, xs, g.reshape(1, d))


def kernel(x_prompt, x_sample, mem_prompt, state_gla, cache_swa_w128, cache_swa_w512, cache_swa_w2048, cache_mem_kv, g_norm, w_in, w_alpha2, b_alpha, g_gla_out, g_mem, w_mem_kv, w_proj_a, w_proj_b, w_proj_c, w_out, g_final):
    batch, seq, d = x_prompt.shape
    db, dec_seq, _ = x_sample.shape
    assert g_norm.shape[0] == 1 and dec_seq == 1

    xp = x_prompt.reshape(batch * seq, d)
    xs = x_sample.reshape(db, d)

    w_main, w_ga, zs, zas = _prep_w_in(jnp.swapaxes(w_in[0], 0, 1), xs, g_norm[0], rows=TILES["prep_rows"])
    wa_pad = jnp.pad(w_alpha2[0], ((0, LANES - GLA_RANK), (0, 0))).astype(BF16)
    wpa, wpb, wpc, wo = (w[0].astype(BF16) for w in (w_proj_a, w_proj_b, w_proj_c, w_out))

    cos_s, sin_s = _rope_tables(jnp.full((1,), PAST_LEN, jnp.int32))
    caches = (cache_swa_w128, cache_swa_w512, cache_swa_w2048)
    uas, ubs, ucs, gla_s, new_rows = _sample_mixers(
        zs, zas, wa_pad, b_alpha[0], g_gla_out[0], cos_s, sin_s, state_gla, caches, cache_mem_kv)
    y_sample = _final(uas, ubs, ucs, zs, xs, wpa, wpb, wpc, wo, g_final, tm=db).reshape(db, 1, d)

    z, za = _norm_matmul(xp, g_norm[0], w_main, w_ga, w_rows_out=True, out_dtype=BF16,
                         tm=TILES["proj_rows"], tn=TILES["proj_cols"])
    flat_caches = [c.reshape(db, w * KV_ROWS, HD) for c, (w, _) in zip(caches, SWA_PATTERNS)]
    ua, gla_p, *shifted = _gla_prompt(z, za, wa_pad, b_alpha[0], g_gla_out[0], batch, seq,
                                      (flat_caches, new_rows), t_blk=TILES["gla_rows"])
    swa_s = [o.reshape(c.shape) for o, c in zip(shifted, caches)]
    mem_kv = _norm_matmul(mem_prompt.reshape(batch * N_MEM, d), g_mem[0], w_mem_kv[0].astype(BF16),
                          tm=batch * N_MEM, tn=TILES["mem_cols"])
    cos_p, sin_p = _rope_tables_range(seq)
    swa_res = _swa_prompt(z, cos_p, sin_p, batch, seq)
    ub, kv_bufs = swa_res[0], swa_res[1:4]
    y_prompt = _final(ua, ub, None, z, xp, wpa, wpb, wpc, wo, g_final, tm=TILES["final_rows"], mem_kv=mem_kv,
                      rows_per_batch=seq).reshape(batch, seq, d)

    swa_p = [kv.reshape(1, batch, w, 2, N_HEADS, HD) for kv, (w, _) in zip(kv_bufs, SWA_PATTERNS)]
    mem_kv_prompt = mem_kv.reshape(1, batch, N_MEM, 2, N_HEADS, HD)
    return (y_prompt, y_sample, gla_p[None], swa_p[0], swa_p[1], swa_p[2], mem_kv_prompt,
            gla_s, swa_s[0], swa_s[1], swa_s[2])
```

```python
import functools

import numpy as np
import jax
import jax.numpy as jnp
from jax import lax
from jax.experimental import pallas as pl
from jax.experimental.pallas import tpu as pltpu

F32 = jnp.float32
BF16 = jnp.bfloat16

EPS = 1e-6
ROPE_THETA = 10000.0
NEG = -1e30
PAST_LEN = 16384

HD = 128
N_HEADS = 4
GLA_DV = 256
GLA_RANK = 16
GLA_TAU = 16.0
LOG2_E = 1.4426950408889634
GLA_CHUNK = 64
GLA_SUB = 16
GLA_UNROLL = 4
SWA_PATTERNS = ((128, 1), (512, 4), (2048, 16))
SWA_BLK = 128
SWA_STEP = 2048
SWA_UNROLL = 2
N_MEM = 256

LANES = 128
VMEM_LIMIT = 60000 * 1024
VMEM_LIMIT_FINAL = 62 * 1024 * 1024

TILES = dict(prep_rows=512, proj_rows=1024, proj_cols=2560, gla_rows=512, mem_cols=512, final_rows=512)

C_GT = 0
C_GQ, C_GK, C_GV, C_GR = 6144, 6656, 7168, 8192
C_SQ, C_SK, C_SV, C_SR = 9216, 10752, 12288, 13824
C_MQ, C_MR = 14336, 14848
N_MAIN = 15360


def _cparams(sem, vmem_limit=VMEM_LIMIT):
    return pltpu.CompilerParams(dimension_semantics=sem, vmem_limit_bytes=vmem_limit)


def _dot(a, b):
    return jnp.dot(a, b, preferred_element_type=F32)


def _dot_nt(a, b):
    return lax.dot_general(a, b, (((1,), (1,)), ((), ())), preferred_element_type=F32)


def _dot_tn(a, b):
    return lax.dot_general(a, b, (((0,), (0,)), ((), ())), preferred_element_type=F32)


def _silu(x):
    return x * jax.nn.sigmoid(x)


def _log_sigmoid(x):
    return jnp.minimum(x, 0.0) - jnp.log1p(jnp.exp(-jnp.abs(x)))


KV_ROWS = 2 * N_HEADS
SHIFT_ROWS = tuple((w - 1) * KV_ROWS for w, _ in SWA_PATTERNS)
SHIFT_OFFS = tuple(sum(SHIFT_ROWS[:g]) for g in range(len(SWA_PATTERNS)))


def _cache_shift_step(step, n_steps, ca, nr, co, stage, in_sem, out_sem, row_sem):
    n_grp = len(ca)
    n_batch = ca[0].shape[0]
    assert n_steps >= n_batch
    slot = lax.rem(step, 2)

    def in_copy(g, bb, sl):
        return pltpu.make_async_copy(ca[g].at[bb, pl.ds(KV_ROWS, SHIFT_ROWS[g])],
                                     stage.at[sl, pl.ds(SHIFT_OFFS[g], SHIFT_ROWS[g])], in_sem.at[sl, g])

    def out_copy(g, bb, sl):
        return pltpu.make_async_copy(stage.at[sl, pl.ds(SHIFT_OFFS[g], SHIFT_ROWS[g])],
                                     co[g].at[bb, pl.ds(0, SHIFT_ROWS[g])], out_sem.at[sl, g])

    def row_copy(g, bb, sl):
        return pltpu.make_async_copy(nr[g].at[bb], co[g].at[bb, pl.ds(SHIFT_ROWS[g], KV_ROWS)], row_sem.at[sl, g])

    def finish_writes(bb, sl):
        for g in range(n_grp):
            out_copy(g, bb, sl).wait()
            row_copy(g, bb, sl).wait()

    @pl.when(step == 0)
    def _():
        for g in range(n_grp):
            in_copy(g, 0, 0).start()

    @pl.when((step >= 1) & (step <= n_batch))
    def _():
        finish_writes(step - 1, 1 - slot)

    @pl.when(step + 1 < n_batch)
    def _():
        for g in range(n_grp):
            in_copy(g, step + 1, 1 - slot).start()

    @pl.when(step < n_batch)
    def _():
        for g in range(n_grp):
            in_copy(g, step, slot).wait()
            out_copy(g, step, slot).start()
            row_copy(g, step, slot).start()

    if n_steps == n_batch:
        @pl.when(step == n_steps - 1)
        def _():
            finish_writes(step, slot)


def _norm_matmul_kernel(*refs, has_extra, row_chunk, w_rows_out):
    mm = _dot_nt if w_rows_out else _dot
    if has_extra:
        x_ref, g_ref, w_ref, wx_ref, o_ref, ox_ref, h_ref = refs
    else:
        x_ref, g_ref, w_ref, o_ref, h_ref = refs
    tm = x_ref.shape[0]

    @pl.when(pl.program_id(1) == 0)
    def _():
        for c in range(tm // row_chunk):
            rs = slice(c * row_chunk, (c + 1) * row_chunk)
            x = x_ref[rs, :]
            y = x * lax.rsqrt(jnp.mean(x * x, axis=-1, keepdims=True) + EPS)
            h_ref[rs, :] = (y * g_ref[...]).astype(BF16)
            if has_extra:
                ox_ref[rs, :] = mm(h_ref[rs, :], wx_ref[...])

    o_ref[...] = mm(h_ref[...], w_ref[...]).astype(o_ref.dtype)


def _norm_matmul(x, g, w, wx=None, *, w_rows_out=False, out_dtype=F32, tm, tn):
    m, d = x.shape
    n = w.shape[0] if w_rows_out else w.shape[1]
    assert m % tm == 0 and n % tn == 0
    has_extra = wx is not None
    row_chunk = min(tm, 128)
    w_spec = pl.BlockSpec((tn, d), lambda i, j: (j, 0)) if w_rows_out else pl.BlockSpec((d, tn), lambda i, j: (0, j))
    in_specs = [pl.BlockSpec((tm, d), lambda i, j: (i, 0)),
                pl.BlockSpec((1, d), lambda i, j: (0, 0)),
                w_spec]
    out_specs = [pl.BlockSpec((tm, tn), lambda i, j: (i, j))]
    out_shape = [jax.ShapeDtypeStruct((m, n), out_dtype)]
    args = [x, g.reshape(1, d), w]
    if has_extra:
        nx = wx.shape[0] if w_rows_out else wx.shape[1]
        in_specs.append(pl.BlockSpec(wx.shape, lambda i, j: (0, 0)))
        out_specs.append(pl.BlockSpec((tm, nx), lambda i, j: (i, 0)))
        out_shape.append(jax.ShapeDtypeStruct((m, nx), F32))
        args.append(wx)
    res = pl.pallas_call(
        functools.partial(_norm_matmul_kernel, has_extra=has_extra, row_chunk=row_chunk, w_rows_out=w_rows_out),
        grid=(m // tm, n // tn),
        in_specs=in_specs, out_specs=out_specs, out_shape=out_shape,
        scratch_shapes=[pltpu.VMEM((tm, d), BF16)],
        compiler_params=_cparams(("arbitrary", "arbitrary")),
        name="norm_matmul",
    )(*args)
    return res if has_extra else res[0]


def _diag_select_matrix():
    rows = np.arange(GLA_SUB * HD)[:, None] // HD
    cols = np.arange(LANES)[None, :] % GLA_SUB
    return jnp.asarray((rows == cols).astype(np.float32), dtype=BF16)


def _gla_chunks(rows_list, zg_ref, ga_ref, wa_ref, ba_ref, gg_ref, em_ref, st_ref, u_ref):
    C, SUB, n_sub = GLA_CHUNK, GLA_SUB, GLA_CHUNK // GLA_SUB
    kw = N_HEADS * HD
    c_k, c_v, c_r = kw, 2 * kw, 2 * kw + N_HEADS * GLA_DV
    row = lax.broadcasted_iota(jnp.int32, (C, C), 0)
    col = lax.broadcasted_iota(jnp.int32, (C, C), 1)
    tri = (col <= row).astype(F32)
    row_c = lax.broadcasted_iota(jnp.int32, (C, HD), 0)
    half = SUB // 2
    half_row = lax.broadcasted_iota(jnp.int32, (half, HD), 0)
    lane_c = lax.broadcasted_iota(jnp.int32, (SUB, C), 1)

    xas = [_dot(ga_ref[rows, :].astype(BF16), wa_ref[...]) + ba_ref[...] for rows in rows_list]
    b_alls = [jnp.dot(tri, _log_sigmoid(xa) * (LOG2_E / GLA_TAU), preferred_element_type=F32,
                      precision=lax.Precision.HIGHEST) for xa in xas]

    chunks = []
    for rows, b_all in zip(rows_list, b_alls):
        a_off = []
        per_head = []
        for h in range(N_HEADS):
            q = zg_ref[rows, h * HD:(h + 1) * HD].astype(F32) * (HD ** -0.5)
            k = zg_ref[rows, c_k + h * HD:c_k + (h + 1) * HD].astype(F32)
            v = zg_ref[rows, c_v + h * GLA_DV:c_v + (h + 1) * GLA_DV]
            b = b_all[:, h * HD:(h + 1) * HD]
            for i in range(n_sub):
                if i == 0:
                    a_off.append(jnp.zeros((SUB, C), F32))
                else:
                    sl = slice(i * SUB, (i + 1) * SUB)
                    b_ref_row = b[i * SUB - 1:i * SUB, :]
                    qs = q[sl] * jnp.exp2(b[sl] - b_ref_row)
                    ks = k * jnp.exp2(jnp.where(row_c < i * SUB, b_ref_row - b, -jnp.inf))
                    a_off.append(_dot_nt(qs.astype(BF16), ks.astype(BF16)))
            st = st_ref[h]
            o_inter = _dot_nt((q * jnp.exp2(b)).astype(BF16), st.astype(BF16))
            b_end = b[C - 1:C, :]
            kd = k * jnp.exp2(b_end - b)
            st_ref[h] = st * jnp.exp2(b_end) + _dot_tn(v.astype(BF16), kd.astype(BF16))
            per_head.append((q, k, v, b, o_inter))
        chunks.append((rows, a_off, per_head))

    p_rows = []
    for _, _, per_head in chunks:
        for h in range(N_HEADS):
            q, k, _, b, _ = per_head[h]
            for i in range(n_sub):
                top, bot = slice(i * SUB, i * SUB + half), slice(i * SUB + half, (i + 1) * SUB)
                b_mid = b[i * SUB + half - 1:i * SUB + half, :]
                qs_bot = q[bot] * jnp.exp2(b[bot] - b_mid)
                ks_top = k[top] * jnp.exp2(b_mid - b[top])
                slabs = []
                for s in range(SUB):
                    rs = top if s < half else bot
                    s0 = s % half
                    e = jnp.where(half_row >= s0, b[rs] - b[rs][s0:s0 + 1, :], -jnp.inf)
                    diag = (q[rs] * k[rs][s0:s0 + 1, :]) * jnp.exp2(e)
                    if s < half:
                        slabs.append(jnp.concatenate([diag, qs_bot * ks_top[s0:s0 + 1, :]], axis=0))
                    else:
                        slabs.append(jnp.concatenate([jnp.zeros((half, HD), F32), diag], axis=0))
                p_rows.append(jnp.concatenate(slabs, axis=1).astype(BF16))
    r_all = _dot(jnp.concatenate(p_rows, axis=0), em_ref[...])

    for ci, (rows, a_off, per_head) in enumerate(chunks):
        for h in range(N_HEADS):
            _, _, v, _, o_inter = per_head[h]
            a_rows = []
            for i in range(n_sub):
                idx = h * n_sub + i
                r0 = (ci * N_HEADS * n_sub + idx) * SUB
                in_blk = (lane_c >= i * SUB) & (lane_c < (i + 1) * SUB)
                a_rows.append(a_off[idx] + jnp.where(in_blk, r_all[r0:r0 + SUB, :C], 0.0))
            a = jnp.concatenate(a_rows, axis=0)
            o = _dot(a.astype(BF16), v.astype(BF16)) + o_inter
            y = o * lax.rsqrt(jnp.mean(o * o, axis=-1, keepdims=True) + EPS)
            y = y * gg_ref[:, h * GLA_DV:(h + 1) * GLA_DV]
            gate = zg_ref[rows, c_r + h * GLA_DV:c_r + (h + 1) * GLA_DV].astype(F32)
            u_ref[rows, h * GLA_DV:(h + 1) * GLA_DV] = (y * _silu(gate)).astype(u_ref.dtype)


def _gla_kernel(*refs, n_shift, n_steps):
    zg_ref, ga_ref, wa_ref, ba_ref, gg_ref, em_ref = refs[:6]
    n_in = 6 + 2 * n_shift
    u_ref, sout_ref = refs[n_in:n_in + 2]
    st_ref = refs[n_in + 2 + n_shift]
    t_blk = pl.program_id(1)
    n_chunks = zg_ref.shape[0] // GLA_CHUNK
    assert n_chunks % GLA_UNROLL == 0

    if n_shift:
        stage, in_sem, out_sem, row_sem = refs[n_in + 3 + n_shift:]
        _cache_shift_step(pl.program_id(0) * pl.num_programs(1) + t_blk, n_steps, refs[6:6 + n_shift],
                          refs[6 + n_shift:n_in], refs[n_in + 2:n_in + 2 + n_shift], stage, in_sem, out_sem, row_sem)

    @pl.when(t_blk == 0)
    def _():
        st_ref[...] = jnp.zeros_like(st_ref)

    def chunks(c, carry):
        rows_list = [pl.ds(pl.multiple_of((c * GLA_UNROLL + u) * GLA_CHUNK, GLA_CHUNK), GLA_CHUNK)
                     for u in range(GLA_UNROLL)]
        _gla_chunks(rows_list, zg_ref, ga_ref, wa_ref, ba_ref, gg_ref, em_ref, st_ref, u_ref)
        return carry

    lax.fori_loop(0, n_chunks // GLA_UNROLL, chunks, 0)

    @pl.when(t_blk == pl.num_programs(1) - 1)
    def _():
        for h in range(N_HEADS):
            sout_ref[h] = st_ref[h].T


def _gla_prompt(z, za, wa_pad, b_alpha, g_gla_out, batch, seq, shift=None, *, t_blk):
    nt = seq // t_blk
    assert seq % t_blk == 0 and t_blk % GLA_CHUNK == 0
    kw = N_HEADS * HD
    vw = N_HEADS * GLA_DV
    gla_w = 2 * kw + 2 * vw
    assert C_GQ % gla_w == 0 and (C_GK, C_GV, C_GR) == (C_GQ + kw, C_GQ + 2 * kw, C_GQ + 2 * kw + vw)
    const = lambda b, t: (0, 0)
    in_specs = [pl.BlockSpec((t_blk, gla_w), lambda b, t: (b * nt + t, C_GQ // gla_w)),
                pl.BlockSpec((t_blk, LANES), lambda b, t: (b * nt + t, 0)),
                pl.BlockSpec((LANES, kw), const), pl.BlockSpec((1, kw), const), pl.BlockSpec((1, vw), const),
                pl.BlockSpec((GLA_SUB * HD, LANES), const)]
    out_specs = [pl.BlockSpec((t_blk, vw), lambda b, t: (b * nt + t, 0)),
                 pl.BlockSpec((None, N_HEADS, HD, GLA_DV), lambda b, t: (b, 0, 0, 0))]
    out_shape = [jax.ShapeDtypeStruct((batch * seq, vw), BF16),
                 jax.ShapeDtypeStruct((batch, N_HEADS, HD, GLA_DV), F32)]
    scratch = [pltpu.VMEM((N_HEADS, GLA_DV, HD), F32)]
    args = [z, za, wa_pad, b_alpha.reshape(1, kw), g_gla_out.reshape(1, vw), _diag_select_matrix()]
    n_shift = 0
    if shift is not None:
        caches, new_rows = shift
        n_shift = len(caches)
        any_spec = pl.BlockSpec(memory_space=pl.ANY)
        in_specs += [any_spec] * n_shift + [pl.BlockSpec(r.shape, lambda b, t: (0, 0, 0)) for r in new_rows]
        out_specs += [any_spec] * n_shift
        out_shape += [jax.ShapeDtypeStruct(c.shape, c.dtype) for c in caches]
        args += list(caches) + list(new_rows)
        scratch += [pltpu.VMEM((2, sum(SHIFT_ROWS), HD), F32)] + [pltpu.SemaphoreType.DMA((2, n_shift))] * 3
    return pl.pallas_call(
        functools.partial(_gla_kernel, n_shift=n_shift, n_steps=batch * nt),
        grid=(batch, nt),
        in_specs=in_specs, out_specs=out_specs, out_shape=out_shape, scratch_shapes=scratch,
        compiler_params=_cparams(("arbitrary", "arbitrary")),
        name="gla",
    )(*args)


def _rope_tables(pos):
    half = HD // 2
    inv = ROPE_THETA ** (-jnp.arange(half, dtype=F32) / half)
    ang = pos.astype(F32)[:, None] * inv[None, :]
    cos, sin = jnp.cos(ang), jnp.sin(ang)
    return jnp.concatenate([cos, cos], axis=1), jnp.concatenate([-sin, sin], axis=1)


def _rope_tables_range(n, blk=128):
    assert n % blk == 0
    half = HD // 2
    inv = ROPE_THETA ** (-jnp.arange(half, dtype=F32) / half)
    inv2 = jnp.concatenate([inv, inv])
    sign = jnp.concatenate([-jnp.ones((half,), F32), jnp.ones((half,), F32)])
    ang_a = (jnp.arange(n // blk, dtype=jnp.int32) * blk).astype(F32)[:, None] * inv2[None, :]
    ang_b = jnp.arange(blk, dtype=jnp.int32).astype(F32)[:, None] * inv2[None, :]
    ca, sa, cb, sb = jnp.cos(ang_a)[:, None], jnp.sin(ang_a)[:, None], jnp.cos(ang_b)[None], jnp.sin(ang_b)[None]
    return (ca * cb - sa * sb).reshape(n, HD), ((sa * cb + ca * sb) * sign).reshape(n, HD)


def _rope(x, cos2, sin2):
    return x * cos2 + pltpu.roll(x, HD // 2, axis=1) * sin2


def _swa_kernel(*refs):
    (q0, q1, q2, k0, k1, k2, v0, v1, v2, sr_ref, cos_ref, sin_ref,
     u_ref, kv0, kv1, kv2,
     q_s, k_s, v_s, o_s, l_s) = refs
    q_in, k_in, v_in = (q0, q1, q2), (k0, k1, k2), (v0, v1, v2)
    kv_out = (kv0, kv1, kv2)
    i = pl.program_id(2)
    T = SWA_STEP
    n_grp = len(SWA_PATTERNS)

    @pl.when(i == 0)
    def _():
        for g, (_, dil) in enumerate(SWA_PATTERNS):
            unit = dil * SWA_BLK
            k_s[g, T - unit:T, :] = jnp.zeros((unit, HD), F32)
            v_s[g, T - unit:T, :] = jnp.zeros((unit, HD), F32)

    @pl.when(i > 0)
    def _():
        for g, (_, dil) in enumerate(SWA_PATTERNS):
            unit = dil * SWA_BLK
            k_s[g, T - unit:T, :] = k_s[g, 2 * T - unit:2 * T, :]
            v_s[g, T - unit:T, :] = v_s[g, 2 * T - unit:2 * T, :]

    cos2, sin2 = cos_ref[...], sin_ref[...]
    swap = (lax.broadcasted_iota(jnp.int32, (HD, HD), 0)
            == lax.rem(lax.broadcasted_iota(jnp.int32, (HD, HD), 1) + HD // 2, HD)).astype(BF16)

    def rope_bf16(x_bf16):
        return x_bf16.astype(F32) * cos2 + _dot(x_bf16, swap) * sin2

    for g in range(n_grp):
        q_s[g] = rope_bf16(q_in[g][...]) * (HD ** -0.5)
        k_s[g, T:2 * T, :] = rope_bf16(k_in[g][...])
        v_s[g, T:2 * T, :] = v_in[g][...].astype(F32)

    qi = lax.broadcasted_iota(jnp.int32, (SWA_BLK, 2 * SWA_BLK), 0)
    ki = lax.broadcasted_iota(jnp.int32, (SWA_BLK, 2 * SWA_BLK), 1)
    delta = SWA_BLK + qi - ki

    def blocks(it, carry):
        todo = []
        for g, (win, dil) in enumerate(SWA_PATTERNS):
            for j in range(SWA_UNROLL):
                n = it * SWA_UNROLL + j
                unit = dil * SWA_BLK
                u = lax.shift_right_logical(n, dil.bit_length() - 1)
                r = lax.bitwise_and(n, dil - 1)

                def rows(start, size, dil=dil):
                    return pl.ds(start, size) if dil == 1 else pl.ds(start, size, stride=dil)

                q = q_s[g, rows(u * unit + r, SWA_BLK), :]
                kk = k_s[g, rows(T + (u - 1) * unit + r, 2 * SWA_BLK), :]
                s = _dot_nt(q.astype(BF16), kk.astype(BF16))
                ki_min = jnp.where((i * (T // unit) + u) == 0, SWA_BLK, 0)
                valid = (delta >= 0) & (delta <= win // dil) & (ki >= ki_min)
                todo.append((g, rows(u * unit + r, SWA_BLK), rows(T + (u - 1) * unit + r, 2 * SWA_BLK), s, valid))
        soft = []
        for g, q_rows, kv_rows, s, valid in todo:
            s = jnp.where(valid, s, NEG)
            m = jnp.max(s, axis=-1, keepdims=True)
            p = jnp.exp(s - m)
            den = jnp.sum(p, axis=-1, keepdims=True)
            soft.append((g, q_rows, kv_rows, p.astype(BF16), den, m + jnp.log(den)))
        for g, q_rows, kv_rows, p, den, lse in soft:
            o_s[g, q_rows, :] = _dot(p, v_s[g, kv_rows, :].astype(BF16)) / den
            l_s[g, q_rows, :] = jnp.broadcast_to(lse, (SWA_BLK, HD))
        return carry

    lax.fori_loop(0, T // SWA_BLK // SWA_UNROLL, blocks, 0)

    rc = 256

    def merge(c, carry):
        r0 = pl.multiple_of(c * rc, rc)
        rr = pl.ds(r0, rc)
        ls = [l_s[g, rr, :] for g in range(n_grp)]
        mx = functools.reduce(jnp.maximum, ls)
        es = [jnp.exp(l - mx) for l in ls]
        tot = functools.reduce(lambda a, b: a + b, es)
        ob = functools.reduce(lambda a, b: a + b, [(es[g] / tot) * o_s[g, rr, :] for g in range(n_grp)])
        u_ref[rr, :] = (ob * _silu(sr_ref[rr, :].astype(F32))).astype(u_ref.dtype)
        return carry

    lax.fori_loop(0, T // rc, merge, 0)

    @pl.when(i == pl.num_programs(2) - 1)
    def _():
        for g, (win, _) in enumerate(SWA_PATTERNS):
            head = pl.program_id(1)
            kv_out[g][pl.ds(head, win, stride=KV_ROWS), :] = k_s[g, 2 * T - win:2 * T, :]
            kv_out[g][pl.ds(N_HEADS + head, win, stride=KV_ROWS), :] = v_s[g, 2 * T - win:2 * T, :]


def _swa_prompt(z, cos2, sin2, batch, seq):
    T = SWA_STEP
    assert seq % T == 0 and all(w <= T for w, _ in SWA_PATTERNS)
    nt = seq // T
    w_grp = N_HEADS * HD

    def zspec(col0, g):
        cb0 = (col0 + g * w_grp) // HD
        return pl.BlockSpec((T, HD), lambda b, j, i: (b * nt + i, cb0 + j))

    in_specs = ([zspec(C_SQ, g) for g in range(3)] + [zspec(C_SK, g) for g in range(3)]
                + [zspec(C_SV, g) for g in range(3)] + [zspec(C_SR, 0)]
                + [pl.BlockSpec((T, HD), lambda b, j, i: (i, 0))] * 2)
    buf_specs = [pl.BlockSpec((None, w * KV_ROWS, HD), lambda b, j, i: (b, 0, 0), pipeline_mode=pl.Buffered(1))
                 for w, _ in SWA_PATTERNS]
    buf_shapes = [jax.ShapeDtypeStruct((batch, w * KV_ROWS, HD), F32) for w, _ in SWA_PATTERNS]
    return pl.pallas_call(
        _swa_kernel,
        grid=(batch, N_HEADS, nt),
        in_specs=in_specs,
        out_specs=[pl.BlockSpec((T, HD), lambda b, j, i: (b * nt + i, j))] + buf_specs,
        out_shape=[jax.ShapeDtypeStruct((batch * seq, w_grp), BF16)] + buf_shapes,
        scratch_shapes=[pltpu.VMEM((3, T, HD), F32), pltpu.VMEM((3, 2 * T, HD), F32),
                        pltpu.VMEM((3, 2 * T, HD), F32), pltpu.VMEM((3, T, HD), F32),
                        pltpu.VMEM((3, T, HD), F32)],
        compiler_params=_cparams(("arbitrary", "arbitrary", "arbitrary")),
        name="swa",
    )(*([z] * 10), cos2, sin2)


def _final_kernel(*refs, fuse_mem):
    if fuse_mem:
        ua_ref, ub_ref, mq_ref, mr_ref, kv_ref, gt_ref, x_ref, wa_ref, wb_ref, wc_ref, wo_ref, gf_ref, y_ref = refs
    else:
        ua_ref, ub_ref, uc_ref, gt_ref, x_ref, wa_ref, wb_ref, wc_ref, wo_ref, gf_ref, y_ref = refs
    d = x_ref.shape[1]
    w = N_HEADS * HD
    if fuse_mem:
        scores = [_dot_nt((mq_ref[:, h * HD:(h + 1) * HD].astype(F32) * (HD ** -0.5)).astype(BF16),
                          kv_ref[:, h * HD:(h + 1) * HD].astype(BF16)) for h in range(N_HEADS)]
    ya = _dot(ua_ref[...].astype(BF16), wa_ref[...])
    yb = _dot(ub_ref[...].astype(BF16), wb_ref[...])
    if fuse_mem:
        ucs = []
        for h, s in enumerate(scores):
            e = jnp.exp(s - jnp.max(s, axis=-1, keepdims=True))
            p = e / jnp.sum(e, axis=-1, keepdims=True)
            o = _dot(p.astype(BF16), kv_ref[:, w + h * HD:w + (h + 1) * HD].astype(BF16))
            ucs.append((o * _silu(mr_ref[:, h * HD:(h + 1) * HD].astype(F32))).astype(BF16))
        uc = jnp.concatenate(ucs, axis=1)
    else:
        uc = uc_ref[...].astype(BF16)
    yc = _dot(uc, wc_ref[...])
    mix = (jax.nn.sigmoid(gt_ref[:, 0:d].astype(F32)) * ya + jax.nn.sigmoid(gt_ref[:, d:2 * d].astype(F32)) * yb
           + jax.nn.sigmoid(gt_ref[:, 2 * d:3 * d].astype(F32)) * yc)
    xo = x_ref[...] + _dot(mix.astype(BF16), wo_ref[...])
    y = xo * lax.rsqrt(jnp.mean(xo * xo, axis=-1, keepdims=True) + EPS)
    y_ref[...] = y * gf_ref[...]


def _final(ua, ub, uc, z, x, wa, wb, wc, wo, g_final, *, tm, mem_kv=None, rows_per_batch=None):
    m, d = x.shape
    w = N_HEADS * HD
    const = lambda i: (0, 0)
    resident = dict(pipeline_mode=pl.Buffered(1))
    fuse_mem = uc is None
    if fuse_mem:
        assert rows_per_batch % tm == 0
        mem_specs = [pl.BlockSpec((tm, w), lambda i: (i, C_MQ // w)), pl.BlockSpec((tm, w), lambda i: (i, C_MR // w)),
                     pl.BlockSpec((N_MEM, 2 * w), lambda i: (i // (rows_per_batch // tm), 0))]
        mem_args = [z, z, mem_kv]
    else:
        mem_specs = [pl.BlockSpec((tm, uc.shape[1]), lambda i: (i, 0))]
        mem_args = [uc]
    return pl.pallas_call(
        functools.partial(_final_kernel, fuse_mem=fuse_mem),
        grid=(m // tm,),
        in_specs=[pl.BlockSpec((tm, ua.shape[1]), lambda i: (i, 0)),
                  pl.BlockSpec((tm, ub.shape[1]), lambda i: (i, 0))] + mem_specs + [
                  pl.BlockSpec((tm, 3 * d), lambda i: (i, C_GT // (3 * d))),
                  pl.BlockSpec((tm, d), lambda i: (i, 0)),
                  pl.BlockSpec(wa.shape, const, **resident),
                  pl.BlockSpec(wb.shape, const, **resident),
                  pl.BlockSpec(wc.shape, const, **resident),
                  pl.BlockSpec(wo.shape, const, **resident),
                  pl.BlockSpec((1, d), const)],
        out_specs=pl.BlockSpec((tm, d), lambda i: (i, 0)),
        out_shape=jax.ShapeDtypeStruct((m, d), F32),
        compiler_params=_cparams(("arbitrary",), VMEM_LIMIT_FINAL),
        name="final",
    )(ua, ub, *mem_args, z, x, wa, wb, wc, wo, g_final.reshape(1, d))


def _heads_rows(row, col0):
    return jnp.concatenate([row[:, col0 + h * HD:col0 + (h + 1) * HD] for h in range(N_HEADS)], axis=0)


def _decode_attention(q4, kk, vv, k_new=None, v_new=None):
    s = jnp.sum(kk * q4[None], axis=-1, keepdims=True)
    m = jnp.max(s, axis=0)
    if k_new is not None:
        s_new = jnp.sum(k_new * q4, axis=-1, keepdims=True)
        m = jnp.maximum(m, s_new)
    p = jnp.exp(s - m[None])
    den = jnp.sum(p, axis=0)
    acc = jnp.sum(p * vv, axis=0)
    if k_new is not None:
        p_new = jnp.exp(s_new - m)
        den = den + p_new
        acc = acc + p_new * v_new
    return acc / den, m + jnp.log(den)


def _sample_kernel(*refs):
    (z_ref, za_ref, wa_ref, ba_ref, gg_ref, cos_ref, sin_ref, st_ref, cg0, cg1, cg2, cm_ref,
     ua_ref, ub_ref, uc_ref, sout_ref, nr0, nr1, nr2) = refs
    cg, nr = (cg0, cg1, cg2), (nr0, nr1, nr2)
    b = pl.program_id(0)
    zrow = z_ref[pl.ds(b, 1), :]

    ga8 = jnp.broadcast_to(za_ref[pl.ds(b, 1), :], (8, LANES))
    xa = _dot(ga8.astype(BF16), wa_ref[...])[0:1, :] + ba_ref[...]
    a_row = jnp.exp(_log_sigmoid(xa) / GLA_TAU)
    eye = (lax.broadcasted_iota(jnp.int32, (HD, HD), 0) == lax.broadcasted_iota(jnp.int32, (HD, HD), 1))

    def col(row_vec):
        return jnp.sum(jnp.where(eye, jnp.broadcast_to(row_vec, (HD, HD)), 0.0), axis=1, keepdims=True)

    for h in range(N_HEADS):
        q = zrow[:, C_GQ + h * HD:C_GQ + (h + 1) * HD] * (HD ** -0.5)
        k = zrow[:, C_GK + h * HD:C_GK + (h + 1) * HD]
        v = zrow[:, C_GV + h * GLA_DV:C_GV + (h + 1) * GLA_DV]
        s_new = col(a_row[:, h * HD:(h + 1) * HD]) * st_ref[h] + col(k) * v
        sout_ref[h] = s_new
        o = jnp.sum(col(q) * s_new, axis=0, keepdims=True)
        y = o * lax.rsqrt(jnp.mean(o * o, axis=-1, keepdims=True) + EPS)
        y = y * gg_ref[:, h * GLA_DV:(h + 1) * GLA_DV]
        gate = zrow[:, C_GR + h * GLA_DV:C_GR + (h + 1) * GLA_DV]
        ua_ref[:, h * GLA_DV:(h + 1) * GLA_DV] = y * _silu(gate)

    cos2, sin2 = cos_ref[...], sin_ref[...]
    outs, lses = [], []
    w_grp = N_HEADS * HD
    for g in range(3):
        q4 = _rope(_heads_rows(zrow, C_SQ + g * w_grp), cos2, sin2) * (HD ** -0.5)
        k4 = _rope(_heads_rows(zrow, C_SK + g * w_grp), cos2, sin2)
        v4 = _heads_rows(zrow, C_SV + g * w_grp)
        nr[g][0:N_HEADS, :] = k4
        nr[g][N_HEADS:KV_ROWS, :] = v4
        o, lse = _decode_attention(q4, cg[g][:, 0:N_HEADS, :], cg[g][:, N_HEADS:KV_ROWS, :], k4, v4)
        outs.append(o)
        lses.append(lse)
    mx = functools.reduce(jnp.maximum, lses)
    es = [jnp.exp(l - mx) for l in lses]
    tot = es[0] + es[1] + es[2]
    ob = (es[0] / tot) * outs[0] + (es[1] / tot) * outs[1] + (es[2] / tot) * outs[2]
    for h in range(N_HEADS):
        gate = zrow[:, C_SR + h * HD:C_SR + (h + 1) * HD]
        ub_ref[:, h * HD:(h + 1) * HD] = ob[h:h + 1, :] * _silu(gate)

    qm = _heads_rows(zrow, C_MQ) * (HD ** -0.5)
    oc, _ = _decode_attention(qm, cm_ref[:, 0:N_HEADS, :], cm_ref[:, N_HEADS:KV_ROWS, :])
    for h in range(N_HEADS):
        gate = zrow[:, C_MR + h * HD:C_MR + (h + 1) * HD]
        uc_ref[:, h * HD:(h + 1) * HD] = oc[h:h + 1, :] * _silu(gate)


def _sample_mixers(z, za, wa_pad, b_alpha, g_gla_out, cos2, sin2, state, caches, cache_mem):
    db = z.shape[0]
    kw, vw, w = N_HEADS * HD, N_HEADS * GLA_DV, N_HEADS * HD
    const2 = lambda b: (0, 0)
    gathered, gather_specs = [], []
    for c, (win, dil) in zip(caches, SWA_PATTERNS):
        assert c.shape == (1, db, win, 2, N_HEADS, HD) and PAST_LEN >= win
        n_keys = win // dil
        gathered.append(c.reshape(db, n_keys, dil * KV_ROWS, HD))
        gather_specs.append(pl.BlockSpec((None, n_keys, KV_ROWS, HD), lambda b: (b, 0, 0, 0)))
    row3 = lambda n: pl.BlockSpec((None, 1, n), lambda b: (b, 0, 0))
    new_spec = pl.BlockSpec((None, KV_ROWS, HD), lambda b: (b, 0, 0))
    res = pl.pallas_call(
        _sample_kernel,
        grid=(db,),
        in_specs=[pl.BlockSpec(z.shape, const2), pl.BlockSpec(za.shape, const2),
                  pl.BlockSpec(wa_pad.shape, const2), pl.BlockSpec((1, kw), const2),
                  pl.BlockSpec((1, vw), const2), pl.BlockSpec((1, HD), const2), pl.BlockSpec((1, HD), const2),
                  pl.BlockSpec((None, None, N_HEADS, HD, GLA_DV), lambda b: (0, b, 0, 0, 0))]
                 + gather_specs
                 + [pl.BlockSpec((None, N_MEM, KV_ROWS, HD), lambda b: (b, 0, 0, 0))],
        out_specs=[row3(vw), row3(w), row3(w),
                   pl.BlockSpec((None, None, N_HEADS, HD, GLA_DV), lambda b: (0, b, 0, 0, 0))] + [new_spec] * 3,
        out_shape=[jax.ShapeDtypeStruct((db, 1, vw), F32), jax.ShapeDtypeStruct((db, 1, w), F32),
                   jax.ShapeDtypeStruct((db, 1, w), F32), jax.ShapeDtypeStruct(state.shape, F32)]
                  + [jax.ShapeDtypeStruct((db, KV_ROWS, HD), F32)] * 3,
        compiler_params=_cparams(("arbitrary",)),
        name="sample_mixers",
    )(z, za, wa_pad, b_alpha.reshape(1, kw), g_gla_out.reshape(1, vw), cos2, sin2, state,
      *gathered, cache_mem.reshape(db, N_MEM, KV_ROWS, HD))
    ua, ub, uc, s_out = res[:4]
    return ua.reshape(db, vw), ub.reshape(db, w), uc.reshape(db, w), s_out, res[4:]


def _prep_w_in_kernel(wt_ref, xs_ref, g_ref, wm_ref, wg_ref, zs_ref, zas_ref, buf, ga_buf, hs_ref, sem, ga_sem,
                      *, c_ga, c_gt, rows):
    i = pl.program_id(0)
    n_gt = (wt_ref.shape[0] - c_gt) // rows
    n_lo = c_ga // rows

    def fetch(ii, slot):
        src = jnp.where(ii < n_gt, c_gt + ii * rows,
                        jnp.where(ii < n_gt + n_lo, (ii - n_gt) * rows, c_ga + GLA_RANK + (ii - n_gt - n_lo) * rows))
        return pltpu.make_async_copy(wt_ref.at[pl.ds(pl.multiple_of(src, GLA_RANK), rows)], buf.at[slot], sem.at[slot])

    ga_copy = pltpu.make_async_copy(wt_ref.at[pl.ds(c_ga, GLA_RANK)], ga_buf, ga_sem)

    @pl.when(i == 0)
    def _():
        fetch(0, 0).start()
        ga_copy.start()

    @pl.when(i + 1 < pl.num_programs(0))
    def _():
        fetch(i + 1, lax.rem(i + 1, 2)).start()

    @pl.when(i == 0)
    def _():
        x = xs_ref[...]
        y = x * lax.rsqrt(jnp.mean(x * x, axis=-1, keepdims=True) + EPS)
        hs_ref[...] = (y * g_ref[...]).astype(BF16)
        ga_copy.wait()
        wg_ref[0:GLA_RANK, :] = ga_buf[...].astype(BF16)
        wg_ref[GLA_RANK:, :] = jnp.zeros((LANES - GLA_RANK, wg_ref.shape[1]), BF16)
        zas_ref[...] = _dot_nt(hs_ref[...], wg_ref[...])

    slot = lax.rem(i, 2)
    fetch(i, slot).wait()
    wm_ref[...] = buf[slot].astype(BF16)
    zs_ref[...] = _dot_nt(hs_ref[...], wm_ref[...])


def _prep_w_in(wt, xs, g, *, rows):
    n_all, d = wt.shape
    db = xs.shape[0]
    c_ga = 2 * N_HEADS * HD + 2 * N_HEADS * GLA_DV
    c_gt = n_all - 3 * d
    assert n_all - GLA_RANK == N_MAIN and xs.shape[1] == d
    assert (n_all - c_gt) % rows == 0 and c_ga % rows == 0 and (c_gt - c_ga - GLA_RANK) % rows == 0
    const = lambda i: (0, 0)
    return pl.pallas_call(
        functools.partial(_prep_w_in_kernel, c_ga=c_ga, c_gt=c_gt, rows=rows),
        grid=(N_MAIN // rows,),
        in_specs=[pl.BlockSpec(memory_space=pl.ANY), pl.BlockSpec((db, d), const), pl.BlockSpec((1, d), const)],
        out_specs=[pl.BlockSpec((rows, d), lambda i: (i, 0)), pl.BlockSpec((LANES, d), const),
                   pl.BlockSpec((db, rows), lambda i: (0, i)), pl.BlockSpec((db, LANES), const)],
        out_shape=[jax.ShapeDtypeStruct((N_MAIN, d), BF16), jax.ShapeDtypeStruct((LANES, d), BF16),
                   jax.ShapeDtypeStruct((db, N_MAIN), F32), jax.ShapeDtypeStruct((db, LANES), F32)],
        scratch_shapes=[pltpu.VMEM((2, rows, d), F32), pltpu.VMEM((GLA_RANK, d), F32), pltpu.VMEM((db, d), BF16),
                        pltpu.SemaphoreType.DMA((2,)), pltpu.SemaphoreType.DMA(())],
        compiler_params=_cparams(("arbitrary",)),
        name="prep_w_in",
    )(wt, xs, g.reshape(1, d))


def kernel(x_prompt, x_sample, mem_prompt, state_gla, cache_swa_w128, cache_swa_w512, cache_swa_w2048, cache_mem_kv, g_norm, w_in, w_alpha2, b_alpha, g_gla_out, g_mem, w_mem_kv, w_proj_a, w_proj_b, w_proj_c, w_out, g_final):
    batch, seq, d = x_prompt.shape
    db, dec_seq, _ = x_sample.shape
    assert g_norm.shape[0] == 1 and dec_seq == 1

    xp = x_prompt.reshape(batch * seq, d)
    xs = x_sample.reshape(db, d)

    w_main, w_ga, zs, zas = _prep_w_in(jnp.swapaxes(w_in[0], 0, 1), xs, g_norm[0], rows=TILES["prep_rows"])
    wa_pad = jnp.pad(w_alpha2[0], ((0, LANES - GLA_RANK), (0, 0))).astype(BF16)
    wpa, wpb, wpc, wo = (w[0].astype(BF16) for w in (w_proj_a, w_proj_b, w_proj_c, w_out))

    cos_s, sin_s = _rope_tables(jnp.full((1,), PAST_LEN, jnp.int32))
    caches = (cache_swa_w128, cache_swa_w512, cache_swa_w2048)
    uas, ubs, ucs, gla_s, new_rows = _sample_mixers(
        zs, zas, wa_pad, b_alpha[0], g_gla_out[0], cos_s, sin_s, state_gla, caches, cache_mem_kv)
    y_sample = _final(uas, ubs, ucs, zs, xs, wpa, wpb, wpc, wo, g_final, tm=db).reshape(db, 1, d)

    z, za = _norm_matmul(xp, g_norm[0], w_main, w_ga, w_rows_out=True, out_dtype=BF16,
                         tm=TILES["proj_rows"], tn=TILES["proj_cols"])
    flat_caches = [c.reshape(db, w * KV_ROWS, HD) for c, (w, _) in zip(caches, SWA_PATTERNS)]
    ua, gla_p, *shifted = _gla_prompt(z, za, wa_pad, b_alpha[0], g_gla_out[0], batch, seq,
                                      (flat_caches, new_rows), t_blk=TILES["gla_rows"])
    swa_s = [o.reshape(c.shape) for o, c in zip(shifted, caches)]
    mem_kv = _norm_matmul(mem_prompt.reshape(batch * N_MEM, d), g_mem[0], w_mem_kv[0].astype(BF16),
                          tm=batch * N_MEM, tn=TILES["mem_cols"])
    cos_p, sin_p = _rope_tables_range(seq)
    swa_res = _swa_prompt(z, cos_p, sin_p, batch, seq)
    ub, kv_bufs = swa_res[0], swa_res[1:4]
    y_prompt = _final(ua, ub, None, z, xp, wpa, wpb, wpc, wo, g_final, tm=TILES["final_rows"], mem_kv=mem_kv,
                      rows_per_batch=seq).reshape(batch, seq, d)

    swa_p = [kv.reshape(1, batch, w, 2, N_HEADS, HD) for kv, (w, _) in zip(kv_bufs, SWA_PATTERNS)]
    mem_kv_prompt = mem_kv.reshape(1, batch, N_MEM, 2, N_HEADS, HD)
    return (y_prompt, y_sample, gla_p[None], swa_p[0], swa_p[1], swa_p[2], mem_kv_prompt,
            gla_s, swa_s[0], swa_s[1], swa_s[2])
```

```python
import functools

import numpy as np
import jax
import jax.numpy as jnp
from jax import lax
from jax.experimental import pallas as pl
from jax.experimental.pallas import tpu as pltpu

F32 = jnp.float32
BF16 = jnp.bfloat16

EPS = 1e-6
ROPE_THETA = 10000.0
NEG = -1e30
PAST_LEN = 16384

HD = 128
N_HEADS = 4
GLA_DV = 256
GLA_RANK = 16
GLA_TAU = 16.0
LOG2_E = 1.4426950408889634
GLA_CHUNK = 64
GLA_SUB = 16
GLA_UNROLL = 4
SWA_PATTERNS = ((128, 1), (512, 4), (2048, 16))
SWA_BLK = 128
SWA_STEP = 2048
SWA_UNROLL = 4
N_MEM = 256

LANES = 128
VMEM_LIMIT = 60000 * 1024
VMEM_LIMIT_FINAL = 62 * 1024 * 1024

TILES = dict(prep_rows=512, proj_rows=1024, proj_cols=2560, gla_rows=512, mem_cols=512, final_rows=512)

C_GT = 0
C_GQ, C_GK, C_GV, C_GR = 6144, 6656, 7168, 8192
C_SQ, C_SK, C_SV, C_SR = 9216, 10752, 12288, 13824
C_MQ, C_MR = 14336, 14848
N_MAIN = 15360


def _cparams(sem, vmem_limit=VMEM_LIMIT):
    return pltpu.CompilerParams(dimension_semantics=sem, vmem_limit_bytes=vmem_limit)


def _dot(a, b):
    return jnp.dot(a, b, preferred_element_type=F32)


def _dot_nt(a, b):
    return lax.dot_general(a, b, (((1,), (1,)), ((), ())), preferred_element_type=F32)


def _dot_tn(a, b):
    return lax.dot_general(a, b, (((0,), (0,)), ((), ())), preferred_element_type=F32)


def _silu(x):
    return x * jax.nn.sigmoid(x)


def _log_sigmoid(x):
    return jnp.minimum(x, 0.0) - jnp.log1p(jnp.exp(-jnp.abs(x)))


KV_ROWS = 2 * N_HEADS
SHIFT_ROWS = tuple((w - 1) * KV_ROWS for w, _ in SWA_PATTERNS)
SHIFT_OFFS = tuple(sum(SHIFT_ROWS[:g]) for g in range(len(SWA_PATTERNS)))


def _cache_shift_step(step, n_steps, ca, nr, co, stage, in_sem, out_sem, row_sem):
    n_grp = len(ca)
    n_batch = ca[0].shape[0]
    assert n_steps >= n_batch
    slot = lax.rem(step, 2)

    def in_copy(g, bb, sl):
        return pltpu.make_async_copy(ca[g].at[bb, pl.ds(KV_ROWS, SHIFT_ROWS[g])],
                                     stage.at[sl, pl.ds(SHIFT_OFFS[g], SHIFT_ROWS[g])], in_sem.at[sl, g])

    def out_copy(g, bb, sl):
        return pltpu.make_async_copy(stage.at[sl, pl.ds(SHIFT_OFFS[g], SHIFT_ROWS[g])],
                                     co[g].at[bb, pl.ds(0, SHIFT_ROWS[g])], out_sem.at[sl, g])

    def row_copy(g, bb, sl):
        return pltpu.make_async_copy(nr[g].at[bb], co[g].at[bb, pl.ds(SHIFT_ROWS[g], KV_ROWS)], row_sem.at[sl, g])

    def finish_writes(bb, sl):
        for g in range(n_grp):
            out_copy(g, bb, sl).wait()
            row_copy(g, bb, sl).wait()

    @pl.when(step == 0)
    def _():
        for g in range(n_grp):
            in_copy(g, 0, 0).start()

    @pl.when((step >= 1) & (step <= n_batch))
    def _():
        finish_writes(step - 1, 1 - slot)

    @pl.when(step + 1 < n_batch)
    def _():
        for g in range(n_grp):
            in_copy(g, step + 1, 1 - slot).start()

    @pl.when(step < n_batch)
    def _():
        for g in range(n_grp):
            in_copy(g, step, slot).wait()
            out_copy(g, step, slot).start()
            row_copy(g, step, slot).start()

    if n_steps == n_batch:
        @pl.when(step == n_steps - 1)
        def _():
            finish_writes(step, slot)


def _norm_matmul_kernel(*refs, has_extra, row_chunk, w_rows_out):
    mm = _dot_nt if w_rows_out else _dot
    if has_extra:
        x_ref, g_ref, w_ref, wx_ref, o_ref, ox_ref, h_ref = refs
    else:
        x_ref, g_ref, w_ref, o_ref, h_ref = refs
    tm = x_ref.shape[0]

    @pl.when(pl.program_id(1) == 0)
    def _():
        for c in range(tm // row_chunk):
            rs = slice(c * row_chunk, (c + 1) * row_chunk)
            x = x_ref[rs, :]
            y = x * lax.rsqrt(jnp.mean(x * x, axis=-1, keepdims=True) + EPS)
            h_ref[rs, :] = (y * g_ref[...]).astype(BF16)
            if has_extra:
                ox_ref[rs, :] = mm(h_ref[rs, :], wx_ref[...])

    o_ref[...] = mm(h_ref[...], w_ref[...]).astype(o_ref.dtype)


def _norm_matmul(x, g, w, wx=None, *, w_rows_out=False, out_dtype=F32, tm, tn):
    m, d = x.shape
    n = w.shape[0] if w_rows_out else w.shape[1]
    assert m % tm == 0 and n % tn == 0
    has_extra = wx is not None
    row_chunk = min(tm, 128)
    w_spec = pl.BlockSpec((tn, d), lambda i, j: (j, 0)) if w_rows_out else pl.BlockSpec((d, tn), lambda i, j: (0, j))
    in_specs = [pl.BlockSpec((tm, d), lambda i, j: (i, 0)),
                pl.BlockSpec((1, d), lambda i, j: (0, 0)),
                w_spec]
    out_specs = [pl.BlockSpec((tm, tn), lambda i, j: (i, j))]
    out_shape = [jax.ShapeDtypeStruct((m, n), out_dtype)]
    args = [x, g.reshape(1, d), w]
    if has_extra:
        nx = wx.shape[0] if w_rows_out else wx.shape[1]
        in_specs.append(pl.BlockSpec(wx.shape, lambda i, j: (0, 0)))
        out_specs.append(pl.BlockSpec((tm, nx), lambda i, j: (i, 0)))
        out_shape.append(jax.ShapeDtypeStruct((m, nx), F32))
        args.append(wx)
    res = pl.pallas_call(
        functools.partial(_norm_matmul_kernel, has_extra=has_extra, row_chunk=row_chunk, w_rows_out=w_rows_out),
        grid=(m // tm, n // tn),
        in_specs=in_specs, out_specs=out_specs, out_shape=out_shape,
        scratch_shapes=[pltpu.VMEM((tm, d), BF16)],
        compiler_params=_cparams(("arbitrary", "arbitrary")),
        name="norm_matmul",
    )(*args)
    return res if has_extra else res[0]


def _diag_select_matrix():
    rows = np.arange(GLA_SUB * HD)[:, None] // HD
    cols = np.arange(LANES)[None, :] % GLA_SUB
    return jnp.asarray((rows == cols).astype(np.float32), dtype=BF16)


def _gla_chunks(rows_list, zg_ref, ga_ref, wa_ref, ba_ref, gg_ref, em_ref, st_ref, u_ref):
    C, SUB, n_sub = GLA_CHUNK, GLA_SUB, GLA_CHUNK // GLA_SUB
    kw = N_HEADS * HD
    c_k, c_v, c_r = kw, 2 * kw, 2 * kw + N_HEADS * GLA_DV
    row = lax.broadcasted_iota(jnp.int32, (C, C), 0)
    col = lax.broadcasted_iota(jnp.int32, (C, C), 1)
    tri = (col <= row).astype(F32)
    row_c = lax.broadcasted_iota(jnp.int32, (C, HD), 0)
    half = SUB // 2
    half_row = lax.broadcasted_iota(jnp.int32, (half, HD), 0)
    lane_c = lax.broadcasted_iota(jnp.int32, (SUB, C), 1)

    xas = [_dot(ga_ref[rows, :].astype(BF16), wa_ref[...]) + ba_ref[...] for rows in rows_list]
    b_alls = [jnp.dot(tri, _log_sigmoid(xa) * (LOG2_E / GLA_TAU), preferred_element_type=F32,
                      precision=lax.Precision.HIGHEST) for xa in xas]

    chunks = []
    for rows, b_all in zip(rows_list, b_alls):
        a_off = []
        per_head = []
        for h in range(N_HEADS):
            q = zg_ref[rows, h * HD:(h + 1) * HD].astype(F32) * (HD ** -0.5)
            k = zg_ref[rows, c_k + h * HD:c_k + (h + 1) * HD].astype(F32)
            v = zg_ref[rows, c_v + h * GLA_DV:c_v + (h + 1) * GLA_DV]
            b = b_all[:, h * HD:(h + 1) * HD]
            for i in range(n_sub):
                if i == 0:
                    a_off.append(jnp.zeros((SUB, C), F32))
                else:
                    sl = slice(i * SUB, (i + 1) * SUB)
                    b_ref_row = b[i * SUB - 1:i * SUB, :]
                    qs = q[sl] * jnp.exp2(b[sl] - b_ref_row)
                    ks = k * jnp.exp2(jnp.where(row_c < i * SUB, b_ref_row - b, -jnp.inf))
                    a_off.append(_dot_nt(qs.astype(BF16), ks.astype(BF16)))
            st = st_ref[h]
            o_inter = _dot_nt((q * jnp.exp2(b)).astype(BF16), st.astype(BF16))
            b_end = b[C - 1:C, :]
            kd = k * jnp.exp2(b_end - b)
            st_ref[h] = st * jnp.exp2(b_end) + _dot_tn(v.astype(BF16), kd.astype(BF16))
            per_head.append((q, k, v, b, o_inter))
        chunks.append((rows, a_off, per_head))

    p_rows = []
    for _, _, per_head in chunks:
        for h in range(N_HEADS):
            q, k, _, b, _ = per_head[h]
            for i in range(n_sub):
                top, bot = slice(i * SUB, i * SUB + half), slice(i * SUB + half, (i + 1) * SUB)
                b_mid = b[i * SUB + half - 1:i * SUB + half, :]
                qs_bot = q[bot] * jnp.exp2(b[bot] - b_mid)
                ks_top = k[top] * jnp.exp2(b_mid - b[top])
                slabs = []
                for s in range(SUB):
                    rs = top if s < half else bot
                    s0 = s % half
                    e = jnp.where(half_row >= s0, b[rs] - b[rs][s0:s0 + 1, :], -jnp.inf)
                    diag = (q[rs] * k[rs][s0:s0 + 1, :]) * jnp.exp2(e)
                    if s < half:
                        slabs.append(jnp.concatenate([diag, qs_bot * ks_top[s0:s0 + 1, :]], axis=0))
                    else:
                        slabs.append(jnp.concatenate([jnp.zeros((half, HD), F32), diag], axis=0))
                p_rows.append(jnp.concatenate(slabs, axis=1).astype(BF16))
    r_all = _dot(jnp.concatenate(p_rows, axis=0), em_ref[...])

    for ci, (rows, a_off, per_head) in enumerate(chunks):
        for h in range(N_HEADS):
            _, _, v, _, o_inter = per_head[h]
            a_rows = []
            for i in range(n_sub):
                idx = h * n_sub + i
                r0 = (ci * N_HEADS * n_sub + idx) * SUB
                in_blk = (lane_c >= i * SUB) & (lane_c < (i + 1) * SUB)
                a_rows.append(a_off[idx] + jnp.where(in_blk, r_all[r0:r0 + SUB, :C], 0.0))
            a = jnp.concatenate(a_rows, axis=0)
            o = _dot(a.astype(BF16), v.astype(BF16)) + o_inter
            y = o * lax.rsqrt(jnp.mean(o * o, axis=-1, keepdims=True) + EPS)
            y = y * gg_ref[:, h * GLA_DV:(h + 1) * GLA_DV]
            gate = zg_ref[rows, c_r + h * GLA_DV:c_r + (h + 1) * GLA_DV].astype(F32)
            u_ref[rows, h * GLA_DV:(h + 1) * GLA_DV] = (y * _silu(gate)).astype(u_ref.dtype)


def _gla_kernel(*refs, n_shift, n_steps):
    zg_ref, ga_ref, wa_ref, ba_ref, gg_ref, em_ref = refs[:6]
    n_in = 6 + 2 * n_shift
    u_ref, sout_ref = refs[n_in:n_in + 2]
    st_ref = refs[n_in + 2 + n_shift]
    t_blk = pl.program_id(1)
    n_chunks = zg_ref.shape[0] // GLA_CHUNK
    assert n_chunks % GLA_UNROLL == 0

    if n_shift:
        stage, in_sem, out_sem, row_sem = refs[n_in + 3 + n_shift:]
        _cache_shift_step(pl.program_id(0) * pl.num_programs(1) + t_blk, n_steps, refs[6:6 + n_shift],
                          refs[6 + n_shift:n_in], refs[n_in + 2:n_in + 2 + n_shift], stage, in_sem, out_sem, row_sem)

    @pl.when(t_blk == 0)
    def _():
        st_ref[...] = jnp.zeros_like(st_ref)

    def chunks(c, carry):
        rows_list = [pl.ds(pl.multiple_of((c * GLA_UNROLL + u) * GLA_CHUNK, GLA_CHUNK), GLA_CHUNK)
                     for u in range(GLA_UNROLL)]
        _gla_chunks(rows_list, zg_ref, ga_ref, wa_ref, ba_ref, gg_ref, em_ref, st_ref, u_ref)
        return carry

    lax.fori_loop(0, n_chunks // GLA_UNROLL, chunks, 0)

    @pl.when(t_blk == pl.num_programs(1) - 1)
    def _():
        for h in range(N_HEADS):
            sout_ref[h] = st_ref[h].T


def _gla_prompt(z, za, wa_pad, b_alpha, g_gla_out, batch, seq, shift=None, *, t_blk):
    nt = seq // t_blk
    assert seq % t_blk == 0 and t_blk % GLA_CHUNK == 0
    kw = N_HEADS * HD
    vw = N_HEADS * GLA_DV
    gla_w = 2 * kw + 2 * vw
    assert C_GQ % gla_w == 0 and (C_GK, C_GV, C_GR) == (C_GQ + kw, C_GQ + 2 * kw, C_GQ + 2 * kw + vw)
    const = lambda b, t: (0, 0)
    in_specs = [pl.BlockSpec((t_blk, gla_w), lambda b, t: (b * nt + t, C_GQ // gla_w)),
                pl.BlockSpec((t_blk, LANES), lambda b, t: (b * nt + t, 0)),
                pl.BlockSpec((LANES, kw), const), pl.BlockSpec((1, kw), const), pl.BlockSpec((1, vw), const),
                pl.BlockSpec((GLA_SUB * HD, LANES), const)]
    out_specs = [pl.BlockSpec((t_blk, vw), lambda b, t: (b * nt + t, 0)),
                 pl.BlockSpec((None, N_HEADS, HD, GLA_DV), lambda b, t: (b, 0, 0, 0))]
    out_shape = [jax.ShapeDtypeStruct((batch * seq, vw), BF16),
                 jax.ShapeDtypeStruct((batch, N_HEADS, HD, GLA_DV), F32)]
    scratch = [pltpu.VMEM((N_HEADS, GLA_DV, HD), F32)]
    args = [z, za, wa_pad, b_alpha.reshape(1, kw), g_gla_out.reshape(1, vw), _diag_select_matrix()]
    n_shift = 0
    if shift is not None:
        caches, new_rows = shift
        n_shift = len(caches)
        any_spec = pl.BlockSpec(memory_space=pl.ANY)
        in_specs += [any_spec] * n_shift + [pl.BlockSpec(r.shape, lambda b, t: (0, 0, 0)) for r in new_rows]
        out_specs += [any_spec] * n_shift
        out_shape += [jax.ShapeDtypeStruct(c.shape, c.dtype) for c in caches]
        args += list(caches) + list(new_rows)
        scratch += [pltpu.VMEM((2, sum(SHIFT_ROWS), HD), F32)] + [pltpu.SemaphoreType.DMA((2, n_shift))] * 3
    return pl.pallas_call(
        functools.partial(_gla_kernel, n_shift=n_shift, n_steps=batch * nt),
        grid=(batch, nt),
        in_specs=in_specs, out_specs=out_specs, out_shape=out_shape, scratch_shapes=scratch,
        compiler_params=_cparams(("arbitrary", "arbitrary")),
        name="gla",
    )(*args)


def _rope_tables(pos):
    half = HD // 2
    inv = ROPE_THETA ** (-jnp.arange(half, dtype=F32) / half)
    ang = pos.astype(F32)[:, None] * inv[None, :]
    cos, sin = jnp.cos(ang), jnp.sin(ang)
    return jnp.concatenate([cos, cos], axis=1), jnp.concatenate([-sin, sin], axis=1)


def _rope_tables_range(n, blk=128):
    assert n % blk == 0
    half = HD // 2
    inv = ROPE_THETA ** (-jnp.arange(half, dtype=F32) / half)
    inv2 = jnp.concatenate([inv, inv])
    sign = jnp.concatenate([-jnp.ones((half,), F32), jnp.ones((half,), F32)])
    ang_a = (jnp.arange(n // blk, dtype=jnp.int32) * blk).astype(F32)[:, None] * inv2[None, :]
    ang_b = jnp.arange(blk, dtype=jnp.int32).astype(F32)[:, None] * inv2[None, :]
    ca, sa, cb, sb = jnp.cos(ang_a)[:, None], jnp.sin(ang_a)[:, None], jnp.cos(ang_b)[None], jnp.sin(ang_b)[None]
    return (ca * cb - sa * sb).reshape(n, HD), ((sa * cb + ca * sb) * sign).reshape(n, HD)


def _rope(x, cos2, sin2):
    return x * cos2 + pltpu.roll(x, HD // 2, axis=1) * sin2


def _swa_kernel(*refs):
    (q0, q1, q2, k0, k1, k2, v0, v1, v2, sr_ref, cos_ref, sin_ref,
     u_ref, kv0, kv1, kv2,
     q_s, k_s, v_s, o_s, l_s) = refs
    q_in, k_in, v_in = (q0, q1, q2), (k0, k1, k2), (v0, v1, v2)
    kv_out = (kv0, kv1, kv2)
    i = pl.program_id(2)
    T = SWA_STEP
    n_grp = len(SWA_PATTERNS)

    @pl.when(i == 0)
    def _():
        for g, (_, dil) in enumerate(SWA_PATTERNS):
            unit = dil * SWA_BLK
            k_s[g, T - unit:T, :] = jnp.zeros((unit, HD), F32)
            v_s[g, T - unit:T, :] = jnp.zeros((unit, HD), F32)

    @pl.when(i > 0)
    def _():
        for g, (_, dil) in enumerate(SWA_PATTERNS):
            unit = dil * SWA_BLK
            k_s[g, T - unit:T, :] = k_s[g, 2 * T - unit:2 * T, :]
            v_s[g, T - unit:T, :] = v_s[g, 2 * T - unit:2 * T, :]

    cos2, sin2 = cos_ref[...], sin_ref[...]
    for g in range(n_grp):
        q_s[g] = _rope(q_in[g][...].astype(F32), cos2, sin2) * (HD ** -0.5)
        k_s[g, T:2 * T, :] = _rope(k_in[g][...].astype(F32), cos2, sin2)
        v_s[g, T:2 * T, :] = v_in[g][...].astype(F32)

    qi = lax.broadcasted_iota(jnp.int32, (SWA_BLK, 2 * SWA_BLK), 0)
    ki = lax.broadcasted_iota(jnp.int32, (SWA_BLK, 2 * SWA_BLK), 1)
    delta = SWA_BLK + qi - ki

    def blocks(it, carry):
        todo = []
        for g, (win, dil) in enumerate(SWA_PATTERNS):
            for j in range(SWA_UNROLL):
                n = it * SWA_UNROLL + j
                unit = dil * SWA_BLK
                u = lax.shift_right_logical(n, dil.bit_length() - 1)
                r = lax.bitwise_and(n, dil - 1)

                def rows(start, size, dil=dil):
                    return pl.ds(start, size) if dil == 1 else pl.ds(start, size, stride=dil)

                q = q_s[g, rows(u * unit + r, SWA_BLK), :]
                kk = k_s[g, rows(T + (u - 1) * unit + r, 2 * SWA_BLK), :]
                s = _dot_nt(q.astype(BF16), kk.astype(BF16))
                ki_min = jnp.where((i * (T // unit) + u) == 0, SWA_BLK, 0)
                valid = (delta >= 0) & (delta <= win // dil) & (ki >= ki_min)
                todo.append((g, rows(u * unit + r, SWA_BLK), rows(T + (u - 1) * unit + r, 2 * SWA_BLK), s, valid))
        soft = []
        for g, q_rows, kv_rows, s, valid in todo:
            s = jnp.where(valid, s, NEG)
            m = jnp.max(s, axis=-1, keepdims=True)
            p = jnp.exp(s - m)
            den = jnp.sum(p, axis=-1, keepdims=True)
            soft.append((g, q_rows, kv_rows, p.astype(BF16), den, m + jnp.log(den)))
        for g, q_rows, kv_rows, p, den, lse in soft:
            o_s[g, q_rows, :] = _dot(p, v_s[g, kv_rows, :].astype(BF16)) / den
            l_s[g, q_rows, :] = jnp.broadcast_to(lse, (SWA_BLK, HD))
        return carry

    lax.fori_loop(0, T // SWA_BLK // SWA_UNROLL, blocks, 0)

    rc = 256

    def merge(c, carry):
        r0 = pl.multiple_of(c * rc, rc)
        rr = pl.ds(r0, rc)
        ls = [l_s[g, rr, :] for g in range(n_grp)]
        mx = functools.reduce(jnp.maximum, ls)
        es = [jnp.exp(l - mx) for l in ls]
        tot = functools.reduce(lambda a, b: a + b, es)
        ob = functools.reduce(lambda a, b: a + b, [(es[g] / tot) * o_s[g, rr, :] for g in range(n_grp)])
        u_ref[rr, :] = (ob * _silu(sr_ref[rr, :].astype(F32))).astype(u_ref.dtype)
        return carry

    lax.fori_loop(0, T // rc, merge, 0)

    @pl.when(i == pl.num_programs(2) - 1)
    def _():
        for g, (win, _) in enumerate(SWA_PATTERNS):
            head = pl.program_id(1)
            kv_out[g][pl.ds(head, win, stride=KV_ROWS), :] = k_s[g, 2 * T - win:2 * T, :]
            kv_out[g][pl.ds(N_HEADS + head, win, stride=KV_ROWS), :] = v_s[g, 2 * T - win:2 * T, :]


def _swa_prompt(z, cos2, sin2, batch, seq):
    T = SWA_STEP
    assert seq % T == 0 and all(w <= T for w, _ in SWA_PATTERNS)
    nt = seq // T
    w_grp = N_HEADS * HD

    def zspec(col0, g):
        cb0 = (col0 + g * w_grp) // HD
        return pl.BlockSpec((T, HD), lambda b, j, i: (b * nt + i, cb0 + j))

    in_specs = ([zspec(C_SQ, g) for g in range(3)] + [zspec(C_SK, g) for g in range(3)]
                + [zspec(C_SV, g) for g in range(3)] + [zspec(C_SR, 0)]
                + [pl.BlockSpec((T, HD), lambda b, j, i: (i, 0))] * 2)
    buf_specs = [pl.BlockSpec((None, w * KV_ROWS, HD), lambda b, j, i: (b, 0, 0), pipeline_mode=pl.Buffered(1))
                 for w, _ in SWA_PATTERNS]
    buf_shapes = [jax.ShapeDtypeStruct((batch, w * KV_ROWS, HD), F32) for w, _ in SWA_PATTERNS]
    return pl.pallas_call(
        _swa_kernel,
        grid=(batch, N_HEADS, nt),
        in_specs=in_specs,
        out_specs=[pl.BlockSpec((T, HD), lambda b, j, i: (b * nt + i, j))] + buf_specs,
        out_shape=[jax.ShapeDtypeStruct((batch * seq, w_grp), BF16)] + buf_shapes,
        scratch_shapes=[pltpu.VMEM((3, T, HD), F32), pltpu.VMEM((3, 2 * T, HD), F32),
                        pltpu.VMEM((3, 2 * T, HD), F32), pltpu.VMEM((3, T, HD), F32),
                        pltpu.VMEM((3, T, HD), F32)],
        compiler_params=_cparams(("arbitrary", "arbitrary", "arbitrary")),
        name="swa",
    )(*([z] * 10), cos2, sin2)


def _final_kernel(*refs, fuse_mem):
    if fuse_mem:
        ua_ref, ub_ref, mq_ref, mr_ref, kv_ref, gt_ref, x_ref, wa_ref, wb_ref, wc_ref, wo_ref, gf_ref, y_ref = refs
    else:
        ua_ref, ub_ref, uc_ref, gt_ref, x_ref, wa_ref, wb_ref, wc_ref, wo_ref, gf_ref, y_ref = refs
    d = x_ref.shape[1]
    w = N_HEADS * HD
    if fuse_mem:
        scores = [_dot_nt((mq_ref[:, h * HD:(h + 1) * HD].astype(F32) * (HD ** -0.5)).astype(BF16),
                          kv_ref[:, h * HD:(h + 1) * HD].astype(BF16)) for h in range(N_HEADS)]
    ya = _dot(ua_ref[...].astype(BF16), wa_ref[...])
    yb = _dot(ub_ref[...].astype(BF16), wb_ref[...])
    if fuse_mem:
        ucs = []
        for h, s in enumerate(scores):
            e = jnp.exp(s - jnp.max(s, axis=-1, keepdims=True))
            p = e / jnp.sum(e, axis=-1, keepdims=True)
            o = _dot(p.astype(BF16), kv_ref[:, w + h * HD:w + (h + 1) * HD].astype(BF16))
            ucs.append((o * _silu(mr_ref[:, h * HD:(h + 1) * HD].astype(F32))).astype(BF16))
        uc = jnp.concatenate(ucs, axis=1)
    else:
        uc = uc_ref[...].astype(BF16)
    yc = _dot(uc, wc_ref[...])
    mix = (jax.nn.sigmoid(gt_ref[:, 0:d].astype(F32)) * ya + jax.nn.sigmoid(gt_ref[:, d:2 * d].astype(F32)) * yb
           + jax.nn.sigmoid(gt_ref[:, 2 * d:3 * d].astype(F32)) * yc)
    xo = x_ref[...] + _dot(mix.astype(BF16), wo_ref[...])
    y = xo * lax.rsqrt(jnp.mean(xo * xo, axis=-1, keepdims=True) + EPS)
    y_ref[...] = y * gf_ref[...]


def _final(ua, ub, uc, z, x, wa, wb, wc, wo, g_final, *, tm, mem_kv=None, rows_per_batch=None):
    m, d = x.shape
    w = N_HEADS * HD
    const = lambda i: (0, 0)
    resident = dict(pipeline_mode=pl.Buffered(1))
    fuse_mem = uc is None
    if fuse_mem:
        assert rows_per_batch % tm == 0
        mem_specs = [pl.BlockSpec((tm, w), lambda i: (i, C_MQ // w)), pl.BlockSpec((tm, w), lambda i: (i, C_MR // w)),
                     pl.BlockSpec((N_MEM, 2 * w), lambda i: (i // (rows_per_batch // tm), 0))]
        mem_args = [z, z, mem_kv]
    else:
        mem_specs = [pl.BlockSpec((tm, uc.shape[1]), lambda i: (i, 0))]
        mem_args = [uc]
    return pl.pallas_call(
        functools.partial(_final_kernel, fuse_mem=fuse_mem),
        grid=(m // tm,),
        in_specs=[pl.BlockSpec((tm, ua.shape[1]), lambda i: (i, 0)),
                  pl.BlockSpec((tm, ub.shape[1]), lambda i: (i, 0))] + mem_specs + [
                  pl.BlockSpec((tm, 3 * d), lambda i: (i, C_GT // (3 * d))),
                  pl.BlockSpec((tm, d), lambda i: (i, 0)),
                  pl.BlockSpec(wa.shape, const, **resident),
                  pl.BlockSpec(wb.shape, const, **resident),
                  pl.BlockSpec(wc.shape, const, **resident),
                  pl.BlockSpec(wo.shape, const, **resident),
                  pl.BlockSpec((1, d), const)],
        out_specs=pl.BlockSpec((tm, d), lambda i: (i, 0)),
        out_shape=jax.ShapeDtypeStruct((m, d), F32),
        compiler_params=_cparams(("arbitrary",), VMEM_LIMIT_FINAL),
        name="final",
    )(ua, ub, *mem_args, z, x, wa, wb, wc, wo, g_final.reshape(1, d))


def _heads_rows(row, col0):
    return jnp.concatenate([row[:, col0 + h * HD:col0 + (h + 1) * HD] for h in range(N_HEADS)], axis=0)


def _decode_attention(q4, kk, vv, k_new=None, v_new=None):
    s = jnp.sum(kk * q4[None], axis=-1, keepdims=True)
    m = jnp.max(s, axis=0)
    if k_new is not None:
        s_new = jnp.sum(k_new * q4, axis=-1, keepdims=True)
        m = jnp.maximum(m, s_new)
    p = jnp.exp(s - m[None])
    den = jnp.sum(p, axis=0)
    acc = jnp.sum(p * vv, axis=0)
    if k_new is not None:
        p_new = jnp.exp(s_new - m)
        den = den + p_new
        acc = acc + p_new * v_new
    return acc / den, m + jnp.log(den)


def _sample_kernel(*refs):
    (z_ref, za_ref, wa_ref, ba_ref, gg_ref, cos_ref, sin_ref, st_ref, cg0, cg1, cg2, cm_ref,
     ua_ref, ub_ref, uc_ref, sout_ref, nr0, nr1, nr2) = refs
    cg, nr = (cg0, cg1, cg2), (nr0, nr1, nr2)
    b = pl.program_id(0)
    zrow = z_ref[pl.ds(b, 1), :]

    ga8 = jnp.broadcast_to(za_ref[pl.ds(b, 1), :], (8, LANES))
    xa = _dot(ga8.astype(BF16), wa_ref[...])[0:1, :] + ba_ref[...]
    a_row = jnp.exp(_log_sigmoid(xa) / GLA_TAU)
    eye = (lax.broadcasted_iota(jnp.int32, (HD, HD), 0) == lax.broadcasted_iota(jnp.int32, (HD, HD), 1))

    def col(row_vec):
        return jnp.sum(jnp.where(eye, jnp.broadcast_to(row_vec, (HD, HD)), 0.0), axis=1, keepdims=True)

    for h in range(N_HEADS):
        q = zrow[:, C_GQ + h * HD:C_GQ + (h + 1) * HD] * (HD ** -0.5)
        k = zrow[:, C_GK + h * HD:C_GK + (h + 1) * HD]
        v = zrow[:, C_GV + h * GLA_DV:C_GV + (h + 1) * GLA_DV]
        s_new = col(a_row[:, h * HD:(h + 1) * HD]) * st_ref[h] + col(k) * v
        sout_ref[h] = s_new
        o = jnp.sum(col(q) * s_new, axis=0, keepdims=True)
        y = o * lax.rsqrt(jnp.mean(o * o, axis=-1, keepdims=True) + EPS)
        y = y * gg_ref[:, h * GLA_DV:(h + 1) * GLA_DV]
        gate = zrow[:, C_GR + h * GLA_DV:C_GR + (h + 1) * GLA_DV]
        ua_ref[:, h * GLA_DV:(h + 1) * GLA_DV] = y * _silu(gate)

    cos2, sin2 = cos_ref[...], sin_ref[...]
    outs, lses = [], []
    w_grp = N_HEADS * HD
    for g in range(3):
        q4 = _rope(_heads_rows(zrow, C_SQ + g * w_grp), cos2, sin2) * (HD ** -0.5)
        k4 = _rope(_heads_rows(zrow, C_SK + g * w_grp), cos2, sin2)
        v4 = _heads_rows(zrow, C_SV + g * w_grp)
        nr[g][0:N_HEADS, :] = k4
        nr[g][N_HEADS:KV_ROWS, :] = v4
        o, lse = _decode_attention(q4, cg[g][:, 0:N_HEADS, :], cg[g][:, N_HEADS:KV_ROWS, :], k4, v4)
        outs.append(o)
        lses.append(lse)
    mx = functools.reduce(jnp.maximum, lses)
    es = [jnp.exp(l - mx) for l in lses]
    tot = es[0] + es[1] + es[2]
    ob = (es[0] / tot) * outs[0] + (es[1] / tot) * outs[1] + (es[2] / tot) * outs[2]
    for h in range(N_HEADS):
        gate = zrow[:, C_SR + h * HD:C_SR + (h + 1) * HD]
        ub_ref[:, h * HD:(h + 1) * HD] = ob[h:h + 1, :] * _silu(gate)

    qm = _heads_rows(zrow, C_MQ) * (HD ** -0.5)
    oc, _ = _decode_attention(qm, cm_ref[:, 0:N_HEADS, :], cm_ref[:, N_HEADS:KV_ROWS, :])
    for h in range(N_HEADS):
        gate = zrow[:, C_MR + h * HD:C_MR + (h + 1) * HD]
        uc_ref[:, h * HD:(h + 1) * HD] = oc[h:h + 1, :] * _silu(gate)


def _sample_mixers(z, za, wa_pad, b_alpha, g_gla_out, cos2, sin2, state, caches, cache_mem):
    db = z.shape[0]
    kw, vw, w = N_HEADS * HD, N_HEADS * GLA_DV, N_HEADS * HD
    const2 = lambda b: (0, 0)
    gathered, gather_specs = [], []
    for c, (win, dil) in zip(caches, SWA_PATTERNS):
        assert c.shape == (1, db, win, 2, N_HEADS, HD) and PAST_LEN >= win
        n_keys = win // dil
        gathered.append(c.reshape(db, n_keys, dil * KV_ROWS, HD))
        gather_specs.append(pl.BlockSpec((None, n_keys, KV_ROWS, HD), lambda b: (b, 0, 0, 0)))
    row3 = lambda n: pl.BlockSpec((None, 1, n), lambda b: (b, 0, 0))
    new_spec = pl.BlockSpec((None, KV_ROWS, HD), lambda b: (b, 0, 0))
    res = pl.pallas_call(
        _sample_kernel,
        grid=(db,),
        in_specs=[pl.BlockSpec(z.shape, const2), pl.BlockSpec(za.shape, const2),
                  pl.BlockSpec(wa_pad.shape, const2), pl.BlockSpec((1, kw), const2),
                  pl.BlockSpec((1, vw), const2), pl.BlockSpec((1, HD), const2), pl.BlockSpec((1, HD), const2),
                  pl.BlockSpec((None, None, N_HEADS, HD, GLA_DV), lambda b: (0, b, 0, 0, 0))]
                 + gather_specs
                 + [pl.BlockSpec((None, N_MEM, KV_ROWS, HD), lambda b: (b, 0, 0, 0))],
        out_specs=[row3(vw), row3(w), row3(w),
                   pl.BlockSpec((None, None, N_HEADS, HD, GLA_DV), lambda b: (0, b, 0, 0, 0))] + [new_spec] * 3,
        out_shape=[jax.ShapeDtypeStruct((db, 1, vw), F32), jax.ShapeDtypeStruct((db, 1, w), F32),
                   jax.ShapeDtypeStruct((db, 1, w), F32), jax.ShapeDtypeStruct(state.shape, F32)]
                  + [jax.ShapeDtypeStruct((db, KV_ROWS, HD), F32)] * 3,
        compiler_params=_cparams(("arbitrary",)),
        name="sample_mixers",
    )(z, za, wa_pad, b_alpha.reshape(1, kw), g_gla_out.reshape(1, vw), cos2, sin2, state,
      *gathered, cache_mem.reshape(db, N_MEM, KV_ROWS, HD))
    ua, ub, uc, s_out = res[:4]
    return ua.reshape(db, vw), ub.reshape(db, w), uc.reshape(db, w), s_out, res[4:]


def _prep_w_in_kernel(wt_ref, xs_ref, g_ref, wm_ref, wg_ref, zs_ref, zas_ref, buf, ga_buf, hs_ref, sem, ga_sem,
                      *, c_ga, c_gt, rows):
    i = pl.program_id(0)
    n_gt = (wt_ref.shape[0] - c_gt) // rows
    n_lo = c_ga // rows

    def fetch(ii, slot):
        src = jnp.where(ii < n_gt, c_gt + ii * rows,
                        jnp.where(ii < n_gt + n_lo, (ii - n_gt) * rows, c_ga + GLA_RANK + (ii - n_gt - n_lo) * rows))
        return pltpu.make_async_copy(wt_ref.at[pl.ds(pl.multiple_of(src, GLA_RANK), rows)], buf.at[slot], sem.at[slot])

    ga_copy = pltpu.make_async_copy(wt_ref.at[pl.ds(c_ga, GLA_RANK)], ga_buf, ga_sem)

    @pl.when(i == 0)
    def _():
        fetch(0, 0).start()
        ga_copy.start()

    @pl.when(i + 1 < pl.num_programs(0))
    def _():
        fetch(i + 1, lax.rem(i + 1, 2)).start()

    @pl.when(i == 0)
    def _():
        x = xs_ref[...]
        y = x * lax.rsqrt(jnp.mean(x * x, axis=-1, keepdims=True) + EPS)
        hs_ref[...] = (y * g_ref[...]).astype(BF16)
        ga_copy.wait()
        wg_ref[0:GLA_RANK, :] = ga_buf[...].astype(BF16)
        wg_ref[GLA_RANK:, :] = jnp.zeros((LANES - GLA_RANK, wg_ref.shape[1]), BF16)
        zas_ref[...] = _dot_nt(hs_ref[...], wg_ref[...])

    slot = lax.rem(i, 2)
    fetch(i, slot).wait()
    wm_ref[...] = buf[slot].astype(BF16)
    zs_ref[...] = _dot_nt(hs_ref[...], wm_ref[...])


def _prep_w_in(wt, xs, g, *, rows):
    n_all, d = wt.shape
    db = xs.shape[0]
    c_ga = 2 * N_HEADS * HD + 2 * N_HEADS * GLA_DV
    c_gt = n_all - 3 * d
    assert n_all - GLA_RANK == N_MAIN and xs.shape[1] == d
    assert (n_all - c_gt) % rows == 0 and c_ga % rows == 0 and (c_gt - c_ga - GLA_RANK) % rows == 0
    const = lambda i: (0, 0)
    return pl.pallas_call(
        functools.partial(_prep_w_in_kernel, c_ga=c_ga, c_gt=c_gt, rows=rows),
        grid=(N_MAIN // rows,),
        in_specs=[pl.BlockSpec(memory_space=pl.ANY), pl.BlockSpec((db, d), const), pl.BlockSpec((1, d), const)],
        out_specs=[pl.BlockSpec((rows, d), lambda i: (i, 0)), pl.BlockSpec((LANES, d), const),
                   pl.BlockSpec((db, rows), lambda i: (0, i)), pl.BlockSpec((db, LANES), const)],
        out_shape=[jax.ShapeDtypeStruct((N_MAIN, d), BF16), jax.ShapeDtypeStruct((LANES, d), BF16),
                   jax.ShapeDtypeStruct((db, N_MAIN), F32), jax.ShapeDtypeStruct((db, LANES), F32)],
        scratch_shapes=[pltpu.VMEM((2, rows, d), F32), pltpu.VMEM((GLA_RANK, d), F32), pltpu.VMEM((db, d), BF16),
                        pltpu.SemaphoreType.DMA((2,)), pltpu.SemaphoreType.DMA(())],
        compiler_params=_cparams(("arbitrary",)),
        name="prep_w_in",
    )(wt, xs, g.reshape(1, d))


def kernel(x_prompt, x_sample, mem_prompt, state_gla, cache_swa_w128, cache_swa_w512, cache_swa_w2048, cache_mem_kv, g_norm, w_in, w_alpha2, b_alpha, g_gla_out, g_mem, w_mem_kv, w_proj_a, w_proj_b, w_proj_c, w_out, g_final):
    batch, seq, d = x_prompt.shape
    db, dec_seq, _ = x_sample.shape
    assert g_norm.shape[0] == 1 and dec_seq == 1

    xp = x_prompt.reshape(batch * seq, d)
    xs = x_sample.reshape(db, d)

    w_main, w_ga, zs, zas = _prep_w_in(jnp.swapaxes(w_in[0], 0, 1), xs, g_norm[0], rows=TILES["prep_rows"])
    wa_pad = jnp.pad(w_alpha2[0], ((0, LANES - GLA_RANK), (0, 0))).astype(BF16)
    wpa, wpb, wpc, wo = (w[0].astype(BF16) for w in (w_proj_a, w_proj_b, w_proj_c, w_out))

    cos_s, sin_s = _rope_tables(jnp.full((1,), PAST_LEN, jnp.int32))
    caches = (cache_swa_w128, cache_swa_w512, cache_swa_w2048)
    uas, ubs, ucs, gla_s, new_rows = _sample_mixers(
        zs, zas, wa_pad, b_alpha[0], g_gla_out[0], cos_s, sin_s, state_gla, caches, cache_mem_kv)
    y_sample = _final(uas, ubs, ucs, zs, xs, wpa, wpb, wpc, wo, g_final, tm=db).reshape(db, 1, d)

    z, za = _norm_matmul(xp, g_norm[0], w_main, w_ga, w_rows_out=True, out_dtype=BF16,
                         tm=TILES["proj_rows"], tn=TILES["proj_cols"])
    flat_caches = [c.reshape(db, w * KV_ROWS, HD) for c, (w, _) in zip(caches, SWA_PATTERNS)]
    ua, gla_p, *shifted = _gla_prompt(z, za, wa_pad, b_alpha[0], g_gla_out[0], batch, seq,
                                      (flat_caches, new_rows), t_blk=TILES["gla_rows"])
    swa_s = [o.reshape(c.shape) for o, c in zip(shifted, caches)]
    mem_kv = _norm_matmul(mem_prompt.reshape(batch * N_MEM, d), g_mem[0], w_mem_kv[0].astype(BF16),
                          tm=batch * N_MEM, tn=TILES["mem_cols"])
    cos_p, sin_p = _rope_tables_range(seq)
    swa_res = _swa_prompt(z, cos_p, sin_p, batch, seq)
    ub, kv_bufs = swa_res[0], swa_res[1:4]
    y_prompt = _final(ua, ub, None, z, xp, wpa, wpb, wpc, wo, g_final, tm=TILES["final_rows"], mem_kv=mem_kv,
                      rows_per_batch=seq).reshape(batch, seq, d)

    swa_p = [kv.reshape(1, batch, w, 2, N_HEADS, HD) for kv, (w, _) in zip(kv_bufs, SWA_PATTERNS)]
    mem_kv_prompt = mem_kv.reshape(1, batch, N_MEM, 2, N_HEADS, HD)
    return (y_prompt, y_sample, gla_p[None], swa_p[0], swa_p[1], swa_p[2], mem_kv_prompt,
            gla_s, swa_s[0], swa_s[1], swa_s[2])
```

```python
import functools

import numpy as np
import jax
import jax.numpy as jnp
from jax import lax
from jax.experimental import pallas as pl
from jax.experimental.pallas import tpu as pltpu

F32 = jnp.float32
BF16 = jnp.bfloat16

EPS = 1e-6
ROPE_THETA = 10000.0
NEG = -1e30
PAST_LEN = 16384

HD = 128
N_HEADS = 4
GLA_DV = 256
GLA_RANK = 16
GLA_TAU = 16.0
LOG2_E = 1.4426950408889634
GLA_CHUNK = 64
GLA_SUB = 16
GLA_UNROLL = 4
SWA_PATTERNS = ((128, 1), (512, 4), (2048, 16))
SWA_BLK = 128
SWA_STEP = 2048
SWA_UNROLL = 2
N_MEM = 256

LANES = 128
VMEM_LIMIT = 60000 * 1024
VMEM_LIMIT_FINAL = 62 * 1024 * 1024

TILES = dict(prep_rows=512, proj_rows=1024, proj_cols=2560, gla_rows=512, mem_cols=512, final_rows=256)

C_GT = 0
C_GQ, C_GK, C_GV, C_GR = 6144, 6656, 7168, 8192
C_SQ, C_SK, C_SV, C_SR = 9216, 10752, 12288, 13824
C_MQ, C_MR = 14336, 14848
N_MAIN = 15360


def _cparams(sem, vmem_limit=VMEM_LIMIT):
    return pltpu.CompilerParams(dimension_semantics=sem, vmem_limit_bytes=vmem_limit)


def _dot(a, b):
    return jnp.dot(a, b, preferred_element_type=F32)


def _dot_nt(a, b):
    return lax.dot_general(a, b, (((1,), (1,)), ((), ())), preferred_element_type=F32)


def _dot_tn(a, b):
    return lax.dot_general(a, b, (((0,), (0,)), ((), ())), preferred_element_type=F32)


def _silu(x):
    return x * jax.nn.sigmoid(x)


def _log_sigmoid(x):
    return jnp.minimum(x, 0.0) - jnp.log1p(jnp.exp(-jnp.abs(x)))


KV_ROWS = 2 * N_HEADS
SHIFT_ROWS = tuple((w - 1) * KV_ROWS for w, _ in SWA_PATTERNS)
SHIFT_OFFS = tuple(sum(SHIFT_ROWS[:g]) for g in range(len(SWA_PATTERNS)))


def _cache_shift_step(step, n_steps, ca, nr, co, stage, in_sem, out_sem, row_sem):
    n_grp = len(ca)
    n_batch = ca[0].shape[0]
    assert n_steps >= n_batch
    slot = lax.rem(step, 2)

    def in_copy(g, bb, sl):
        return pltpu.make_async_copy(ca[g].at[bb, pl.ds(KV_ROWS, SHIFT_ROWS[g])],
                                     stage.at[sl, pl.ds(SHIFT_OFFS[g], SHIFT_ROWS[g])], in_sem.at[sl, g])

    def out_copy(g, bb, sl):
        return pltpu.make_async_copy(stage.at[sl, pl.ds(SHIFT_OFFS[g], SHIFT_ROWS[g])],
                                     co[g].at[bb, pl.ds(0, SHIFT_ROWS[g])], out_sem.at[sl, g])

    def row_copy(g, bb, sl):
        return pltpu.make_async_copy(nr[g].at[bb], co[g].at[bb, pl.ds(SHIFT_ROWS[g], KV_ROWS)], row_sem.at[sl, g])

    def finish_writes(bb, sl):
        for g in range(n_grp):
            out_copy(g, bb, sl).wait()
            row_copy(g, bb, sl).wait()

    @pl.when(step == 0)
    def _():
        for g in range(n_grp):
            in_copy(g, 0, 0).start()

    @pl.when((step >= 1) & (step <= n_batch))
    def _():
        finish_writes(step - 1, 1 - slot)

    @pl.when(step + 1 < n_batch)
    def _():
        for g in range(n_grp):
            in_copy(g, step + 1, 1 - slot).start()

    @pl.when(step < n_batch)
    def _():
        for g in range(n_grp):
            in_copy(g, step, slot).wait()
            out_copy(g, step, slot).start()
            row_copy(g, step, slot).start()

    if n_steps == n_batch:
        @pl.when(step == n_steps - 1)
        def _():
            finish_writes(step, slot)


def _norm_matmul_kernel(*refs, has_extra, row_chunk, w_rows_out):
    mm = _dot_nt if w_rows_out else _dot
    if has_extra:
        x_ref, g_ref, w_ref, wx_ref, o_ref, ox_ref, h_ref = refs
    else:
        x_ref, g_ref, w_ref, o_ref, h_ref = refs
    tm = x_ref.shape[0]

    @pl.when(pl.program_id(1) == 0)
    def _():
        for c in range(tm // row_chunk):
            rs = slice(c * row_chunk, (c + 1) * row_chunk)
            x = x_ref[rs, :]
            y = x * lax.rsqrt(jnp.mean(x * x, axis=-1, keepdims=True) + EPS)
            h_ref[rs, :] = (y * g_ref[...]).astype(BF16)
            if has_extra:
                ox_ref[rs, :] = mm(h_ref[rs, :], wx_ref[...])

    o_ref[...] = mm(h_ref[...], w_ref[...]).astype(o_ref.dtype)


def _norm_matmul(x, g, w, wx=None, *, w_rows_out=False, out_dtype=F32, tm, tn):
    m, d = x.shape
    n = w.shape[0] if w_rows_out else w.shape[1]
    assert m % tm == 0 and n % tn == 0
    has_extra = wx is not None
    row_chunk = min(tm, 128)
    w_spec = pl.BlockSpec((tn, d), lambda i, j: (j, 0)) if w_rows_out else pl.BlockSpec((d, tn), lambda i, j: (0, j))
    in_specs = [pl.BlockSpec((tm, d), lambda i, j: (i, 0)),
                pl.BlockSpec((1, d), lambda i, j: (0, 0)),
                w_spec]
    out_specs = [pl.BlockSpec((tm, tn), lambda i, j: (i, j))]
    out_shape = [jax.ShapeDtypeStruct((m, n), out_dtype)]
    args = [x, g.reshape(1, d), w]
    if has_extra:
        nx = wx.shape[0] if w_rows_out else wx.shape[1]
        in_specs.append(pl.BlockSpec(wx.shape, lambda i, j: (0, 0)))
        out_specs.append(pl.BlockSpec((tm, nx), lambda i, j: (i, 0)))
        out_shape.append(jax.ShapeDtypeStruct((m, nx), F32))
        args.append(wx)
    res = pl.pallas_call(
        functools.partial(_norm_matmul_kernel, has_extra=has_extra, row_chunk=row_chunk, w_rows_out=w_rows_out),
        grid=(m // tm, n // tn),
        in_specs=in_specs, out_specs=out_specs, out_shape=out_shape,
        scratch_shapes=[pltpu.VMEM((tm, d), BF16)],
        compiler_params=_cparams(("arbitrary", "arbitrary")),
        name="norm_matmul",
    )(*args)
    return res if has_extra else res[0]


def _diag_select_matrix():
    rows = np.arange(GLA_SUB * HD)[:, None] // HD
    cols = np.arange(LANES)[None, :] % GLA_SUB
    return jnp.asarray((rows == cols).astype(np.float32), dtype=BF16)


def _gla_chunks(rows_list, zg_ref, ga_ref, wa_ref, ba_ref, gg_ref, em_ref, st_ref, u_ref):
    C, SUB, n_sub = GLA_CHUNK, GLA_SUB, GLA_CHUNK // GLA_SUB
    kw = N_HEADS * HD
    c_k, c_v, c_r = kw, 2 * kw, 2 * kw + N_HEADS * GLA_DV
    row = lax.broadcasted_iota(jnp.int32, (C, C), 0)
    col = lax.broadcasted_iota(jnp.int32, (C, C), 1)
    tri = (col <= row).astype(F32)
    row_c = lax.broadcasted_iota(jnp.int32, (C, HD), 0)
    half = SUB // 2
    half_row = lax.broadcasted_iota(jnp.int32, (half, HD), 0)
    lane_c = lax.broadcasted_iota(jnp.int32, (SUB, C), 1)

    xas = [_dot(ga_ref[rows, :].astype(BF16), wa_ref[...]) + ba_ref[...] for rows in rows_list]
    b_alls = [jnp.dot(tri, _log_sigmoid(xa) * (LOG2_E / GLA_TAU), preferred_element_type=F32,
                      precision=lax.Precision.HIGHEST) for xa in xas]

    chunks = []
    for rows, b_all in zip(rows_list, b_alls):
        a_off = []
        per_head = []
        for h in range(N_HEADS):
            q = zg_ref[rows, h * HD:(h + 1) * HD].astype(F32) * (HD ** -0.5)
            k = zg_ref[rows, c_k + h * HD:c_k + (h + 1) * HD].astype(F32)
            v = zg_ref[rows, c_v + h * GLA_DV:c_v + (h + 1) * GLA_DV]
            b = b_all[:, h * HD:(h + 1) * HD]
            for i in range(n_sub):
                if i == 0:
                    a_off.append(jnp.zeros((SUB, C), F32))
                else:
                    sl = slice(i * SUB, (i + 1) * SUB)
                    b_ref_row = b[i * SUB - 1:i * SUB, :]
                    qs = q[sl] * jnp.exp2(b[sl] - b_ref_row)
                    ks = k * jnp.exp2(jnp.where(row_c < i * SUB, b_ref_row - b, -jnp.inf))
                    a_off.append(_dot_nt(qs.astype(BF16), ks.astype(BF16)))
            st = st_ref[h]
            o_inter = _dot_nt((q * jnp.exp2(b)).astype(BF16), st.astype(BF16))
            b_end = b[C - 1:C, :]
            kd = k * jnp.exp2(b_end - b)
            st_ref[h] = st * jnp.exp2(b_end) + _dot_tn(v.astype(BF16), kd.astype(BF16))
            per_head.append((q, k, v, b, o_inter))
        chunks.append((rows, a_off, per_head))

    p_rows = []
    for _, _, per_head in chunks:
        for h in range(N_HEADS):
            q, k, _, b, _ = per_head[h]
            for i in range(n_sub):
                top, bot = slice(i * SUB, i * SUB + half), slice(i * SUB + half, (i + 1) * SUB)
                b_mid = b[i * SUB + half - 1:i * SUB + half, :]
                qs_bot = q[bot] * jnp.exp2(b[bot] - b_mid)
                ks_top = k[top] * jnp.exp2(b_mid - b[top])
                slabs = []
                for s in range(SUB):
                    rs = top if s < half else bot
                    s0 = s % half
                    e = jnp.where(half_row >= s0, b[rs] - b[rs][s0:s0 + 1, :], -jnp.inf)
                    diag = (q[rs] * k[rs][s0:s0 + 1, :]) * jnp.exp2(e)
                    if s < half:
                        slabs.append(jnp.concatenate([diag, qs_bot * ks_top[s0:s0 + 1, :]], axis=0))
                    else:
                        slabs.append(jnp.concatenate([jnp.zeros((half, HD), F32), diag], axis=0))
                p_rows.append(jnp.concatenate(slabs, axis=1).astype(BF16))
    r_all = _dot(jnp.concatenate(p_rows, axis=0), em_ref[...])

    for ci, (rows, a_off, per_head) in enumerate(chunks):
        for h in range(N_HEADS):
            _, _, v, _, o_inter = per_head[h]
            a_rows = []
            for i in range(n_sub):
                idx = h * n_sub + i
                r0 = (ci * N_HEADS * n_sub + idx) * SUB
                in_blk = (lane_c >= i * SUB) & (lane_c < (i + 1) * SUB)
                a_rows.append(a_off[idx] + jnp.where(in_blk, r_all[r0:r0 + SUB, :C], 0.0))
            a = jnp.concatenate(a_rows, axis=0)
            o = _dot(a.astype(BF16), v.astype(BF16)) + o_inter
            y = o * lax.rsqrt(jnp.mean(o * o, axis=-1, keepdims=True) + EPS)
            y = y * gg_ref[:, h * GLA_DV:(h + 1) * GLA_DV]
            gate = zg_ref[rows, c_r + h * GLA_DV:c_r + (h + 1) * GLA_DV].astype(F32)
            u_ref[rows, h * GLA_DV:(h + 1) * GLA_DV] = (y * _silu(gate)).astype(u_ref.dtype)


def _gla_kernel(*refs, n_shift, n_steps):
    zg_ref, ga_ref, wa_ref, ba_ref, gg_ref, em_ref = refs[:6]
    n_in = 6 + 2 * n_shift
    u_ref, sout_ref = refs[n_in:n_in + 2]
    st_ref = refs[n_in + 2 + n_shift]
    t_blk = pl.program_id(1)
    n_chunks = zg_ref.shape[0] // GLA_CHUNK
    assert n_chunks % GLA_UNROLL == 0

    if n_shift:
        stage, in_sem, out_sem, row_sem = refs[n_in + 3 + n_shift:]
        _cache_shift_step(pl.program_id(0) * pl.num_programs(1) + t_blk, n_steps, refs[6:6 + n_shift],
                          refs[6 + n_shift:n_in], refs[n_in + 2:n_in + 2 + n_shift], stage, in_sem, out_sem, row_sem)

    @pl.when(t_blk == 0)
    def _():
        st_ref[...] = jnp.zeros_like(st_ref)

    def chunks(c, carry):
        rows_list = [pl.ds(pl.multiple_of((c * GLA_UNROLL + u) * GLA_CHUNK, GLA_CHUNK), GLA_CHUNK)
                     for u in range(GLA_UNROLL)]
        _gla_chunks(rows_list, zg_ref, ga_ref, wa_ref, ba_ref, gg_ref, em_ref, st_ref, u_ref)
        return carry

    lax.fori_loop(0, n_chunks // GLA_UNROLL, chunks, 0)

    @pl.when(t_blk == pl.num_programs(1) - 1)
    def _():
        for h in range(N_HEADS):
            sout_ref[h] = st_ref[h].T


def _gla_prompt(z, za, wa_pad, b_alpha, g_gla_out, batch, seq, shift=None, *, t_blk):
    nt = seq // t_blk
    assert seq % t_blk == 0 and t_blk % GLA_CHUNK == 0
    kw = N_HEADS * HD
    vw = N_HEADS * GLA_DV
    gla_w = 2 * kw + 2 * vw
    assert C_GQ % gla_w == 0 and (C_GK, C_GV, C_GR) == (C_GQ + kw, C_GQ + 2 * kw, C_GQ + 2 * kw + vw)
    const = lambda b, t: (0, 0)
    in_specs = [pl.BlockSpec((t_blk, gla_w), lambda b, t: (b * nt + t, C_GQ // gla_w)),
                pl.BlockSpec((t_blk, LANES), lambda b, t: (b * nt + t, 0)),
                pl.BlockSpec((LANES, kw), const), pl.BlockSpec((1, kw), const), pl.BlockSpec((1, vw), const),
                pl.BlockSpec((GLA_SUB * HD, LANES), const)]
    out_specs = [pl.BlockSpec((t_blk, vw), lambda b, t: (b * nt + t, 0)),
                 pl.BlockSpec((None, N_HEADS, HD, GLA_DV), lambda b, t: (b, 0, 0, 0))]
    out_shape = [jax.ShapeDtypeStruct((batch * seq, vw), BF16),
                 jax.ShapeDtypeStruct((batch, N_HEADS, HD, GLA_DV), F32)]
    scratch = [pltpu.VMEM((N_HEADS, GLA_DV, HD), F32)]
    args = [z, za, wa_pad, b_alpha.reshape(1, kw), g_gla_out.reshape(1, vw), _diag_select_matrix()]
    n_shift = 0
    if shift is not None:
        caches, new_rows = shift
        n_shift = len(caches)
        any_spec = pl.BlockSpec(memory_space=pl.ANY)
        in_specs += [any_spec] * n_shift + [pl.BlockSpec(r.shape, lambda b, t: (0, 0, 0)) for r in new_rows]
        out_specs += [any_spec] * n_shift
        out_shape += [jax.ShapeDtypeStruct(c.shape, c.dtype) for c in caches]
        args += list(caches) + list(new_rows)
        scratch += [pltpu.VMEM((2, sum(SHIFT_ROWS), HD), F32)] + [pltpu.SemaphoreType.DMA((2, n_shift))] * 3
    return pl.pallas_call(
        functools.partial(_gla_kernel, n_shift=n_shift, n_steps=batch * nt),
        grid=(batch, nt),
        in_specs=in_specs, out_specs=out_specs, out_shape=out_shape, scratch_shapes=scratch,
        compiler_params=_cparams(("arbitrary", "arbitrary")),
        name="gla",
    )(*args)


def _rope_tables(pos):
    half = HD // 2
    inv = ROPE_THETA ** (-jnp.arange(half, dtype=F32) / half)
    ang = pos.astype(F32)[:, None] * inv[None, :]
    cos, sin = jnp.cos(ang), jnp.sin(ang)
    return jnp.concatenate([cos, cos], axis=1), jnp.concatenate([-sin, sin], axis=1)


def _rope_tables_range(n, blk=128):
    assert n % blk == 0
    half = HD // 2
    inv = ROPE_THETA ** (-jnp.arange(half, dtype=F32) / half)
    inv2 = jnp.concatenate([inv, inv])
    sign = jnp.concatenate([-jnp.ones((half,), F32), jnp.ones((half,), F32)])
    ang_a = (jnp.arange(n // blk, dtype=jnp.int32) * blk).astype(F32)[:, None] * inv2[None, :]
    ang_b = jnp.arange(blk, dtype=jnp.int32).astype(F32)[:, None] * inv2[None, :]
    ca, sa, cb, sb = jnp.cos(ang_a)[:, None], jnp.sin(ang_a)[:, None], jnp.cos(ang_b)[None], jnp.sin(ang_b)[None]
    return (ca * cb - sa * sb).reshape(n, HD), ((sa * cb + ca * sb) * sign).reshape(n, HD)


def _rope(x, cos2, sin2):
    return x * cos2 + pltpu.roll(x, HD // 2, axis=1) * sin2


def _swa_kernel(*refs):
    (q0, q1, q2, k0, k1, k2, v0, v1, v2, sr_ref, cos_ref, sin_ref,
     u_ref, kv0, kv1, kv2,
     q_s, k_s, v_s, o_s, l_s) = refs
    q_in, k_in, v_in = (q0, q1, q2), (k0, k1, k2), (v0, v1, v2)
    kv_out = (kv0, kv1, kv2)
    i = pl.program_id(2)
    T = SWA_STEP
    n_grp = len(SWA_PATTERNS)

    @pl.when(i == 0)
    def _():
        for g, (_, dil) in enumerate(SWA_PATTERNS):
            unit = dil * SWA_BLK
            k_s[g, T - unit:T, :] = jnp.zeros((unit, HD), F32)
            v_s[g, T - unit:T, :] = jnp.zeros((unit, HD), F32)

    @pl.when(i > 0)
    def _():
        for g, (_, dil) in enumerate(SWA_PATTERNS):
            unit = dil * SWA_BLK
            k_s[g, T - unit:T, :] = k_s[g, 2 * T - unit:2 * T, :]
            v_s[g, T - unit:T, :] = v_s[g, 2 * T - unit:2 * T, :]

    cos2, sin2 = cos_ref[...], sin_ref[...]
    for g in range(n_grp):
        q_s[g] = _rope(q_in[g][...].astype(F32), cos2, sin2) * (HD ** -0.5)
        k_s[g, T:2 * T, :] = _rope(k_in[g][...].astype(F32), cos2, sin2)
        v_s[g, T:2 * T, :] = v_in[g][...].astype(F32)

    qi = lax.broadcasted_iota(jnp.int32, (SWA_BLK, 2 * SWA_BLK), 0)
    ki = lax.broadcasted_iota(jnp.int32, (SWA_BLK, 2 * SWA_BLK), 1)
    delta = SWA_BLK + qi - ki

    def blocks(it, carry):
        todo = []
        for g, (win, dil) in enumerate(SWA_PATTERNS):
            for j in range(SWA_UNROLL):
                n = it * SWA_UNROLL + j
                unit = dil * SWA_BLK
                u = lax.shift_right_logical(n, dil.bit_length() - 1)
                r = lax.bitwise_and(n, dil - 1)

                def rows(start, size, dil=dil):
                    return pl.ds(start, size) if dil == 1 else pl.ds(start, size, stride=dil)

                q = q_s[g, rows(u * unit + r, SWA_BLK), :]
                kk = k_s[g, rows(T + (u - 1) * unit + r, 2 * SWA_BLK), :]
                s = _dot_nt(q.astype(BF16), kk.astype(BF16))
                ki_min = jnp.where((i * (T // unit) + u) == 0, SWA_BLK, 0)
                valid = (delta >= 0) & (delta <= win // dil) & (ki >= ki_min)
                todo.append((g, rows(u * unit + r, SWA_BLK), rows(T + (u - 1) * unit + r, 2 * SWA_BLK), s, valid))
        soft = []
        for g, q_rows, kv_rows, s, valid in todo:
            s = jnp.where(valid, s, NEG)
            m = jnp.max(s, axis=-1, keepdims=True)
            p = jnp.exp(s - m)
            den = jnp.sum(p, axis=-1, keepdims=True)
            soft.append((g, q_rows, kv_rows, p.astype(BF16), den, m + jnp.log(den)))
        for g, q_rows, kv_rows, p, den, lse in soft:
            o_s[g, q_rows, :] = _dot(p, v_s[g, kv_rows, :].astype(BF16)) / den
            l_s[g, q_rows, :] = jnp.broadcast_to(lse, (SWA_BLK, HD))
        return carry

    lax.fori_loop(0, T // SWA_BLK // SWA_UNROLL, blocks, 0)

    rc = 256

    def merge(c, carry):
        r0 = pl.multiple_of(c * rc, rc)
        rr = pl.ds(r0, rc)
        ls = [l_s[g, rr, :] for g in range(n_grp)]
        mx = functools.reduce(jnp.maximum, ls)
        es = [jnp.exp(l - mx) for l in ls]
        tot = functools.reduce(lambda a, b: a + b, es)
        ob = functools.reduce(lambda a, b: a + b, [(es[g] / tot) * o_s[g, rr, :] for g in range(n_grp)])
        u_ref[rr, :] = (ob * _silu(sr_ref[rr, :].astype(F32))).astype(u_ref.dtype)
        return carry

    lax.fori_loop(0, T // rc, merge, 0)

    @pl.when(i == pl.num_programs(2) - 1)
    def _():
        for g, (win, _) in enumerate(SWA_PATTERNS):
            head = pl.program_id(1)
            kv_out[g][pl.ds(head, win, stride=KV_ROWS), :] = k_s[g, 2 * T - win:2 * T, :]
            kv_out[g][pl.ds(N_HEADS + head, win, stride=KV_ROWS), :] = v_s[g, 2 * T - win:2 * T, :]


def _swa_prompt(z, cos2, sin2, batch, seq):
    T = SWA_STEP
    assert seq % T == 0 and all(w <= T for w, _ in SWA_PATTERNS)
    nt = seq // T
    w_grp = N_HEADS * HD

    def zspec(col0, g):
        cb0 = (col0 + g * w_grp) // HD
        return pl.BlockSpec((T, HD), lambda b, j, i: (b * nt + i, cb0 + j))

    in_specs = ([zspec(C_SQ, g) for g in range(3)] + [zspec(C_SK, g) for g in range(3)]
                + [zspec(C_SV, g) for g in range(3)] + [zspec(C_SR, 0)]
                + [pl.BlockSpec((T, HD), lambda b, j, i: (i, 0))] * 2)
    buf_specs = [pl.BlockSpec((None, w * KV_ROWS, HD), lambda b, j, i: (b, 0, 0), pipeline_mode=pl.Buffered(1))
                 for w, _ in SWA_PATTERNS]
    buf_shapes = [jax.ShapeDtypeStruct((batch, w * KV_ROWS, HD), F32) for w, _ in SWA_PATTERNS]
    return pl.pallas_call(
        _swa_kernel,
        grid=(batch, N_HEADS, nt),
        in_specs=in_specs,
        out_specs=[pl.BlockSpec((T, HD), lambda b, j, i: (b * nt + i, j))] + buf_specs,
        out_shape=[jax.ShapeDtypeStruct((batch * seq, w_grp), BF16)] + buf_shapes,
        scratch_shapes=[pltpu.VMEM((3, T, HD), F32), pltpu.VMEM((3, 2 * T, HD), F32),
                        pltpu.VMEM((3, 2 * T, HD), F32), pltpu.VMEM((3, T, HD), F32),
                        pltpu.VMEM((3, T, HD), F32)],
        compiler_params=_cparams(("arbitrary", "arbitrary", "arbitrary")),
        name="swa",
    )(*([z] * 10), cos2, sin2)


def _final_kernel(*refs, fuse_mem):
    if fuse_mem:
        ua_ref, ub_ref, mq_ref, mr_ref, kv_ref, gt_ref, x_ref, wa_ref, wb_ref, wc_ref, wo_ref, gf_ref, y_ref = refs
    else:
        ua_ref, ub_ref, uc_ref, gt_ref, x_ref, wa_ref, wb_ref, wc_ref, wo_ref, gf_ref, y_ref = refs
    d = x_ref.shape[1]
    w = N_HEADS * HD
    if fuse_mem:
        scores = [_dot_nt((mq_ref[:, h * HD:(h + 1) * HD].astype(F32) * (HD ** -0.5)).astype(BF16),
                          kv_ref[:, h * HD:(h + 1) * HD].astype(BF16)) for h in range(N_HEADS)]
    ya = _dot(ua_ref[...].astype(BF16), wa_ref[...])
    yb = _dot(ub_ref[...].astype(BF16), wb_ref[...])
    if fuse_mem:
        ucs = []
        for h, s in enumerate(scores):
            e = jnp.exp(s - jnp.max(s, axis=-1, keepdims=True))
            p = e / jnp.sum(e, axis=-1, keepdims=True)
            o = _dot(p.astype(BF16), kv_ref[:, w + h * HD:w + (h + 1) * HD].astype(BF16))
            ucs.append((o * _silu(mr_ref[:, h * HD:(h + 1) * HD].astype(F32))).astype(BF16))
        uc = jnp.concatenate(ucs, axis=1)
    else:
        uc = uc_ref[...].astype(BF16)
    yc = _dot(uc, wc_ref[...])
    mix = (jax.nn.sigmoid(gt_ref[:, 0:d].astype(F32)) * ya + jax.nn.sigmoid(gt_ref[:, d:2 * d].astype(F32)) * yb
           + jax.nn.sigmoid(gt_ref[:, 2 * d:3 * d].astype(F32)) * yc)
    xo = x_ref[...] + _dot(mix.astype(BF16), wo_ref[...])
    y = xo * lax.rsqrt(jnp.mean(xo * xo, axis=-1, keepdims=True) + EPS)
    y_ref[...] = y * gf_ref[...]


def _final(ua, ub, uc, z, x, wa, wb, wc, wo, g_final, *, tm, mem_kv=None, rows_per_batch=None):
    m, d = x.shape
    w = N_HEADS * HD
    const = lambda i: (0, 0)
    resident = dict(pipeline_mode=pl.Buffered(1))
    fuse_mem = uc is None
    if fuse_mem:
        assert rows_per_batch % tm == 0
        mem_specs = [pl.BlockSpec((tm, w), lambda i: (i, C_MQ // w)), pl.BlockSpec((tm, w), lambda i: (i, C_MR // w)),
                     pl.BlockSpec((N_MEM, 2 * w), lambda i: (i // (rows_per_batch // tm), 0))]
        mem_args = [z, z, mem_kv]
    else:
        mem_specs = [pl.BlockSpec((tm, uc.shape[1]), lambda i: (i, 0))]
        mem_args = [uc]
    return pl.pallas_call(
        functools.partial(_final_kernel, fuse_mem=fuse_mem),
        grid=(m // tm,),
        in_specs=[pl.BlockSpec((tm, ua.shape[1]), lambda i: (i, 0)),
                  pl.BlockSpec((tm, ub.shape[1]), lambda i: (i, 0))] + mem_specs + [
                  pl.BlockSpec((tm, 3 * d), lambda i: (i, C_GT // (3 * d))),
                  pl.BlockSpec((tm, d), lambda i: (i, 0)),
                  pl.BlockSpec(wa.shape, const, **resident),
                  pl.BlockSpec(wb.shape, const, **resident),
                  pl.BlockSpec(wc.shape, const, **resident),
                  pl.BlockSpec(wo.shape, const, **resident),
                  pl.BlockSpec((1, d), const)],
        out_specs=pl.BlockSpec((tm, d), lambda i: (i, 0)),
        out_shape=jax.ShapeDtypeStruct((m, d), F32),
        compiler_params=_cparams(("arbitrary",), VMEM_LIMIT_FINAL),
        name="final",
    )(ua, ub, *mem_args, z, x, wa, wb, wc, wo, g_final.reshape(1, d))


def _heads_rows(row, col0):
    return jnp.concatenate([row[:, col0 + h * HD:col0 + (h + 1) * HD] for h in range(N_HEADS)], axis=0)


def _decode_attention(q4, kk, vv, k_new=None, v_new=None):
    s = jnp.sum(kk * q4[None], axis=-1, keepdims=True)
    m = jnp.max(s, axis=0)
    if k_new is not None:
        s_new = jnp.sum(k_new * q4, axis=-1, keepdims=True)
        m = jnp.maximum(m, s_new)
    p = jnp.exp(s - m[None])
    den = jnp.sum(p, axis=0)
    acc = jnp.sum(p * vv, axis=0)
    if k_new is not None:
        p_new = jnp.exp(s_new - m)
        den = den + p_new
        acc = acc + p_new * v_new
    return acc / den, m + jnp.log(den)


def _sample_kernel(*refs):
    (z_ref, za_ref, wa_ref, ba_ref, gg_ref, cos_ref, sin_ref, st_ref, cg0, cg1, cg2, cm_ref,
     ua_ref, ub_ref, uc_ref, sout_ref, nr0, nr1, nr2) = refs
    cg, nr = (cg0, cg1, cg2), (nr0, nr1, nr2)
    b = pl.program_id(0)
    zrow = z_ref[pl.ds(b, 1), :]

    ga8 = jnp.broadcast_to(za_ref[pl.ds(b, 1), :], (8, LANES))
    xa = _dot(ga8.astype(BF16), wa_ref[...])[0:1, :] + ba_ref[...]
    a_row = jnp.exp(_log_sigmoid(xa) / GLA_TAU)
    eye = (lax.broadcasted_iota(jnp.int32, (HD, HD), 0) == lax.broadcasted_iota(jnp.int32, (HD, HD), 1))

    def col(row_vec):
        return jnp.sum(jnp.where(eye, jnp.broadcast_to(row_vec, (HD, HD)), 0.0), axis=1, keepdims=True)

    for h in range(N_HEADS):
        q = zrow[:, C_GQ + h * HD:C_GQ + (h + 1) * HD] * (HD ** -0.5)
        k = zrow[:, C_GK + h * HD:C_GK + (h + 1) * HD]
        v = zrow[:, C_GV + h * GLA_DV:C_GV + (h + 1) * GLA_DV]
        s_new = col(a_row[:, h * HD:(h + 1) * HD]) * st_ref[h] + col(k) * v
        sout_ref[h] = s_new
        o = jnp.sum(col(q) * s_new, axis=0, keepdims=True)
        y = o * lax.rsqrt(jnp.mean(o * o, axis=-1, keepdims=True) + EPS)
        y = y * gg_ref[:, h * GLA_DV:(h + 1) * GLA_DV]
        gate = zrow[:, C_GR + h * GLA_DV:C_GR + (h + 1) * GLA_DV]
        ua_ref[:, h * GLA_DV:(h + 1) * GLA_DV] = y * _silu(gate)

    cos2, sin2 = cos_ref[...], sin_ref[...]
    outs, lses = [], []
    w_grp = N_HEADS * HD
    for g in range(3):
        q4 = _rope(_heads_rows(zrow, C_SQ + g * w_grp), cos2, sin2) * (HD ** -0.5)
        k4 = _rope(_heads_rows(zrow, C_SK + g * w_grp), cos2, sin2)
        v4 = _heads_rows(zrow, C_SV + g * w_grp)
        nr[g][0:N_HEADS, :] = k4
        nr[g][N_HEADS:KV_ROWS, :] = v4
        o, lse = _decode_attention(q4, cg[g][:, 0:N_HEADS, :], cg[g][:, N_HEADS:KV_ROWS, :], k4, v4)
        outs.append(o)
        lses.append(lse)
    mx = functools.reduce(jnp.maximum, lses)
    es = [jnp.exp(l - mx) for l in lses]
    tot = es[0] + es[1] + es[2]
    ob = (es[0] / tot) * outs[0] + (es[1] / tot) * outs[1] + (es[2] / tot) * outs[2]
    for h in range(N_HEADS):
        gate = zrow[:, C_SR + h * HD:C_SR + (h + 1) * HD]
        ub_ref[:, h * HD:(h + 1) * HD] = ob[h:h + 1, :] * _silu(gate)

    qm = _heads_rows(zrow, C_MQ) * (HD ** -0.5)
    oc, _ = _decode_attention(qm, cm_ref[:, 0:N_HEADS, :], cm_ref[:, N_HEADS:KV_ROWS, :])
    for h in range(N_HEADS):
        gate = zrow[:, C_MR + h * HD:C_MR + (h + 1) * HD]
        uc_ref[:, h * HD:(h + 1) * HD] = oc[h:h + 1, :] * _silu(gate)


def _sample_mixers(z, za, wa_pad, b_alpha, g_gla_out, cos2, sin2, state, caches, cache_mem):
    db = z.shape[0]
    kw, vw, w = N_HEADS * HD, N_HEADS * GLA_DV, N_HEADS * HD
    const2 = lambda b: (0, 0)
    gathered, gather_specs = [], []
    for c, (win, dil) in zip(caches, SWA_PATTERNS):
        assert c.shape == (1, db, win, 2, N_HEADS, HD) and PAST_LEN >= win
        n_keys = win // dil
        gathered.append(c.reshape(db, n_keys, dil * KV_ROWS, HD))
        gather_specs.append(pl.BlockSpec((None, n_keys, KV_ROWS, HD), lambda b: (b, 0, 0, 0)))
    row3 = lambda n: pl.BlockSpec((None, 1, n), lambda b: (b, 0, 0))
    new_spec = pl.BlockSpec((None, KV_ROWS, HD), lambda b: (b, 0, 0))
    res = pl.pallas_call(
        _sample_kernel,
        grid=(db,),
        in_specs=[pl.BlockSpec(z.shape, const2), pl.BlockSpec(za.shape, const2),
                  pl.BlockSpec(wa_pad.shape, const2), pl.BlockSpec((1, kw), const2),
                  pl.BlockSpec((1, vw), const2), pl.BlockSpec((1, HD), const2), pl.BlockSpec((1, HD), const2),
                  pl.BlockSpec((None, None, N_HEADS, HD, GLA_DV), lambda b: (0, b, 0, 0, 0))]
                 + gather_specs
                 + [pl.BlockSpec((None, N_MEM, KV_ROWS, HD), lambda b: (b, 0, 0, 0))],
        out_specs=[row3(vw), row3(w), row3(w),
                   pl.BlockSpec((None, None, N_HEADS, HD, GLA_DV), lambda b: (0, b, 0, 0, 0))] + [new_spec] * 3,
        out_shape=[jax.ShapeDtypeStruct((db, 1, vw), F32), jax.ShapeDtypeStruct((db, 1, w), F32),
                   jax.ShapeDtypeStruct((db, 1, w), F32), jax.ShapeDtypeStruct(state.shape, F32)]
                  + [jax.ShapeDtypeStruct((db, KV_ROWS, HD), F32)] * 3,
        compiler_params=_cparams(("arbitrary",)),
        name="sample_mixers",
    )(z, za, wa_pad, b_alpha.reshape(1, kw), g_gla_out.reshape(1, vw), cos2, sin2, state,
      *gathered, cache_mem.reshape(db, N_MEM, KV_ROWS, HD))
    ua, ub, uc, s_out = res[:4]
    return ua.reshape(db, vw), ub.reshape(db, w), uc.reshape(db, w), s_out, res[4:]


def _prep_w_in_kernel(wt_ref, xs_ref, g_ref, wm_ref, wg_ref, zs_ref, zas_ref, buf, ga_buf, hs_ref, sem, ga_sem,
                      *, c_ga, c_gt, rows):
    i = pl.program_id(0)
    n_gt = (wt_ref.shape[0] - c_gt) // rows
    n_lo = c_ga // rows

    def fetch(ii, slot):
        src = jnp.where(ii < n_gt, c_gt + ii * rows,
                        jnp.where(ii < n_gt + n_lo, (ii - n_gt) * rows, c_ga + GLA_RANK + (ii - n_gt - n_lo) * rows))
        return pltpu.make_async_copy(wt_ref.at[pl.ds(pl.multiple_of(src, GLA_RANK), rows)], buf.at[slot], sem.at[slot])

    ga_copy = pltpu.make_async_copy(wt_ref.at[pl.ds(c_ga, GLA_RANK)], ga_buf, ga_sem)

    @pl.when(i == 0)
    def _():
        fetch(0, 0).start()
        ga_copy.start()

    @pl.when(i + 1 < pl.num_programs(0))
    def _():
        fetch(i + 1, lax.rem(i + 1, 2)).start()

    @pl.when(i == 0)
    def _():
        x = xs_ref[...]
        y = x * lax.rsqrt(jnp.mean(x * x, axis=-1, keepdims=True) + EPS)
        hs_ref[...] = (y * g_ref[...]).astype(BF16)
        ga_copy.wait()
        wg_ref[0:GLA_RANK, :] = ga_buf[...].astype(BF16)
        wg_ref[GLA_RANK:, :] = jnp.zeros((LANES - GLA_RANK, wg_ref.shape[1]), BF16)
        zas_ref[...] = _dot_nt(hs_ref[...], wg_ref[...])

    slot = lax.rem(i, 2)
    fetch(i, slot).wait()
    wm_ref[...] = buf[slot].astype(BF16)
    zs_ref[...] = _dot_nt(hs_ref[...], wm_ref[...])


def _prep_w_in(wt, xs, g, *, rows):
    n_all, d = wt.shape
    db = xs.shape[0]
    c_ga = 2 * N_HEADS * HD + 2 * N_HEADS * GLA_DV
    c_gt = n_all - 3 * d
    assert n_all - GLA_RANK == N_MAIN and xs.shape[1] == d
    assert (n_all - c_gt) % rows == 0 and c_ga % rows == 0 and (c_gt - c_ga - GLA_RANK) % rows == 0
    const = lambda i: (0, 0)
    return pl.pallas_call(
        functools.partial(_prep_w_in_kernel, c_ga=c_ga, c_gt=c_gt, rows=rows),
        grid=(N_MAIN // rows,),
        in_specs=[pl.BlockSpec(memory_space=pl.ANY), pl.BlockSpec((db, d), const), pl.BlockSpec((1, d), const)],
        out_specs=[pl.BlockSpec((rows, d), lambda i: (i, 0)), pl.BlockSpec((LANES, d), const),
                   pl.BlockSpec((db, rows), lambda i: (0, i)), pl.BlockSpec((db, LANES), const)],
        out_shape=[jax.ShapeDtypeStruct((N_MAIN, d), BF16), jax.ShapeDtypeStruct((LANES, d), BF16),
                   jax.ShapeDtypeStruct((db, N_MAIN), F32), jax.ShapeDtypeStruct((db, LANES), F32)],
        scratch_shapes=[pltpu.VMEM((2, rows, d), F32), pltpu.VMEM((GLA_RANK, d), F32), pltpu.VMEM((db, d), BF16),
                        pltpu.SemaphoreType.DMA((2,)), pltpu.SemaphoreType.DMA(())],
        compiler_params=_cparams(("arbitrary",)),
        name="prep_w_in",
    )(wt, xs, g.reshape(1, d))


def kernel(x_prompt, x_sample, mem_prompt, state_gla, cache_swa_w128, cache_swa_w512, cache_swa_w2048, cache_mem_kv, g_norm, w_in, w_alpha2, b_alpha, g_gla_out, g_mem, w_mem_kv, w_proj_a, w_proj_b, w_proj_c, w_out, g_final):
    batch, seq, d = x_prompt.shape
    db, dec_seq, _ = x_sample.shape
    assert g_norm.shape[0] == 1 and dec_seq == 1

    xp = x_prompt.reshape(batch * seq, d)
    xs = x_sample.reshape(db, d)

    w_main, w_ga, zs, zas = _prep_w_in(jnp.swapaxes(w_in[0], 0, 1), xs, g_norm[0], rows=TILES["prep_rows"])
    wa_pad = jnp.pad(w_alpha2[0], ((0, LANES - GLA_RANK), (0, 0))).astype(BF16)
    wpa, wpb, wpc, wo = (w[0].astype(BF16) for w in (w_proj_a, w_proj_b, w_proj_c, w_out))

    cos_s, sin_s = _rope_tables(jnp.full((1,), PAST_LEN, jnp.int32))
    caches = (cache_swa_w128, cache_swa_w512, cache_swa_w2048)
    uas, ubs, ucs, gla_s, new_rows = _sample_mixers(
        zs, zas, wa_pad, b_alpha[0], g_gla_out[0], cos_s, sin_s, state_gla, caches, cache_mem_kv)
    y_sample = _final(uas, ubs, ucs, zs, xs, wpa, wpb, wpc, wo, g_final, tm=db).reshape(db, 1, d)

    z, za = _norm_matmul(xp, g_norm[0], w_main, w_ga, w_rows_out=True, out_dtype=BF16,
                         tm=TILES["proj_rows"], tn=TILES["proj_cols"])
    flat_caches = [c.reshape(db, w * KV_ROWS, HD) for c, (w, _) in zip(caches, SWA_PATTERNS)]
    ua, gla_p, *shifted = _gla_prompt(z, za, wa_pad, b_alpha[0], g_gla_out[0], batch, seq,
                                      (flat_caches, new_rows), t_blk=TILES["gla_rows"])
    swa_s = [o.reshape(c.shape) for o, c in zip(shifted, caches)]
    mem_kv = _norm_matmul(mem_prompt.reshape(batch * N_MEM, d), g_mem[0], w_mem_kv[0].astype(BF16),
                          tm=batch * N_MEM, tn=TILES["mem_cols"])
    cos_p, sin_p = _rope_tables_range(seq)
    swa_res = _swa_prompt(z, cos_p, sin_p, batch, seq)
    ub, kv_bufs = swa_res[0], swa_res[1:4]
    y_prompt = _final(ua, ub, None, z, xp, wpa, wpb, wpc, wo, g_final, tm=TILES["final_rows"], mem_kv=mem_kv,
                      rows_per_batch=seq).reshape(batch, seq, d)

    swa_p = [kv.reshape(1, batch, w, 2, N_HEADS, HD) for kv, (w, _) in zip(kv_bufs, SWA_PATTERNS)]
    mem_kv_prompt = mem_kv.reshape(1, batch, N_MEM, 2, N_HEADS, HD)
    return (y_prompt, y_sample, gla_p[None], swa_p[0], swa_p[1], swa_p[2], mem_kv_prompt,
            gla_s, swa_s[0], swa_s[1], swa_s[2])
```
